```python
import jax
import jax.numpy as jnp
from jax import lax
import numpy as np

D_MODEL = 1024
BATCH = 8
SEQ = 2048
DEPTH = 4
DEC_BATCH = 128
DEC_SEQ = 1
PAST_LEN = 8192
PAGE_SIZE = 128

HEAD_DIM = 64
ROT_DIM = HEAD_DIM // 4
ROPE_THETA = 500000.0
ATTN_SCALE = HEAD_DIM ** -0.5
RMS_EPS = 1e-6
BLOCK_Q = 128

N_MIXERS = 3
LAYER_KIND = tuple(i % N_MIXERS for i in range(DEPTH))
LAYER_SLOT = tuple(LAYER_KIND[:i].count(LAYER_KIND[i]) for i in range(DEPTH))
N_LAYERS_A = LAYER_KIND.count(0)
N_LAYERS_B = LAYER_KIND.count(1)
N_LAYERS_C = LAYER_KIND.count(2)

N_Q_A = 16
N_KV_A = 4
WINDOW_A = 128

B_PATTERNS = ((128, 1), (512, 4), (2048, 16))
N_GROUPS_B = len(B_PATTERNS)
B_HEADS_PER_GROUP = 4
B_WIDTH = N_GROUPS_B * B_HEADS_PER_GROUP * HEAD_DIM

N_Q_C = 16
N_KV_C = 4
CMP_LEN = 32
CMP_STRIDE = 16
CMP_HID = 64
SEL_BLOCK = 64
N_SEL = 16
N_FORCED = 3
WINDOW_C = 512
NSA_Q_BLOCK = 32
C_KV_WIDTH = 2 * N_KV_C * HEAD_DIM
C_IN_WIDTH = N_Q_C * HEAD_DIM + 3 * C_KV_WIDTH + 3 * N_Q_C

N_EXPERTS = 64
TOP_K = 8
N_EXPERT_GROUPS = 8
TOPK_GROUPS = 4
D_EXPERT = 256
D_SHARED = 256
ROUTED_SCALE = 2.5
MOE_BLOCK = 128

kernel_name = 'hybrid_swa_dilated_nsa_moe_adaln_step'


def rms_norm(x, g):
    xf = x.astype(jnp.float32)
    y = xf * lax.rsqrt(jnp.mean(xf * xf, axis=-1, keepdims=True) + RMS_EPS)
    return (y * g.astype(jnp.float32)).astype(x.dtype)


def sublayer_mods(c, w, b):
    return (jax.nn.silu(c) @ w + b).reshape(c.shape[0], 6, -1)


def modulate(x, shift, scale):
    return x * (1 + scale[:, None, :]) + shift[:, None, :]


def rope(x, pos):
    half = ROT_DIM // 2
    inv_freq = ROPE_THETA ** (-jnp.arange(half, dtype=jnp.float32) / half)
    ang = pos.astype(jnp.float32)[:, None] * inv_freq
    cos, sin = jnp.cos(ang)[:, None, :], jnp.sin(ang)[:, None, :]
    xr = x[..., :ROT_DIM].astype(jnp.float32)
    x1, x2 = xr[..., :half], xr[..., half:]
    rot = jnp.concatenate([x1 * cos - x2 * sin, x2 * cos + x1 * sin], axis=-1).astype(x.dtype)
    return jnp.concatenate([rot, x[..., ROT_DIM:]], axis=-1)


def rope_kv(kv, pos):
    return jnp.stack([rope(kv[:, :, 0], pos), kv[:, :, 1]], axis=2)


def masked_softmax(s, mask, sink=None):
    s = jnp.where(mask, s, -jnp.inf)
    m = jnp.max(s, axis=-1, keepdims=True)
    if sink is not None:
        m = jnp.maximum(m, sink)
    m = jnp.where(jnp.isfinite(m), m, 0.0)
    e = jnp.exp(s - m)
    den = jnp.sum(e, axis=-1, keepdims=True)
    if sink is not None:
        den = den + jnp.exp(sink - m)
    p = e / jnp.maximum(den, 1e-30)
    return p, (m + jnp.log(den))[..., 0]


def attend(q, k, v, mask, sink=None):
    s = jnp.einsum('bqhgd,bkhd->bhgqk', q, k, preferred_element_type=jnp.float32) * ATTN_SCALE
    p, lse = masked_softmax(s, mask, sink)
    return jnp.einsum('bhgqk,bkhd->bqhgd', p.astype(v.dtype), v), lse


def attend_gathered(q, k, v, mask):
    s = jnp.einsum('bqhgd,bhqnd->bhgqn', q, k, preferred_element_type=jnp.float32) * ATTN_SCALE
    p, lse = masked_softmax(s, mask)
    return jnp.einsum('bhgqn,bhqnd->bqhgd', p.astype(v.dtype), v), lse


def banded_attention(q, k, v, window, sink=None):
    B_, S = q.shape[:2]
    blk = min(BLOCK_Q, S)
    n_blk = -(-S // blk)
    s_pad = n_blk * blk
    halo = -(-window // blk) * blk
    qp = jnp.pad(q, ((0, 0), (0, s_pad - S), (0, 0), (0, 0), (0, 0)))
    kp = jnp.pad(k, ((0, 0), (halo, s_pad - S), (0, 0), (0, 0)))
    vp = jnp.pad(v, ((0, 0), (halo, s_pad - S), (0, 0), (0, 0)))

    def one_block(i):
        start = i * blk
        qb = lax.dynamic_slice_in_dim(qp, start, blk, axis=1)
        kb = lax.dynamic_slice_in_dim(kp, start, halo + blk, axis=1)
        vb = lax.dynamic_slice_in_dim(vp, start, halo + blk, axis=1)
        qi = start + jnp.arange(blk)
        kj = start - halo + jnp.arange(halo + blk)
        diff = qi[:, None] - kj[None, :]
        mask = (diff >= 0) & (diff <= window) & (kj >= 0)[None, :]
        return attend(qb, kb, vb, mask, sink)

    o, lse = lax.map(one_block, jnp.arange(n_blk))
    o = jnp.moveaxis(o, 0, 1).reshape((B_, s_pad) + q.shape[2:])[:, :S]
    lse = jnp.moveaxis(lse, 0, -2)
    lse = lse.reshape(lse.shape[:3] + (s_pad,))[..., :S]
    return o, lse


def window_attend_with_buffer(q, kv, buf, window, sink=None):
    Lb, T = buf.shape[1], kv.shape[1]
    allkv = jnp.concatenate([buf, kv], axis=1)
    kpos = PAST_LEN - Lb + jnp.arange(Lb + T)
    qpos = PAST_LEN + jnp.arange(T)
    diff = qpos[:, None] - kpos[None, :]
    mask = (diff >= 0) & (diff <= window)
    o, lse = attend(q, allkv[:, :, 0], allkv[:, :, 1], mask, sink)
    return o, lse, allkv[:, -min(window, PAST_LEN + T):]


def mixer_a(h, pos, w_qkv, w_o, sink, past=None):
    B_, T = h.shape[:2]
    G = N_Q_A // N_KV_A
    nq = N_Q_A * HEAD_DIM
    y = h @ w_qkv
    q = rope(y[..., :nq].reshape(B_, T, N_Q_A, HEAD_DIM), pos).reshape(B_, T, N_KV_A, G, HEAD_DIM)
    kv = rope_kv(y[..., nq:].reshape(B_, T, 2, N_KV_A, HEAD_DIM), pos)
    sink_b = sink.astype(jnp.float32).reshape(1, N_KV_A, G, 1, 1)
    if past is None:
        o, _ = banded_attention(q, kv[:, :, 0], kv[:, :, 1], WINDOW_A, sink_b)
        new = kv[:, -min(WINDOW_A, T):]
    else:
        o, _, new = window_attend_with_buffer(q, kv, past, WINDOW_A, sink_b)
    return o.reshape(B_, T, -1) @ w_o, new


def dilated_prompt(q, k, v, win, dil):
    B_, S, H = q.shape[:3]
    L = S // dil

    def fold(x):
        return x.reshape(B_, L, dil, H, HEAD_DIM).transpose(0, 2, 1, 3, 4).reshape(B_ * dil, L, H, HEAD_DIM)

    o, lse = banded_attention(fold(q)[:, :, :, None], fold(k), fold(v), win // dil)
    o = o[:, :, :, 0].reshape(B_, dil, L, H, HEAD_DIM).transpose(0, 2, 1, 3, 4).reshape(B_, S, H, HEAD_DIM)
    lse = lse[:, :, 0].reshape(B_, dil, H, L).transpose(0, 3, 1, 2).reshape(B_, S, H)
    return o, lse


def dilated_sample(q, kv_new, buf, win, dil):
    Lb, T = buf.shape[1], kv_new.shape[1]
    allkv = jnp.concatenate([buf, kv_new], axis=1)
    n_taps = win // dil + 1
    idx = Lb + jnp.arange(T)[:, None] - dil * jnp.arange(n_taps)[None, :]
    valid = idx >= 0
    g = allkv[:, jnp.maximum(idx, 0)]
    k = g[:, :, :, 0].transpose(0, 3, 1, 2, 4)
    v = g[:, :, :, 1].transpose(0, 3, 1, 2, 4)
    o, lse = attend_gathered(q[:, :, :, None], k, v, valid)
    return o[:, :, :, 0], lse[:, :, 0].transpose(0, 2, 1), allkv[:, -min(win, PAST_LEN + T):]


def mixer_b(h, pos, w_qkv, w_o, past=None):
    B_, T = h.shape[:2]
    hpg = B_HEADS_PER_GROUP
    y = (h @ w_qkv).reshape(B_, T, 3, N_GROUPS_B * hpg, HEAD_DIM)
    q, k, v = rope(y[:, :, 0], pos), rope(y[:, :, 1], pos), y[:, :, 2]
    outs, lses, news = [], [], []
    for g, (win, dil) in enumerate(B_PATTERNS):
        hs = slice(g * hpg, (g + 1) * hpg)
        kv_g = jnp.stack([k[:, :, hs], v[:, :, hs]], axis=2)
        if past is None:
            o, lse = dilated_prompt(q[:, :, hs], k[:, :, hs], v[:, :, hs], win, dil)
            new = kv_g[:, -min(win, T):]
        else:
            o, lse, new = dilated_sample(q[:, :, hs], kv_g, past[g], win, dil)
        outs.append(o)
        lses.append(lse)
        news.append(new)
    alpha = jax.nn.softmax(jnp.stack(lses), axis=0)
    o = jnp.concatenate([(alpha[g][..., None] * outs[g]).astype(h.dtype) for g in range(N_GROUPS_B)], axis=2)
    return o.reshape(B_, T, -1) @ w_o, tuple(news)


def half_block_proj(rows, w1h):
    B_, L = rows.shape[:2]
    n_half = -(-L // CMP_STRIDE)
    if n_half * CMP_STRIDE != L:
        rows = jnp.pad(rows, ((0, 0), (0, n_half * CMP_STRIDE - L), (0, 0), (0, 0), (0, 0)))
    halves = rows.reshape(B_, n_half, CMP_STRIDE, 2, N_KV_C, HEAD_DIM)
    return jnp.einsum('bnsckd,cjsdh->bnjckh', halves, w1h)


def compress_blocks(hp, L, w1h, b1, w2, pe):
    r = CMP_LEN // CMP_STRIDE
    n_cmp = (L - CMP_LEN) // CMP_STRIDE + 1
    pre = (jnp.einsum('cjsd,cjsdh->ch', pe.reshape(2, r, CMP_STRIDE, HEAD_DIM), w1h) + b1)[:, None, :]
    for j in range(r):
        pre = pre + hp[:, j:j + n_cmp, j]
    return jnp.einsum('bnckh,chd->bnckd', jax.nn.gelu(pre), w2)


def cmp_to_sel(p, n_slc):
    ratio = SEL_BLOCK // CMP_STRIDE
    left = CMP_LEN // CMP_STRIDE - 1
    pad = [(0, 0)] * (p.ndim - 1) + [(left, ratio * n_slc - p.shape[-1])]
    pp = jnp.pad(p, pad)
    out = pp[..., 0:ratio * n_slc:ratio]
    for o in range(1, ratio + left):
        out = out + pp[..., o:o + ratio * n_slc:ratio]
    return out


def select_blocks(imp, pos, n_slc):
    k_top = N_SEL - N_FORCED
    cur = pos // SEL_BLOCK
    j = jnp.arange(n_slc)
    cand = (j[None, :] >= 1) & (j[None, :] <= cur[:, None] - 2)
    sc = jnp.where(cand, imp, -jnp.inf)
    if n_slc < k_top:
        sc = jnp.pad(sc, ((0, 0), (0, 0), (0, 0), (0, k_top - n_slc)), constant_values=-jnp.inf)
    vals, idx = lax.top_k(sc, k_top)
    forced = jnp.stack([cur, cur - 1, jnp.zeros_like(cur)], axis=-1)
    forced_ok = jnp.stack([cur >= 0, cur >= 1, cur >= 2], axis=-1)
    lead = imp.shape[:2]
    blocks = jnp.concatenate([jnp.broadcast_to(forced, lead + forced.shape), idx.astype(cur.dtype)], axis=-1)
    ok = jnp.concatenate([jnp.broadcast_to(forced_ok, lead + forced_ok.shape), vals > -jnp.inf], axis=-1)
    return jnp.clip(blocks, 0, n_slc - 1), ok


def selected_prompt(q, slc_kv, blocks, ok, pos):
    B_, S = q.shape[:2]
    n_slc = S // SEL_BLOCK
    kb_k = slc_kv[:, :, 0].reshape(B_, n_slc, SEL_BLOCK, N_KV_C, HEAD_DIM)
    kb_v = slc_kv[:, :, 1].reshape(B_, n_slc, SEL_BLOCK, N_KV_C, HEAD_DIM)
    bi = jnp.arange(B_)[:, None, None, None]
    hi = jnp.arange(N_KV_C)[None, :, None, None]
    nk = N_SEL * SEL_BLOCK

    def one_block(i):
        s0 = i * NSA_Q_BLOCK
        qb = lax.dynamic_slice_in_dim(q, s0, NSA_Q_BLOCK, axis=1)
        blk = lax.dynamic_slice_in_dim(blocks, s0, NSA_Q_BLOCK, axis=2)
        okb = lax.dynamic_slice_in_dim(ok, s0, NSA_Q_BLOCK, axis=2)
        qpos = lax.dynamic_slice_in_dim(pos, s0, NSA_Q_BLOCK)
        k = kb_k[bi, blk, :, hi].reshape(B_, N_KV_C, NSA_Q_BLOCK, nk, HEAD_DIM)
        v = kb_v[bi, blk, :, hi].reshape(B_, N_KV_C, NSA_Q_BLOCK, nk, HEAD_DIM)
        kpos = blk[..., None] * SEL_BLOCK + jnp.arange(SEL_BLOCK)
        mask = okb[..., None] & (kpos <= qpos[None, None, :, None, None])
        o, _ = attend_gathered(qb, k, v, mask.reshape(B_, N_KV_C, 1, NSA_Q_BLOCK, nk))
        return o

    o = lax.map(one_block, jnp.arange(S // NSA_Q_BLOCK))
    return jnp.moveaxis(o, 0, 1).reshape(q.shape)


def gather_selected_sample(pool, page_table, new_kv, blocks):
    B_, T = new_kv.shape[:2]
    sub = PAGE_SIZE // SEL_BLOCK
    nb_past = PAST_LEN // SEL_BLOCK
    n_new = -(-T // SEL_BLOCK)
    pool_b = pool.reshape(pool.shape[0], sub, SEL_BLOCK, 2, N_KV_C, HEAD_DIM)
    newp = jnp.pad(new_kv, ((0, 0), (0, n_new * SEL_BLOCK - T), (0, 0), (0, 0), (0, 0)))
    newp = newp.reshape(B_, n_new, SEL_BLOCK, 2, N_KV_C, HEAD_DIM)
    bi = jnp.arange(B_)[:, None, None, None]
    hi = jnp.arange(N_KV_C)[None, :, None, None]
    pb = jnp.minimum(blocks, nb_past - 1)
    past = pool_b[page_table[bi, pb // sub], pb % sub, :, :, hi]
    new = newp[bi, jnp.clip(blocks - nb_past, 0, n_new - 1), :, :, hi]
    return jnp.where((blocks >= nb_past)[..., None, None, None], new, past)


def mixer_c(h, pos, w_in, w_o, cmp_w1, cmp_b1, cmp_w2, cmp_pe, past=None):
    B_, T = h.shape[:2]
    G = N_Q_C // N_KV_C
    nq = N_Q_C * HEAD_DIM
    y = h @ w_in
    q_raw = y[..., :nq].reshape(B_, T, N_Q_C, HEAD_DIM)
    cmp_kv = y[..., nq:nq + C_KV_WIDTH].reshape(B_, T, 2, N_KV_C, HEAD_DIM)
    slc_kv = rope_kv(y[..., nq + C_KV_WIDTH:nq + 2 * C_KV_WIDTH].reshape(B_, T, 2, N_KV_C, HEAD_DIM), pos)
    win_kv = rope_kv(y[..., nq + 2 * C_KV_WIDTH:nq + 3 * C_KV_WIDTH].reshape(B_, T, 2, N_KV_C, HEAD_DIM), pos)
    gates = jax.nn.sigmoid(y[..., nq + 3 * C_KV_WIDTH:].astype(jnp.float32)).reshape(B_, T, 3, N_KV_C, G, 1)
    q_cmp = q_raw.reshape(B_, T, N_KV_C, G, HEAD_DIM)
    q = rope(q_raw, pos).reshape(B_, T, N_KV_C, G, HEAD_DIM)

    w1h = cmp_w1.reshape(2, CMP_LEN // CMP_STRIDE, CMP_STRIDE, HEAD_DIM, CMP_HID)
    if past is None:
        hp = half_block_proj(cmp_kv, w1h)
        L = T
    else:
        win_buf, cmp_pool, slc_pool, page_table = past
        past_rows = cmp_pool[page_table].reshape(B_, -1, 2, N_KV_C, HEAD_DIM)
        hp = jnp.concatenate([half_block_proj(past_rows, w1h), half_block_proj(cmp_kv, w1h)], axis=1)
        L = PAST_LEN + T
    comp = compress_blocks(hp, L, w1h, cmp_b1, cmp_w2, cmp_pe)
    blk_end = jnp.arange(comp.shape[1]) * CMP_STRIDE + CMP_LEN - 1
    s = jnp.einsum('bqhgd,bnhd->bhgqn', q_cmp, comp[:, :, 0], preferred_element_type=jnp.float32) * ATTN_SCALE
    p_cmp, _ = masked_softmax(s, blk_end[None, :] <= pos[:, None])
    o_cmp = jnp.einsum('bhgqn,bnhd->bqhgd', p_cmp.astype(comp.dtype), comp[:, :, 1])

    n_slc = -(-L // SEL_BLOCK)
    blocks, ok = select_blocks(cmp_to_sel(p_cmp.sum(axis=2), n_slc), pos, n_slc)
    if past is None:
        o_slc = selected_prompt(q, slc_kv, blocks, ok, pos)
        o_win, _ = banded_attention(q, win_kv[:, :, 0], win_kv[:, :, 1], WINDOW_C)
        new_win = win_kv[:, -min(WINDOW_C, T):]
    else:
        nk = N_SEL * SEL_BLOCK
        kv_sel = gather_selected_sample(slc_pool, page_table, slc_kv, blocks)
        kpos = blocks[..., None] * SEL_BLOCK + jnp.arange(SEL_BLOCK)
        mask = ok[..., None] & (kpos <= pos[None, None, :, None, None])
        o_slc, _ = attend_gathered(q, kv_sel[..., 0, :].reshape(B_, N_KV_C, T, nk, HEAD_DIM),
                                   kv_sel[..., 1, :].reshape(B_, N_KV_C, T, nk, HEAD_DIM),
                                   mask.reshape(B_, N_KV_C, 1, T, nk))
        o_win, _, new_win = window_attend_with_buffer(q, win_kv, win_buf, WINDOW_C)
    o = (gates[:, :, 0] * o_cmp.astype(jnp.float32) + gates[:, :, 1] * o_slc.astype(jnp.float32)
         + gates[:, :, 2] * o_win.astype(jnp.float32)).astype(h.dtype)
    return o.reshape(B_, T, -1) @ w_o, (new_win, cmp_kv, slc_kv)


def routed_experts(x, eidx, ew, w_gate, w_up, w_down):
    N = x.shape[0]
    NK = N * TOP_K
    flat_e = eidx.reshape(-1)
    order = jnp.argsort(flat_e)
    sorted_e = flat_e[order]
    counts = jnp.bincount(flat_e, length=N_EXPERTS)
    padded = (counts + MOE_BLOCK - 1) // MOE_BLOCK * MOE_BLOCK
    ends = jnp.cumsum(padded)
    dest = (ends - padded)[sorted_e] + jnp.arange(NK) - (jnp.cumsum(counts) - counts)[sorted_e]
    n_blk = (NK + N_EXPERTS * (MOE_BLOCK - 1)) // MOE_BLOCK + 1
    cap = n_blk * MOE_BLOCK
    row_tok = jnp.full((cap,), N, jnp.int32).at[dest].set((order // TOP_K).astype(jnp.int32))
    row_w = jnp.zeros((cap,), jnp.float32).at[dest].set(ew.reshape(-1)[order])
    blk_e = jnp.minimum(jnp.searchsorted(ends, jnp.arange(n_blk) * MOE_BLOCK, side='right'), N_EXPERTS - 1)
    x_pad = jnp.concatenate([x, jnp.zeros((1, x.shape[1]), x.dtype)], axis=0)

    def one_block(args):
        tok, e = args
        xb = x_pad[tok]
        return (jax.nn.silu(xb @ w_gate[e]) * (xb @ w_up[e])) @ w_down[e]

    out = lax.map(one_block, (row_tok.reshape(n_blk, MOE_BLOCK), blk_e))
    out = out.reshape(cap, -1) * row_w[:, None].astype(x.dtype)
    return jnp.zeros_like(x_pad).at[row_tok].add(out)[:N]


def moe_ffn(h, router_w, router_b, w_gate, w_up, w_down, sh_gate, sh_up, sh_down):
    shp = h.shape
    x = h.reshape(-1, shp[-1])
    N = x.shape[0]
    per = N_EXPERTS // N_EXPERT_GROUPS
    scores = jax.nn.sigmoid(jnp.einsum('nd,de->ne', x, router_w, preferred_element_type=jnp.float32))
    biased = scores + router_b.astype(jnp.float32)
    grp = lax.top_k(biased.reshape(N, N_EXPERT_GROUPS, per), 2)[0].sum(-1)
    _, gidx = lax.top_k(grp, TOPK_GROUPS)
    gmask = jnp.any(gidx[:, :, None] == jnp.arange(N_EXPERT_GROUPS), axis=1)
    _, eidx = lax.top_k(jnp.where(jnp.repeat(gmask, per, axis=1), biased, -jnp.inf), TOP_K)
    ew = jnp.take_along_axis(scores, eidx, axis=-1)
    ew = ew / jnp.sum(ew, axis=-1, keepdims=True) * ROUTED_SCALE
    shared = (jax.nn.silu(x @ sh_gate) * (x @ sh_up)) @ sh_down
    return (routed_experts(x, eidx, ew, w_gate, w_up, w_down) + shared).reshape(shp)


def setup_inputs(seed: int = 0) -> dict:
    key = jax.random.key(seed)
    keys = iter(jax.random.split(key, 48))

    def rnd(shape, scale):
        return jax.random.normal(next(keys), shape, jnp.float32) * scale

    n_pages = PAST_LEN // PAGE_SIZE
    n_used = DEC_BATCH * n_pages
    n_pool = n_used + n_used // 4
    d = D_MODEL
    hb = B_HEADS_PER_GROUP
    inp = {}
    inp['x_prompt'] = rnd((BATCH, SEQ, d), 1.0)
    inp['x_sample'] = rnd((DEC_BATCH, DEC_SEQ, d), 1.0)
    inp['c_prompt'] = rnd((BATCH, d), 1.0)
    inp['c_sample'] = rnd((DEC_BATCH, d), 1.0)
    inp['cache_a_kv'] = rnd((N_LAYERS_A, DEC_BATCH, min(WINDOW_A, PAST_LEN), 2, N_KV_A, HEAD_DIM), 1.0)
    inp['cache_b_kv_w128'] = rnd((N_LAYERS_B, DEC_BATCH, min(B_PATTERNS[0][0], PAST_LEN), 2, hb, HEAD_DIM), 1.0)
    inp['cache_b_kv_w512'] = rnd((N_LAYERS_B, DEC_BATCH, min(B_PATTERNS[1][0], PAST_LEN), 2, hb, HEAD_DIM), 1.0)
    inp['cache_b_kv_w2048'] = rnd((N_LAYERS_B, DEC_BATCH, min(B_PATTERNS[2][0], PAST_LEN), 2, hb, HEAD_DIM), 1.0)
    inp['cache_c_win_kv'] = rnd((N_LAYERS_C, DEC_BATCH, min(WINDOW_C, PAST_LEN), 2, N_KV_C, HEAD_DIM), 1.0)
    inp['cache_c_cmp_kv'] = rnd((N_LAYERS_C, n_pool, PAGE_SIZE, 2, N_KV_C, HEAD_DIM), 1.0)
    inp['cache_c_slc_kv'] = rnd((N_LAYERS_C, n_pool, PAGE_SIZE, 2, N_KV_C, HEAD_DIM), 1.0)
    inp['page_table'] = jax.random.permutation(next(keys), n_pool)[:n_used].reshape(DEC_BATCH, n_pages).astype(jnp.int32)
    inp['norm_g'] = 1.0 + rnd((DEPTH, 2, d), 0.02)
    inp['final_g'] = 1.0 + rnd((d,), 0.02)
    inp['ada_w'] = rnd((DEPTH, d, 6 * d), 0.3 * d ** -0.5)
    inp['ada_b'] = rnd((DEPTH, 6 * d), 0.1)
    inp['a_w_qkv'] = rnd((N_LAYERS_A, d, (N_Q_A + 2 * N_KV_A) * HEAD_DIM), d ** -0.5)
    inp['a_w_o'] = rnd((N_LAYERS_A, N_Q_A * HEAD_DIM, d), (N_Q_A * HEAD_DIM) ** -0.5)
    inp['a_sink'] = rnd((N_LAYERS_A, N_Q_A), 1.0)
    inp['b_w_qkv'] = rnd((N_LAYERS_B, d, 3 * B_WIDTH), d ** -0.5)
    inp['b_w_o'] = rnd((N_LAYERS_B, B_WIDTH, d), B_WIDTH ** -0.5)
    inp['c_w_in'] = rnd((N_LAYERS_C, d, C_IN_WIDTH), d ** -0.5)
    inp['c_w_o'] = rnd((N_LAYERS_C, N_Q_C * HEAD_DIM, d), (N_Q_C * HEAD_DIM) ** -0.5)
    inp['c_cmp_w1'] = rnd((N_LAYERS_C, 2, CMP_LEN, HEAD_DIM, CMP_HID), (CMP_LEN * HEAD_DIM) ** -0.5)
    inp['c_cmp_b1'] = rnd((N_LAYERS_C, 2, CMP_HID), 0.02)
    inp['c_cmp_w2'] = rnd((N_LAYERS_C, 2, CMP_HID, HEAD_DIM), 1.5 * CMP_HID ** -0.5)
    inp['c_cmp_pe'] = rnd((N_LAYERS_C, 2, CMP_LEN, HEAD_DIM), 0.5)
    inp['moe_router'] = rnd((DEPTH, d, N_EXPERTS), d ** -0.5)
    inp['moe_bias'] = rnd((DEPTH, N_EXPERTS), 0.01)
    inp['moe_w_gate'] = rnd((DEPTH, N_EXPERTS, d, D_EXPERT), d ** -0.5)
    inp['moe_w_up'] = rnd((DEPTH, N_EXPERTS, d, D_EXPERT), d ** -0.5)
    inp['moe_w_down'] = rnd((DEPTH, N_EXPERTS, D_EXPERT, d), D_EXPERT ** -0.5)
    inp['shared_w_gate'] = rnd((DEPTH, d, D_SHARED), d ** -0.5)
    inp['shared_w_up'] = rnd((DEPTH, d, D_SHARED), d ** -0.5)
    inp['shared_w_down'] = rnd((DEPTH, D_SHARED, d), D_SHARED ** -0.5)
    return inp


def reference(x_prompt, x_sample, c_prompt, c_sample, cache_a_kv, cache_b_kv_w128, cache_b_kv_w512,
              cache_b_kv_w2048, cache_c_win_kv, cache_c_cmp_kv, cache_c_slc_kv, page_table, norm_g, final_g,
              ada_w, ada_b, a_w_qkv, a_w_o, a_sink, b_w_qkv, b_w_o, c_w_in, c_w_o, c_cmp_w1, c_cmp_b1,
              c_cmp_w2, c_cmp_pe, moe_router, moe_bias, moe_w_gate, moe_w_up, moe_w_down, shared_w_gate,
              shared_w_up, shared_w_down):
    b_caches = (cache_b_kv_w128, cache_b_kv_w512, cache_b_kv_w2048)

    def layer(x, c, pos, l, past):
        kind, slot = LAYER_KIND[l], LAYER_SLOT[l]
        mod = sublayer_mods(c, ada_w[l], ada_b[l])
        h = modulate(rms_norm(x, norm_g[l, 0]), mod[:, 0], mod[:, 1])
        if kind == 0:
            o, st = mixer_a(h, pos, a_w_qkv[slot], a_w_o[slot], a_sink[slot], past)
        elif kind == 1:
            o, st = mixer_b(h, pos, b_w_qkv[slot], b_w_o[slot], past)
        else:
            o, st = mixer_c(h, pos, c_w_in[slot], c_w_o[slot], c_cmp_w1[slot], c_cmp_b1[slot],
                            c_cmp_w2[slot], c_cmp_pe[slot], past)
        x = x + mod[:, 2, None] * o
        h = modulate(rms_norm(x, norm_g[l, 1]), mod[:, 3], mod[:, 4])
        x = x + mod[:, 5, None] * moe_ffn(h, moe_router[l], moe_bias[l], moe_w_gate[l], moe_w_up[l],
                                           moe_w_down[l], shared_w_gate[l], shared_w_up[l], shared_w_down[l])
        return x, st

    def sample_past(l):
        kind, slot = LAYER_KIND[l], LAYER_SLOT[l]
        if kind == 0:
            return cache_a_kv[slot]
        if kind == 1:
            return tuple(buf[slot] for buf in b_caches)
        return (cache_c_win_kv[slot], cache_c_cmp_kv[slot], cache_c_slc_kv[slot], page_table)

    pos_p = jnp.arange(x_prompt.shape[1], dtype=jnp.int32)
    pos_s = PAST_LEN + jnp.arange(x_sample.shape[1], dtype=jnp.int32)
    xp, xs = x_prompt, x_sample
    st_p = {0: [], 1: [], 2: []}
    st_s = {0: [], 1: [], 2: []}
    for l in range(DEPTH):
        xp, sp = layer(xp, c_prompt, pos_p, l, None)
        xs, ss = layer(xs, c_sample, pos_s, l, sample_past(l))
        st_p[LAYER_KIND[l]].append(sp)
        st_s[LAYER_KIND[l]].append(ss)
    y_prompt = rms_norm(xp, final_g)
    y_sample = rms_norm(xs, final_g)

    new_a_kv_prompt = jnp.stack(st_p[0])
    new_a_kv_sample = jnp.stack(st_s[0])
    new_b_kv_w128_prompt = jnp.stack([s[0] for s in st_p[1]])
    new_b_kv_w128_sample = jnp.stack([s[0] for s in st_s[1]])
    new_b_kv_w512_prompt = jnp.stack([s[1] for s in st_p[1]])
    new_b_kv_w512_sample = jnp.stack([s[1] for s in st_s[1]])
    new_b_kv_w2048_prompt = jnp.stack([s[2] for s in st_p[1]])
    new_b_kv_w2048_sample = jnp.stack([s[2] for s in st_s[1]])
    new_c_win_kv_prompt = jnp.stack([s[0] for s in st_p[2]])
    new_c_win_kv_sample = jnp.stack([s[0] for s in st_s[2]])
    new_c_cmp_kv_prompt = jnp.stack([s[1] for s in st_p[2]])
    new_c_cmp_kv_sample = jnp.stack([s[1] for s in st_s[2]])
    new_c_slc_kv_prompt = jnp.stack([s[2] for s in st_p[2]])
    new_c_slc_kv_sample = jnp.stack([s[2] for s in st_s[2]])
    return (y_prompt, y_sample, new_a_kv_prompt, new_a_kv_sample, new_b_kv_w128_prompt, new_b_kv_w128_sample,
            new_b_kv_w512_prompt, new_b_kv_w512_sample, new_b_kv_w2048_prompt, new_b_kv_w2048_sample,
            new_c_win_kv_prompt, new_c_win_kv_sample, new_c_cmp_kv_prompt, new_c_cmp_kv_sample,
            new_c_slc_kv_prompt, new_c_slc_kv_sample)
```

```python
import functools

import jax
import jax.numpy as jnp
from jax import lax
from jax.experimental import pallas as pl
from jax.experimental.pallas import tpu as pltpu

D_MODEL = 1024
DEPTH = 4
PAST_LEN = 8192
PAGE_SIZE = 128
HEAD_DIM = 64
ROT_DIM = HEAD_DIM // 4
ROPE_THETA = 500000.0
ATTN_SCALE = HEAD_DIM ** -0.5
RMS_EPS = 1e-6

N_MIXERS = 3
LAYER_KIND = tuple(i % N_MIXERS for i in range(DEPTH))
LAYER_SLOT = tuple(LAYER_KIND[:i].count(LAYER_KIND[i]) for i in range(DEPTH))

N_Q_A = 16
N_KV_A = 4
WINDOW_A = 128
B_PATTERNS = ((128, 1), (512, 4), (2048, 16))
N_GROUPS_B = len(B_PATTERNS)
B_HEADS_PER_GROUP = 4
B_WIDTH = N_GROUPS_B * B_HEADS_PER_GROUP * HEAD_DIM
N_Q_C = 16
N_KV_C = 4
CMP_LEN = 32
CMP_STRIDE = 16
CMP_HID = 64
SEL_BLOCK = 64
N_SEL = 16
N_FORCED = 3
WINDOW_C = 512
C_KV_WIDTH = 2 * N_KV_C * HEAD_DIM
C_Q_WIDTH = N_Q_C * HEAD_DIM
C_MAIN_WIDTH = C_Q_WIDTH + 3 * C_KV_WIDTH
N_GATES_C = 3 * N_Q_C

N_EXPERTS = 64
TOP_K = 8
N_EXPERT_GROUPS = 8
TOPK_GROUPS = 4
D_EXPERT = 256
ROUTED_SCALE = 2.5

LANES_V7X = 128
VMEM_LIMIT_V7X = 56 * 1024 * 1024

ROW_TILE = 512
COL_GROUP = 512
MOE_TILE = 512
SEL_Q_TILE = 128
SEL_K_TILE = 256
NEG_BIG = -1e30

BF16 = jnp.bfloat16
F32 = jnp.float32


def _cparams(*sem):
    return pltpu.CompilerParams(dimension_semantics=sem, vmem_limit_bytes=VMEM_LIMIT_V7X)


def _rope_tables(pos, rows):
    half = ROT_DIM // 2
    inv_freq = ROPE_THETA ** (-jnp.arange(half, dtype=F32) / half)
    ang = pos.astype(F32)[:, None] * inv_freq
    cos, sin = jnp.cos(ang), jnp.sin(ang)
    t = pos.shape[0]
    z8 = jnp.zeros((t, half), F32)
    rest1 = jnp.ones((t, HEAD_DIM - ROT_DIM), F32)
    rest0 = jnp.zeros((t, HEAD_DIM - ROT_DIM), F32)
    reps = LANES_V7X // HEAD_DIM
    cos_t = jnp.tile(jnp.concatenate([cos, cos, rest1], axis=1), (1, reps))
    sin_a = jnp.tile(jnp.concatenate([-sin, z8, rest0], axis=1), (1, reps))
    sin_b = jnp.tile(jnp.concatenate([z8, sin, rest0], axis=1), (1, reps))
    return tuple(jnp.broadcast_to(a, (rows, LANES_V7X)) for a in (cos_t, sin_a, sin_b))


def _rope_chunk(blk, cos_t, sin_a, sin_b):
    return (blk * cos_t + pltpu.roll(blk, LANES_V7X - ROT_DIM // 2, 1) * sin_a
            + pltpu.roll(blk, ROT_DIM // 2, 1) * sin_b)


def _build_plan(n_cols, specs):
    plan = []
    for c0 in range(0, n_cols, COL_GROUP):
        width = min(COL_GROUP, n_cols - c0)
        segs = []
        for (s0, s1, mode, oi, d0) in specs:
            lo, hi = max(s0, c0), min(s1, c0 + width)
            if lo >= hi:
                continue
            step = LANES_V7X if mode == 'rope' else hi - lo
            for a in range(lo, hi, step):
                segs.append((a - c0, min(step, hi - a), mode, oi, d0 + a - s0))
        plan.append((c0, width, tuple(segs)))
    return tuple(plan)


def _modulated_norm(x, g, scale, shift):
    var = jnp.mean(x * x, axis=-1, keepdims=True)
    y = x * lax.rsqrt(var + RMS_EPS) * g
    return y * (1 + scale) + shift


def _norm_linear_kernel(x_ref, g_ref, sh_ref, sc_ref, w_ref, cos_ref, sa_ref, sb_ref, *refs, plan, n_out):
    outs, h_ref = refs[:n_out], refs[n_out]
    h_ref[...] = _modulated_norm(x_ref[0], g_ref[...], sc_ref[0], sh_ref[0]).astype(BF16)
    for (c0, width, segs) in plan:
        acc = jnp.dot(h_ref[...], w_ref[:, c0:c0 + width], preferred_element_type=F32)
        for (off, wd, mode, oi, dst) in segs:
            blk = acc[:, off:off + wd]
            if mode == 'rope':
                blk = _rope_chunk(blk, cos_ref[...], sa_ref[...], sb_ref[...])
            elif mode == 'sigmoid':
                blk = jax.nn.sigmoid(blk)
            outs[oi][0, :, dst:dst + wd] = blk


def norm_linear(x, g, shift, scale, w_bf, tabs, specs, out_widths, tm):
    b, t, k = x.shape
    n = w_bf.shape[1]
    per_row = shift.shape[1] != 1
    mod_spec = (pl.BlockSpec((1, tm, k), lambda bi, i: (bi, i, 0)) if per_row
                else pl.BlockSpec((1, 1, k), lambda bi, i: (bi, 0, 0)))
    tab_spec = pl.BlockSpec((tm, LANES_V7X), lambda bi, i: (i, 0))
    plan = _build_plan(n, specs)
    outs = pl.pallas_call(
        functools.partial(_norm_linear_kernel, plan=plan, n_out=len(out_widths)),
        grid=(b, t // tm),
        in_specs=[pl.BlockSpec((1, tm, k), lambda bi, i: (bi, i, 0)),
                  pl.BlockSpec((1, k), lambda bi, i: (0, 0)),
                  mod_spec, mod_spec,
                  pl.BlockSpec((k, n), lambda bi, i: (0, 0)),
                  tab_spec, tab_spec, tab_spec],
        out_specs=[pl.BlockSpec((1, tm, wd), lambda bi, i: (bi, i, 0)) for wd in out_widths],
        out_shape=[jax.ShapeDtypeStruct((b, t, wd), F32) for wd in out_widths],
        scratch_shapes=[pltpu.VMEM((tm, k), BF16)],
        compiler_params=_cparams("parallel", "arbitrary"),
        name="norm_linear",
    )(x, g.reshape(1, k), shift, scale, w_bf, *tabs)
    return outs


def _linear_out_kernel(*refs, mode, n_o):
    o_refs = refs[:n_o]
    w_ref, x_ref, gate_ref, out_ref = refs[n_o:]
    if mode == 'plain':
        o = o_refs[0][0]
    elif mode == 'sum3':
        o = (o_refs[0][0] + o_refs[1][0]) + o_refs[2][0]
    else:
        ng = n_o // 2
        lses = [r[0] for r in o_refs[ng:]]
        m = functools.reduce(jnp.maximum, lses)
        es = [jnp.exp(l - m) for l in lses]
        den = functools.reduce(lambda a, c: a + c, es)
        o = jnp.concatenate([(e / den) * r[0] for e, r in zip(es, o_refs[:ng])], axis=1)
    acc = jnp.dot(o.astype(BF16), w_ref[...], preferred_element_type=F32)
    out_ref[0] = x_ref[0] + gate_ref[0] * acc


def linear_out(o_list, w_bf, x, gate, mode, tm):
    b, t, d = x.shape
    per_row = gate.shape[1] != 1
    gate_spec = (pl.BlockSpec((1, tm, d), lambda bi, i: (bi, i, 0)) if per_row
                 else pl.BlockSpec((1, 1, d), lambda bi, i: (bi, 0, 0)))
    row_spec = lambda wd: pl.BlockSpec((1, tm, wd), lambda bi, i: (bi, i, 0))
    return pl.pallas_call(
        functools.partial(_linear_out_kernel, mode=mode, n_o=len(o_list)),
        grid=(b, t // tm),
        in_specs=[row_spec(o.shape[-1]) for o in o_list]
        + [pl.BlockSpec(w_bf.shape, lambda bi, i: (0, 0)), row_spec(d), gate_spec],
        out_specs=row_spec(d),
        out_shape=jax.ShapeDtypeStruct((b, t, d), F32),
        compiler_params=_cparams("parallel", "parallel"),
        name="linear_out",
    )(*o_list, w_bf, x, gate)


def _band_kernel(*refs, n_kv, group, tq, has_sink, has_gate, gate_col, want_lse):
    q_ref, kp_ref, kc_ref, vp_ref, vc_ref = refs[:5]
    pos = 5
    sink_ref = gate_ref = lse_ref = None
    if has_sink:
        sink_ref = refs[pos]
        pos += 1
    if has_gate:
        gate_ref = refs[pos]
        pos += 1
    o_ref = refs[pos]
    if want_lse:
        lse_ref = refs[pos + 1]
    no_prev = jnp.where(pl.program_id(2) == 0, 2 * tq, 0)
    rows = lax.broadcasted_iota(jnp.int32, (tq, 2 * tq), 0)
    cols = lax.broadcasted_iota(jnp.int32, (tq, 2 * tq), 1)
    mask = ((cols < tq) & (cols >= rows + no_prev)) | ((cols >= tq) & ((cols - tq) <= rows))
    for j in range(n_kv):
        ks = slice(j * HEAD_DIM, (j + 1) * HEAD_DIM)
        k = jnp.concatenate([kp_ref[0, :, ks], kc_ref[0, :, ks]], axis=0).astype(BF16)
        v = jnp.concatenate([vp_ref[0, :, ks], vc_ref[0, :, ks]], axis=0).astype(BF16)
        for g in range(group):
            h = j * group + g
            hs = slice(h * HEAD_DIM, (h + 1) * HEAD_DIM)
            q = (q_ref[0, :, hs] * ATTN_SCALE).astype(BF16)
            s = lax.dot_general(q, k, (((1,), (1,)), ((), ())), preferred_element_type=F32)
            s = jnp.where(mask, s, -jnp.inf)
            m = jnp.max(s, axis=-1, keepdims=True)
            if has_sink:
                m = jnp.maximum(m, sink_ref[h])
            e = jnp.exp(s - m)
            den = jnp.sum(e, axis=-1, keepdims=True)
            if has_sink:
                den = den + jnp.exp(sink_ref[h] - m)
            o = jnp.dot(e.astype(BF16), v, preferred_element_type=F32) / den
            if has_gate:
                o = gate_ref[0, :, gate_col + h:gate_col + h + 1] * o
            o_ref[0, :, hs] = o
            if want_lse:
                lse_ref[0, :, hs] = jnp.broadcast_to(m + jnp.log(den), (tq, HEAD_DIM))


def band_attention(qa, ka, va, *, dil, n_kv, group, tq, q_idx, k_idx, v_idx, sink=None, gate=None,
                   gate_col=0, want_lse=False):
    b, s = qa.shape[:2]
    l = s // dil
    qw, kw = n_kv * group * HEAD_DIM, n_kv * HEAD_DIM
    q_rs, k_rs, v_rs = qa.shape[2] // qw, ka.shape[2] // kw, va.shape[2] // kw
    q2, k2, v2 = (a.reshape(b, l, dil * a.shape[2]) for a in (qa, ka, va))
    prev = lambda i: jnp.maximum(i - 1, 0)
    in_specs = [pl.BlockSpec((1, tq, qw), lambda bi, r, i: (bi, i, r * q_rs + q_idx)),
                pl.BlockSpec((1, tq, kw), lambda bi, r, i: (bi, prev(i), r * k_rs + k_idx)),
                pl.BlockSpec((1, tq, kw), lambda bi, r, i: (bi, i, r * k_rs + k_idx)),
                pl.BlockSpec((1, tq, kw), lambda bi, r, i: (bi, prev(i), r * v_rs + v_idx)),
                pl.BlockSpec((1, tq, kw), lambda bi, r, i: (bi, i, r * v_rs + v_idx))]
    args = [q2, k2, k2, v2, v2]
    if sink is not None:
        in_specs.append(pl.BlockSpec(memory_space=pltpu.SMEM))
        args.append(sink.astype(F32))
    if gate is not None:
        in_specs.append(pl.BlockSpec((1, tq, gate.shape[2]), lambda bi, r, i: (bi, i, 0)))
        args.append(gate)
    o_spec = pl.BlockSpec((1, tq, qw), lambda bi, r, i: (bi, i, r))
    o_shape = jax.ShapeDtypeStruct((b, l, dil * qw), F32)
    res = pl.pallas_call(
        functools.partial(_band_kernel, n_kv=n_kv, group=group, tq=tq, has_sink=sink is not None,
                          has_gate=gate is not None, gate_col=gate_col, want_lse=want_lse),
        grid=(b, dil, l // tq),
        in_specs=in_specs,
        out_specs=[o_spec, o_spec] if want_lse else o_spec,
        out_shape=[o_shape, o_shape] if want_lse else o_shape,
        compiler_params=_cparams("parallel", "parallel", "arbitrary"),
        name="band_attention",
    )(*args)
    if want_lse:
        return res[0].reshape(b, s, qw), res[1].reshape(b, s, qw)
    return res.reshape(b, s, qw)


def _sel_kernel(q_ref, k_ref, v_ref, sel_ref, gate_ref, o_ref, *, tq, tk, n_kv, group, n_blocks, gate_col):
    i = pl.program_id(1)
    n_chunks = (i * tq + tq + tk - 1) // tk
    qpos = i * tq + lax.broadcasted_iota(jnp.int32, (tq, tk), 0)
    for j in range(n_kv):
        ks = slice(j * HEAD_DIM, (j + 1) * HEAD_DIM)
        q = jnp.concatenate(
            [(q_ref[0, :, (j * group + g) * HEAD_DIM:(j * group + g + 1) * HEAD_DIM] * ATTN_SCALE).astype(BF16)
             for g in range(group)], axis=0)
        sel = sel_ref[0, j]

        def body(c, carry):
            m, l, acc = carry
            k0 = pl.multiple_of(c * tk, tk)
            k = k_ref[0, pl.ds(k0, tk), ks].astype(BF16)
            v = v_ref[0, pl.ds(k0, tk), ks].astype(BF16)
            kpos = k0 + lax.broadcasted_iota(jnp.int32, (tq, tk), 1)
            blk_of_key = lax.shift_right_logical(k0 + lax.broadcasted_iota(jnp.int32, (n_blocks, tk), 1),
                                                 SEL_BLOCK.bit_length() - 1)
            expand = (blk_of_key == lax.broadcasted_iota(jnp.int32, (n_blocks, tk), 0)).astype(BF16)
            chosen = jnp.dot(sel, expand, preferred_element_type=F32) > 0.5
            mask = chosen & (kpos <= qpos)
            mask = jnp.concatenate([mask] * group, axis=0)
            s = lax.dot_general(q, k, (((1,), (1,)), ((), ())), preferred_element_type=F32)
            s = jnp.where(mask, s, NEG_BIG)
            m_new = jnp.maximum(m, jnp.max(s, axis=-1, keepdims=True))
            p = jnp.where(mask, jnp.exp(s - m_new), 0.0)
            alpha = jnp.exp(m - m_new)
            l = alpha * l + jnp.sum(p, axis=-1, keepdims=True)
            acc = alpha * acc + jnp.dot(p.astype(BF16), v, preferred_element_type=F32)
            return m_new, l, acc

        init = (jnp.full((group * tq, 1), NEG_BIG, F32), jnp.zeros((group * tq, 1), F32),
                jnp.zeros((group * tq, HEAD_DIM), F32))
        _, l, acc = lax.fori_loop(0, n_chunks, body, init)
        o = acc / l
        for g in range(group):
            h = j * group + g
            o_ref[0, :, h * HEAD_DIM:(h + 1) * HEAD_DIM] = (
                gate_ref[0, :, gate_col + h:gate_col + h + 1] * o[g * tq:(g + 1) * tq])


def selected_attention_prompt(q_rope, y_main, selmask, gates, *, k_idx, v_idx, gate_col):
    b, t, qw = q_rope.shape
    n_blocks = selmask.shape[-1]
    kw = N_KV_C * HEAD_DIM
    tq, tk = SEL_Q_TILE, SEL_K_TILE
    return pl.pallas_call(
        functools.partial(_sel_kernel, tq=tq, tk=tk, n_kv=N_KV_C, group=N_Q_C // N_KV_C,
                          n_blocks=n_blocks, gate_col=gate_col),
        grid=(b, t // tq),
        in_specs=[pl.BlockSpec((1, tq, qw), lambda bi, i: (bi, i, 0)),
                  pl.BlockSpec((1, t, kw), lambda bi, i: (bi, 0, k_idx)),
                  pl.BlockSpec((1, t, kw), lambda bi, i: (bi, 0, v_idx)),
                  pl.BlockSpec((1, N_KV_C, tq, n_blocks), lambda bi, i: (bi, 0, i, 0)),
                  pl.BlockSpec((1, tq, gates.shape[2]), lambda bi, i: (bi, i, 0))],
        out_specs=pl.BlockSpec((1, tq, qw), lambda bi, i: (bi, i, 0)),
        out_shape=jax.ShapeDtypeStruct((b, t, qw), F32),
        compiler_params=_cparams("parallel", "arbitrary"),
        name="selected_attention",
    )(q_rope, y_main, y_main, selmask, gates)


def _sel_sample_kernel(idx_ref, ok_ref, q_ref, knew_ref, vnew_ref, gate_ref, b0, b1, b2, b3, o_ref,
                       m_s, l_s, acc_s, *, n_sel):
    bi, s = pl.program_id(0), pl.program_id(1)
    n_q, grp = N_Q_C, N_Q_C // N_KV_C
    q = q_ref[0] * ATTN_SCALE

    @pl.when(s == 0)
    def _():
        m_s[...] = jnp.sum(q * knew_ref[0], axis=-1, keepdims=True)
        l_s[...] = jnp.ones_like(l_s)
        acc_s[...] = vnew_ref[0]

    @pl.when(s > 0)
    def _():
        blks = (b0, b1, b2, b3)
        half = N_KV_C * HEAD_DIM
        k = jnp.concatenate([blks[j][0, :, j * HEAD_DIM:(j + 1) * HEAD_DIM] for j in range(N_KV_C)], axis=0)
        v = jnp.concatenate([blks[j][0, :, half + j * HEAD_DIM:half + (j + 1) * HEAD_DIM]
                             for j in range(N_KV_C)], axis=0)
        sc = lax.dot_general(q.astype(BF16), k.astype(BF16), (((1,), (1,)), ((), ())),
                             preferred_element_type=F32)
        shape = (n_q, N_KV_C * SEL_BLOCK)
        key_head = lax.shift_right_logical(lax.broadcasted_iota(jnp.int32, shape, 1), SEL_BLOCK.bit_length() - 1)
        q_head = lax.shift_right_logical(lax.broadcasted_iota(jnp.int32, shape, 0), grp.bit_length() - 1)
        ok = jnp.zeros(shape, jnp.int32)
        for j in range(N_KV_C):
            ok = jnp.where(key_head == j, ok_ref[(bi * N_KV_C + j) * n_sel + s], ok)
        mask = (key_head == q_head) & (ok > 0)
        sc = jnp.where(mask, sc, NEG_BIG)
        m_old = m_s[...]
        m_new = jnp.maximum(m_old, jnp.max(sc, axis=-1, keepdims=True))
        p = jnp.where(mask, jnp.exp(sc - m_new), 0.0)
        alpha = jnp.exp(m_old - m_new)
        m_s[...] = m_new
        l_s[...] = alpha * l_s[...] + jnp.sum(p, axis=-1, keepdims=True)
        acc_s[...] = alpha * acc_s[...] + jnp.dot(p.astype(BF16), v.astype(BF16), preferred_element_type=F32)

    @pl.when(s == n_sel - 1)
    def _():
        o_ref[0] = gate_ref[0] * (acc_s[...] / l_s[...])


def selected_attention_sample(q, k_new, v_new, gate, pool, half_idx, ok):
    n, n_q, d = q.shape
    n_sel = half_idx.shape[0] // (n * N_KV_C)
    width = 2 * N_KV_C * HEAD_DIM
    pool2 = pool.reshape(pool.shape[0] * (PAGE_SIZE // SEL_BLOCK), SEL_BLOCK, width)
    row_spec = lambda w: pl.BlockSpec((1, n_q, w), lambda bi, s, idx, okr: (bi, 0, 0))

    def blk_spec(j):
        return pl.BlockSpec((1, SEL_BLOCK, width),
                            lambda bi, s, idx, okr: (idx[(bi * N_KV_C + j) * n_sel + s], 0, 0))

    grid_spec = pltpu.PrefetchScalarGridSpec(
        num_scalar_prefetch=2,
        grid=(n, n_sel),
        in_specs=[row_spec(d), row_spec(d), row_spec(d), row_spec(1)] + [blk_spec(j) for j in range(N_KV_C)],
        out_specs=row_spec(d),
        scratch_shapes=[pltpu.VMEM((n_q, 1), F32), pltpu.VMEM((n_q, 1), F32), pltpu.VMEM((n_q, d), F32)],
    )
    return pl.pallas_call(
        functools.partial(_sel_sample_kernel, n_sel=n_sel),
        grid_spec=grid_spec,
        out_shape=jax.ShapeDtypeStruct((n, n_q, d), F32),
        compiler_params=_cparams("parallel", "arbitrary"),
        name="selected_attention_sample",
    )(half_idx, ok, q, k_new, v_new, gate, pool2, pool2, pool2, pool2)


def _half_proj_kernel(*refs, n_rows):
    n_pairs = N_KV_C // 2
    x_refs, w_ref, o_ref = refs[:2 * n_pairs], refs[2 * n_pairs], refs[2 * n_pairs + 1]
    out_w = 2 * 2 * CMP_HID
    for c in range(2):
        for p in range(n_pairs):
            x_ref = x_refs[c * n_pairs + p]
            acc = jnp.zeros((n_rows, out_w), F32)
            for sp in range(CMP_STRIDE // 2):
                lo = x_ref[0, pl.ds(2 * sp, n_rows, stride=CMP_STRIDE), :]
                hi = x_ref[0, pl.ds(2 * sp + 1, n_rows, stride=CMP_STRIDE), :]
                lhs = jnp.concatenate([lo, hi], axis=1).astype(BF16)
                acc = acc + jnp.dot(lhs, w_ref[c, sp], preferred_element_type=F32)
            o_ref[0, :, (c * n_pairs + p) * out_w:(c * n_pairs + p + 1) * out_w] = acc


def half_block_proj(rows_arr, col_idx, w_bd, tile_rows):
    b, l = rows_arr.shape[:2]
    n_half_tile = tile_rows // CMP_STRIDE
    out_w = 2 * 2 * N_KV_C * CMP_HID
    n_chunks = C_KV_WIDTH // LANES_V7X

    def chunk_spec(cp):
        return pl.BlockSpec((1, tile_rows, LANES_V7X), lambda bi, i: (bi, i, col_idx * n_chunks + cp))

    return pl.pallas_call(
        functools.partial(_half_proj_kernel, n_rows=n_half_tile),
        grid=(b, l // tile_rows),
        in_specs=[chunk_spec(cp) for cp in range(n_chunks)] + [pl.BlockSpec(w_bd.shape, lambda bi, i: (0, 0, 0, 0))],
        out_specs=pl.BlockSpec((1, n_half_tile, out_w), lambda bi, i: (bi, i, 0)),
        out_shape=jax.ShapeDtypeStruct((b, l // CMP_STRIDE, out_w), F32),
        compiler_params=_cparams("parallel", "parallel"),
        name="half_block_proj",
    )(*([rows_arr] * n_chunks), w_bd)


def _half_proj_weight(w1h):
    r = CMP_LEN // CMP_STRIDE
    eye = jnp.eye(2, dtype=F32)
    w = w1h.reshape(2, r, CMP_STRIDE // 2, 2, HEAD_DIM, CMP_HID)
    w = jnp.einsum('cjpldh,kq->cplkdqjh', w, eye)
    return w.reshape(2, CMP_STRIDE // 2, 4 * HEAD_DIM, 2 * r * CMP_HID).astype(BF16)


def _moe_in_kernel(x_ref, g_ref, sh_ref, sc_ref, rw_ref, sg_ref, su_ref, sd_ref, h_ref, score_ref, shared_ref):
    h = _modulated_norm(x_ref[0], g_ref[...], sc_ref[0], sh_ref[0])
    logits = jnp.dot(h, rw_ref[...], preferred_element_type=F32, precision=lax.Precision.HIGHEST)
    score_ref[0] = jax.nn.sigmoid(logits)
    hb = h.astype(BF16)
    h_ref[0] = hb
    gate = jnp.dot(hb, sg_ref[...], preferred_element_type=F32)
    up = jnp.dot(hb, su_ref[...], preferred_element_type=F32)
    mid = (jax.nn.silu(gate) * up).astype(BF16)
    shared_ref[0] = jnp.dot(mid, sd_ref[...], preferred_element_type=F32)


def moe_in(x, g, shift, scale, router_w, sg_bf, su_bf, sd_bf, tm):
    b, t, k = x.shape
    per_row = shift.shape[1] != 1
    mod_spec = (pl.BlockSpec((1, tm, k), lambda bi, i: (bi, i, 0)) if per_row
                else pl.BlockSpec((1, 1, k), lambda bi, i: (bi, 0, 0)))
    row_spec = lambda wd: pl.BlockSpec((1, tm, wd), lambda bi, i: (bi, i, 0))
    full = lambda a: pl.BlockSpec(a.shape, lambda bi, i: (0,) * a.ndim)
    return pl.pallas_call(
        _moe_in_kernel,
        grid=(b, t // tm),
        in_specs=[row_spec(k), pl.BlockSpec((1, k), lambda bi, i: (0, 0)), mod_spec, mod_spec,
                  full(router_w), full(sg_bf), full(su_bf), full(sd_bf)],
        out_specs=[row_spec(k), row_spec(N_EXPERTS), row_spec(k)],
        out_shape=[jax.ShapeDtypeStruct((b, t, k), BF16), jax.ShapeDtypeStruct((b, t, N_EXPERTS), F32),
                   jax.ShapeDtypeStruct((b, t, k), F32)],
        compiler_params=_cparams("parallel", "parallel"),
        name="moe_in",
    )(x, g.reshape(1, k), shift, scale, router_w, sg_bf, su_bf, sd_bf)


def _gmm_kernel(blk_e_ref, n_used_ref, x_ref, wg_ref, wu_ref, wd_ref, o_ref, wg_s, wu_s, wd_s):
    blk = pl.program_id(0)
    prev_e = blk_e_ref[jnp.maximum(blk - 1, 0)]
    new_expert = (blk == 0) | (blk_e_ref[blk] != prev_e)

    @pl.when(new_expert)
    def _():
        wg_s[...] = wg_ref[0].astype(BF16)
        wu_s[...] = wu_ref[0].astype(BF16)
        wd_s[...] = wd_ref[0].astype(BF16)

    @pl.when(blk < n_used_ref[0])
    def _():
        x = x_ref[...]
        gate = jnp.dot(x, wg_s[...], preferred_element_type=F32)
        up = jnp.dot(x, wu_s[...], preferred_element_type=F32)
        mid = (jax.nn.silu(gate) * up).astype(BF16)
        o_ref[...] = jnp.dot(mid, wd_s[...], preferred_element_type=F32)

    @pl.when(blk >= n_used_ref[0])
    def _():
        o_ref[...] = jnp.zeros_like(o_ref)


def grouped_experts(xs, blk_e, n_used, w_gate, w_up, w_down):
    cap, k = xs.shape
    tm = MOE_TILE
    n_blk = cap // tm
    de = w_gate.shape[-1]
    grid_spec = pltpu.PrefetchScalarGridSpec(
        num_scalar_prefetch=2,
        grid=(n_blk,),
        in_specs=[pl.BlockSpec((tm, k), lambda i, be, nu: (i, 0)),
                  pl.BlockSpec((1, k, de), lambda i, be, nu: (be[i], 0, 0)),
                  pl.BlockSpec((1, k, de), lambda i, be, nu: (be[i], 0, 0)),
                  pl.BlockSpec((1, de, k), lambda i, be, nu: (be[i], 0, 0))],
        out_specs=pl.BlockSpec((tm, k), lambda i, be, nu: (i, 0)),
        scratch_shapes=[pltpu.VMEM((k, de), BF16), pltpu.VMEM((k, de), BF16), pltpu.VMEM((de, k), BF16)],
    )
    return pl.pallas_call(
        _gmm_kernel,
        grid_spec=grid_spec,
        out_shape=jax.ShapeDtypeStruct((cap, k), F32),
        compiler_params=_cparams("arbitrary"),
        name="grouped_experts",
    )(blk_e, n_used, xs, w_gate, w_up, w_down)


def _moe_out_kernel(x_ref, gate_ref, routed_ref, shared_ref, o_ref):
    o_ref[0] = x_ref[0] + gate_ref[0] * (routed_ref[0] + shared_ref[0])


def moe_out(x, gate, routed, shared, tm):
    b, t, d = x.shape
    per_row = gate.shape[1] != 1
    gate_spec = (pl.BlockSpec((1, tm, d), lambda bi, i: (bi, i, 0)) if per_row
                 else pl.BlockSpec((1, 1, d), lambda bi, i: (bi, 0, 0)))
    row_spec = pl.BlockSpec((1, tm, d), lambda bi, i: (bi, i, 0))
    return pl.pallas_call(
        _moe_out_kernel,
        grid=(b, t // tm),
        in_specs=[row_spec, gate_spec, row_spec, row_spec],
        out_specs=row_spec,
        out_shape=jax.ShapeDtypeStruct((b, t, d), F32),
        compiler_params=_cparams("parallel", "parallel"),
        name="moe_out",
    )(x, gate, routed, shared)


def _final_norm_kernel(x_ref, g_ref, o_ref):
    x = x_ref[0]
    var = jnp.mean(x * x, axis=-1, keepdims=True)
    o_ref[0] = x * lax.rsqrt(var + RMS_EPS) * g_ref[...]


def final_norm(x, g, tm):
    b, t, d = x.shape
    row_spec = pl.BlockSpec((1, tm, d), lambda bi, i: (bi, i, 0))
    return pl.pallas_call(
        _final_norm_kernel,
        grid=(b, t // tm),
        in_specs=[row_spec, pl.BlockSpec((1, d), lambda bi, i: (0, 0))],
        out_specs=row_spec,
        out_shape=jax.ShapeDtypeStruct((b, t, d), F32),
        compiler_params=_cparams("parallel", "parallel"),
        name="final_norm",
    )(x, g.reshape(1, d))


def _ada_kernel(c_ref, w_ref, b_ref, o_ref):
    a = jax.nn.silu(c_ref[...]).astype(BF16)
    o_ref[0] = jnp.dot(a, w_ref[0].astype(BF16), preferred_element_type=F32) + b_ref[0]


def ada_mods(c_all, ada_w, ada_b):
    n, d = c_all.shape
    depth, _, width = ada_w.shape
    tn = 1024
    return pl.pallas_call(
        _ada_kernel,
        grid=(depth, width // tn),
        in_specs=[pl.BlockSpec((n, d), lambda l, j: (0, 0)),
                  pl.BlockSpec((1, d, tn), lambda l, j: (l, 0, j)),
                  pl.BlockSpec((1, 1, tn), lambda l, j: (l, 0, j))],
        out_specs=pl.BlockSpec((1, n, tn), lambda l, j: (l, 0, j)),
        out_shape=jax.ShapeDtypeStruct((depth, n, width), F32),
        compiler_params=_cparams("parallel", "parallel"),
        name="ada_mods",
    )(c_all, ada_w, ada_b.reshape(depth, 1, width))


def _masked_softmax(s, mask, sink=None):
    s = jnp.where(mask, s, -jnp.inf)
    m = jnp.max(s, axis=-1, keepdims=True)
    if sink is not None:
        m = jnp.maximum(m, sink)
    m = jnp.where(jnp.isfinite(m), m, 0.0)
    e = jnp.exp(s - m)
    den = jnp.sum(e, axis=-1, keepdims=True)
    if sink is not None:
        den = den + jnp.exp(sink - m)
    p = e / jnp.maximum(den, 1e-30)
    return p, (m + jnp.log(den))[..., 0]


def _attend(q, k, v, mask, sink=None):
    s = jnp.einsum('bqhgd,bkhd->bhgqk', q, k, preferred_element_type=F32) * ATTN_SCALE
    p, lse = _masked_softmax(s, mask, sink)
    return jnp.einsum('bhgqk,bkhd->bqhgd', p.astype(v.dtype), v), lse


def _attend_gathered(q, k, v, mask):
    s = jnp.einsum('bqhgd,bhqnd->bhgqn', q, k, preferred_element_type=F32) * ATTN_SCALE
    p, lse = _masked_softmax(s, mask)
    return jnp.einsum('bhgqn,bhqnd->bqhgd', p.astype(v.dtype), v), lse


def _window_attend_with_buffer(q, kv, buf, window, sink=None):
    lb, t = buf.shape[1], kv.shape[1]
    allkv = jnp.concatenate([buf, kv], axis=1)
    kpos = PAST_LEN - lb + jnp.arange(lb + t)
    qpos = PAST_LEN + jnp.arange(t)
    diff = qpos[:, None] - kpos[None, :]
    mask = (diff >= 0) & (diff <= window)
    o, lse = _attend(q, allkv[:, :, 0], allkv[:, :, 1], mask, sink)
    return o, lse, allkv[:, -min(window, PAST_LEN + t):]


def _dilated_sample(q, kv_new, buf, win, dil):
    lb, t = buf.shape[1], kv_new.shape[1]
    allkv = jnp.concatenate([buf, kv_new], axis=1)
    n_taps = win // dil + 1
    idx = lb + jnp.arange(t)[:, None] - dil * jnp.arange(n_taps)[None, :]
    valid = idx >= 0
    g = allkv[:, jnp.maximum(idx, 0)]
    k = g[:, :, :, 0].transpose(0, 3, 1, 2, 4)
    v = g[:, :, :, 1].transpose(0, 3, 1, 2, 4)
    o, lse = _attend_gathered(q[:, :, :, None], k, v, valid)
    return o[:, :, :, 0], lse[:, :, 0].transpose(0, 2, 1), allkv[:, -min(win, PAST_LEN + t):]


def _cmp_to_sel(p, n_slc):
    ratio = SEL_BLOCK // CMP_STRIDE
    left = CMP_LEN // CMP_STRIDE - 1
    pad = [(0, 0)] * (p.ndim - 1) + [(left, ratio * n_slc - p.shape[-1])]
    pp = jnp.pad(p, pad)
    out = pp[..., 0:ratio * n_slc:ratio]
    for o in range(1, ratio + left):
        out = out + pp[..., o:o + ratio * n_slc:ratio]
    return out


def _select_blocks(imp, pos, n_slc):
    k_top = N_SEL - N_FORCED
    cur = pos // SEL_BLOCK
    j = jnp.arange(n_slc)
    cand = (j[None, :] >= 1) & (j[None, :] <= cur[:, None] - 2)
    sc = jnp.where(cand, imp, -jnp.inf)
    if n_slc < k_top:
        sc = jnp.pad(sc, ((0, 0), (0, 0), (0, 0), (0, k_top - n_slc)), constant_values=-jnp.inf)
    vals, idx = lax.top_k(sc, k_top)
    forced = jnp.stack([cur, cur - 1, jnp.zeros_like(cur)], axis=-1)
    forced_ok = jnp.stack([cur >= 0, cur >= 1, cur >= 2], axis=-1)
    lead = imp.shape[:2]
    blocks = jnp.concatenate([jnp.broadcast_to(forced, lead + forced.shape), idx.astype(cur.dtype)], axis=-1)
    ok = jnp.concatenate([jnp.broadcast_to(forced_ok, lead + forced_ok.shape), vals > -jnp.inf], axis=-1)
    return jnp.clip(blocks, 0, n_slc - 1), ok


def _compress(hp, n_cmp, w1h, b1, w2, pe):
    b = hp.shape[0]
    r = CMP_LEN // CMP_STRIDE
    hp = hp.reshape(b, hp.shape[1], 2, N_KV_C, r, CMP_HID)
    pre = (jnp.einsum('cjsd,cjsdh->ch', pe.reshape(2, r, CMP_STRIDE, HEAD_DIM), w1h) + b1)[:, None, :]
    for j in range(r):
        pre = pre + hp[:, j:j + n_cmp, :, :, j]
    return jnp.einsum('bnckh,chd->bnckd', jax.nn.gelu(pre), w2)


def _compressed_attention(q_cmp, comp, pos):
    blk_end = jnp.arange(comp.shape[1]) * CMP_STRIDE + CMP_LEN - 1
    s = jnp.einsum('bqhgd,bnhd->bhgqn', q_cmp, comp[:, :, 0], preferred_element_type=F32) * ATTN_SCALE
    p_cmp, _ = _masked_softmax(s, blk_end[None, :] <= pos[:, None])
    o_cmp = jnp.einsum('bhgqn,bnhd->bqhgd', p_cmp.astype(comp.dtype), comp[:, :, 1])
    return o_cmp, p_cmp


A_SPECS = ((0, (N_Q_A + N_KV_A) * HEAD_DIM, 'rope', 0, 0),
           ((N_Q_A + N_KV_A) * HEAD_DIM, (N_Q_A + 2 * N_KV_A) * HEAD_DIM, 'raw', 0, (N_Q_A + N_KV_A) * HEAD_DIM))
B_SPECS = ((0, 2 * B_WIDTH, 'rope', 0, 0), (2 * B_WIDTH, 3 * B_WIDTH, 'raw', 0, 2 * B_WIDTH))
_C0 = C_Q_WIDTH + C_KV_WIDTH
_C1 = _C0 + C_KV_WIDTH
C_SPECS = ((0, _C0, 'raw', 0, 0),
           (_C0, _C0 + C_KV_WIDTH // 2, 'rope', 0, _C0), (_C0 + C_KV_WIDTH // 2, _C1, 'raw', 0, _C0 + C_KV_WIDTH // 2),
           (_C1, _C1 + C_KV_WIDTH // 2, 'rope', 0, _C1), (_C1 + C_KV_WIDTH // 2, C_MAIN_WIDTH, 'raw', 0, _C1 + C_KV_WIDTH // 2),
           (0, C_Q_WIDTH, 'rope', 1, 0),
           (C_MAIN_WIDTH, C_MAIN_WIDTH + N_GATES_C, 'sigmoid', 2, 0))


def _mixer_a(x, mods, g, w_qkv_bf, w_o_bf, sink, tabs, tm, past):
    shift, scale, gate = mods
    nq = N_Q_A * HEAD_DIM
    kvw = N_KV_A * HEAD_DIM
    (y,) = norm_linear(x, g, shift, scale, w_qkv_bf, tabs, A_SPECS, (nq + 2 * kvw,), tm)
    if past is None:
        o = band_attention(y, y, y, dil=1, n_kv=N_KV_A, group=N_Q_A // N_KV_A, tq=WINDOW_A,
                           q_idx=0, k_idx=nq // kvw, v_idx=nq // kvw + 1, sink=sink)
        b, t = y.shape[:2]
        new = y[:, -min(WINDOW_A, t):, nq:].reshape(b, -1, 2, N_KV_A, HEAD_DIM)
    else:
        n = y.shape[1]
        q = y[0, :, :nq].reshape(n, 1, N_KV_A, N_Q_A // N_KV_A, HEAD_DIM)
        kv = y[0, :, nq:].reshape(n, 1, 2, N_KV_A, HEAD_DIM)
        sink_b = sink.astype(F32).reshape(1, N_KV_A, N_Q_A // N_KV_A, 1, 1)
        o, _, new = _window_attend_with_buffer(q, kv, past, WINDOW_A, sink_b)
        o = o.reshape(1, n, nq)
    return linear_out([o], w_o_bf, x, gate, 'plain', tm), new


def _mixer_b(x, mods, g, w_qkv_bf, w_o_bf, tabs, tm, past):
    shift, scale, gate = mods
    hpg = B_HEADS_PER_GROUP
    gw = hpg * HEAD_DIM
    (y,) = norm_linear(x, g, shift, scale, w_qkv_bf, tabs, B_SPECS, (3 * B_WIDTH,), tm)
    outs, lses, news = [], [], []
    if past is None:
        b, t = y.shape[:2]
        for gi, (win, dil) in enumerate(B_PATTERNS):
            o, lse = band_attention(y, y, y, dil=dil, n_kv=hpg, group=1, tq=win // dil,
                                    q_idx=gi, k_idx=N_GROUPS_B + gi, v_idx=2 * N_GROUPS_B + gi, want_lse=True)
            outs.append(o)
            lses.append(lse)
            k = y[:, -min(win, t):, B_WIDTH + gi * gw:B_WIDTH + (gi + 1) * gw]
            v = y[:, -min(win, t):, 2 * B_WIDTH + gi * gw:2 * B_WIDTH + (gi + 1) * gw]
            news.append(jnp.stack([k, v], axis=2).reshape(b, -1, 2, hpg, HEAD_DIM))
    else:
        n = y.shape[1]
        y3 = y[0].reshape(n, 1, 3, N_GROUPS_B * hpg, HEAD_DIM)
        for gi, (win, dil) in enumerate(B_PATTERNS):
            hs = slice(gi * hpg, (gi + 1) * hpg)
            kv_g = jnp.stack([y3[:, :, 1, hs], y3[:, :, 2, hs]], axis=2)
            o, lse, new = _dilated_sample(y3[:, :, 0, hs], kv_g, past[gi], win, dil)
            outs.append(o.reshape(1, n, gw))
            lses.append(jnp.broadcast_to(lse.reshape(1, n, hpg, 1), (1, n, hpg, HEAD_DIM)).reshape(1, n, gw))
            news.append(new)
    return linear_out(outs + lses, w_o_bf, x, gate, 'mix3', tm), tuple(news)


def _mixer_c(x, mods, g, w_in_bf, w_o_bf, cmp_w1, cmp_b1, cmp_w2, cmp_pe, tabs, pos, tm, past):
    shift, scale, gate = mods
    grp = N_Q_C // N_KV_C
    kvw = N_KV_C * HEAD_DIM
    y, q_rope, gates = norm_linear(x, g, shift, scale, w_in_bf, tabs, C_SPECS,
                                   (C_MAIN_WIDTH, C_Q_WIDTH, N_GATES_C), tm)
    w1h = cmp_w1.reshape(2, CMP_LEN // CMP_STRIDE, CMP_STRIDE, HEAD_DIM, CMP_HID)
    w_bd = _half_proj_weight(w1h)
    cmp_idx, slc_idx, win_idx = C_Q_WIDTH // C_KV_WIDTH, _C0 // C_KV_WIDTH, _C1 // C_KV_WIDTH
    if past is None:
        b, t = y.shape[:2]
        hp = half_block_proj(y, cmp_idx, w_bd, t)
        n_cmp = (t - CMP_LEN) // CMP_STRIDE + 1
        comp = _compress(hp, n_cmp, w1h, cmp_b1, cmp_w2, cmp_pe)
        q_cmp = y[..., :C_Q_WIDTH].reshape(b, t, N_KV_C, grp, HEAD_DIM)
        o_cmp, p_cmp = _compressed_attention(q_cmp, comp, pos)
        gview = gates.reshape(b, t, 3, N_KV_C, grp, 1)
        o_cmp = (gview[:, :, 0] * o_cmp.astype(F32)).reshape(b, t, C_Q_WIDTH)
        n_slc = -(-t // SEL_BLOCK)
        blocks, ok = _select_blocks(_cmp_to_sel(p_cmp.sum(axis=2), n_slc), pos, n_slc)
        onehot = (blocks[..., None] == jnp.arange(n_slc)) & ok[..., None]
        selmask = jnp.any(onehot, axis=-2).astype(BF16)
        o_slc = selected_attention_prompt(q_rope, y, selmask, gates, k_idx=2 * slc_idx, v_idx=2 * slc_idx + 1,
                                          gate_col=N_Q_C)
        o_win = band_attention(q_rope, y, y, dil=1, n_kv=N_KV_C, group=grp, tq=WINDOW_C, q_idx=0,
                               k_idx=2 * win_idx, v_idx=2 * win_idx + 1, gate=gates, gate_col=2 * N_Q_C)
        new_win = y[:, -min(WINDOW_C, t):, _C1:].reshape(b, -1, 2, N_KV_C, HEAD_DIM)
        cmp_kv = y[..., C_Q_WIDTH:_C0].reshape(b, t, 2, N_KV_C, HEAD_DIM)
        slc_kv = y[..., _C0:_C1].reshape(b, t, 2, N_KV_C, HEAD_DIM)
    else:
        win_buf, cmp_pool, slc_pool, page_table = past
        n = y.shape[1]
        t = 1
        cmp_kv = y[0, :, C_Q_WIDTH:_C0].reshape(n, t, 2, N_KV_C, HEAD_DIM)
        slc_kv = y[0, :, _C0:_C1].reshape(n, t, 2, N_KV_C, HEAD_DIM)
        win_kv = y[0, :, _C1:].reshape(n, t, 2, N_KV_C, HEAD_DIM)
        q_cmp = y[0, :, :C_Q_WIDTH].reshape(n, t, N_KV_C, grp, HEAD_DIM)
        q = q_rope[0].reshape(n, t, N_KV_C, grp, HEAD_DIM)
        gview = gates[0].reshape(n, t, 3, N_KV_C, grp, 1)
        past_rows = cmp_pool[page_table].reshape(n, PAST_LEN, C_KV_WIDTH)
        hp = half_block_proj(past_rows, 0, w_bd, PAST_LEN)
        length = PAST_LEN + t
        n_cmp = (length - CMP_LEN) // CMP_STRIDE + 1
        comp = _compress(hp, n_cmp, w1h, cmp_b1, cmp_w2, cmp_pe)
        o_cmp, p_cmp = _compressed_attention(q_cmp, comp, pos)
        n_slc = -(-length // SEL_BLOCK)
        blocks, ok = _select_blocks(_cmp_to_sel(p_cmp.sum(axis=2), n_slc), pos, n_slc)
        sub = PAGE_SIZE // SEL_BLOCK
        pb = jnp.minimum(blocks[:, :, 0], PAST_LEN // SEL_BLOCK - 1)
        page = jnp.take_along_axis(jnp.broadcast_to(page_table[:, None], (n, N_KV_C, page_table.shape[1])),
                                   pb // sub, axis=2)
        half_idx = page * sub + pb % sub
        half_idx = half_idx.at[:, :, 0].set(half_idx[:, :, 1])
        o_slc = selected_attention_sample(
            q_rope[0].reshape(n, N_Q_C, HEAD_DIM), jnp.repeat(slc_kv[:, 0, 0], grp, axis=1),
            jnp.repeat(slc_kv[:, 0, 1], grp, axis=1), gates[0, :, N_Q_C:2 * N_Q_C].reshape(n, N_Q_C, 1),
            slc_pool, half_idx.reshape(-1).astype(jnp.int32), ok[:, :, 0].reshape(-1).astype(jnp.int32))
        o_slc = o_slc.reshape(1, n, C_Q_WIDTH)
        o_win, _, new_win = _window_attend_with_buffer(q, win_kv, win_buf, WINDOW_C)
        o_cmp = (gview[:, :, 0] * o_cmp.astype(F32)).reshape(1, n, C_Q_WIDTH)
        o_win = (gview[:, :, 2] * o_win.astype(F32)).reshape(1, n, C_Q_WIDTH)
    return linear_out([o_cmp, o_slc, o_win], w_o_bf, x, gate, 'sum3', tm), (new_win, cmp_kv, slc_kv)


def _route(scores, router_b):
    n = scores.shape[0]
    per = N_EXPERTS // N_EXPERT_GROUPS
    biased = scores + router_b.astype(F32)
    grp = lax.top_k(biased.reshape(n, N_EXPERT_GROUPS, per), 2)[0].sum(-1)
    _, gidx = lax.top_k(grp, TOPK_GROUPS)
    gmask = jnp.any(gidx[:, :, None] == jnp.arange(N_EXPERT_GROUPS), axis=1)
    _, eidx = lax.top_k(jnp.where(jnp.repeat(gmask, per, axis=1), biased, -jnp.inf), TOP_K)
    ew = jnp.take_along_axis(scores, eidx, axis=-1)
    ew = ew / jnp.sum(ew, axis=-1, keepdims=True) * ROUTED_SCALE
    return eidx, ew


def _routed(h_bf, eidx, ew, w_gate, w_up, w_down):
    n, d = h_bf.shape
    nk = n * TOP_K
    tm = MOE_TILE
    n_blk = (nk + N_EXPERTS * (tm - 1)) // tm + 1
    cap = n_blk * tm
    flat_e = eidx.reshape(-1)
    order = jnp.argsort(flat_e)
    sorted_e = flat_e[order]
    counts = jnp.bincount(flat_e, length=N_EXPERTS)
    padded = (counts + tm - 1) // tm * tm
    ends = jnp.cumsum(padded)
    dest = (ends - padded)[sorted_e] + jnp.arange(nk) - (jnp.cumsum(counts) - counts)[sorted_e]
    row_tok = jnp.full((cap,), n, jnp.int32).at[dest].set((order // TOP_K).astype(jnp.int32))
    pair_row = jnp.zeros((nk,), jnp.int32).at[order].set(dest.astype(jnp.int32))
    blk_e = jnp.minimum(jnp.searchsorted(ends, jnp.arange(n_blk) * tm, side='right'), N_EXPERTS - 1).astype(jnp.int32)
    n_used = (ends[-1] // tm).astype(jnp.int32).reshape(1)
    h_pad = jnp.concatenate([h_bf, jnp.zeros((1, d), h_bf.dtype)], axis=0)
    ys = grouped_experts(h_pad[row_tok], blk_e, n_used, w_gate, w_up, w_down)
    picked = ys[pair_row].reshape(n, TOP_K, d) * ew[..., None]
    return jnp.sum(picked, axis=1)


def _moe(xp, xs, mods_p, mods_s, g, router_w, router_b, w_gate, w_up, w_down, sg_bf, su_bf, sd_bf):
    hp, sp, shp = moe_in(xp, g, mods_p[0], mods_p[1], router_w, sg_bf, su_bf, sd_bf, ROW_TILE)
    hs, ss, shs = moe_in(xs, g, mods_s[0], mods_s[1], router_w, sg_bf, su_bf, sd_bf, xs.shape[1])
    d = xp.shape[-1]
    n_p = xp.shape[0] * xp.shape[1]
    h_all = jnp.concatenate([hp.reshape(-1, d), hs.reshape(-1, d)], axis=0)
    scores = jnp.concatenate([sp.reshape(-1, N_EXPERTS), ss.reshape(-1, N_EXPERTS)], axis=0)
    eidx, ew = _route(scores, router_b)
    routed = _routed(h_all, eidx, ew, w_gate, w_up, w_down)
    xp = moe_out(xp, mods_p[2], routed[:n_p].reshape(xp.shape), shp, ROW_TILE)
    xs = moe_out(xs, mods_s[2], routed[n_p:].reshape(xs.shape), shs, xs.shape[1])
    return xp, xs


def kernel(x_prompt, x_sample, c_prompt, c_sample, cache_a_kv, cache_b_kv_w128, cache_b_kv_w512, cache_b_kv_w2048, cache_c_win_kv, cache_c_cmp_kv, cache_c_slc_kv, page_table, norm_g, final_g, ada_w, ada_b, a_w_qkv, a_w_o, a_sink, b_w_qkv, b_w_o, c_w_in, c_w_o, c_cmp_w1, c_cmp_b1, c_cmp_w2, c_cmp_pe, moe_router, moe_bias, moe_w_gate, moe_w_up, moe_w_down, shared_w_gate, shared_w_up, shared_w_down):
    bp, seq, d = x_prompt.shape
    ns = x_sample.shape[0]
    b_caches = (cache_b_kv_w128, cache_b_kv_w512, cache_b_kv_w2048)
    pos_p = jnp.arange(seq, dtype=jnp.int32)
    pos_s = PAST_LEN + jnp.arange(x_sample.shape[1], dtype=jnp.int32)
    tabs_p = _rope_tables(pos_p, seq)
    tabs_s = _rope_tables(pos_s, ns)

    mods = ada_mods(jnp.concatenate([c_prompt, c_sample], axis=0), ada_w, ada_b)
    mods_p = mods[:, :bp].reshape(DEPTH, bp, 6, 1, d)
    mods_s = mods[:, bp:].reshape(DEPTH, 1, ns, 6, d)

    xp = x_prompt
    xs = x_sample.reshape(1, ns, d)
    st_p = {0: [], 1: [], 2: []}
    st_s = {0: [], 1: [], 2: []}
    for l in range(DEPTH):
        kind, slot = LAYER_KIND[l], LAYER_SLOT[l]
        mp = [mods_p[l, :, i] for i in range(6)]
        ms = [mods_s[l, :, :, i] for i in range(6)]
        g_mix, g_moe = norm_g[l, 0], norm_g[l, 1]
        if kind == 0:
            w_in, w_o = a_w_qkv[slot].astype(BF16), a_w_o[slot].astype(BF16)
            xp, sp = _mixer_a(xp, mp[:3], g_mix, w_in, w_o, a_sink[slot], tabs_p, ROW_TILE, None)
            xs, ss = _mixer_a(xs, ms[:3], g_mix, w_in, w_o, a_sink[slot], tabs_s, ns, cache_a_kv[slot])
        elif kind == 1:
            w_in, w_o = b_w_qkv[slot].astype(BF16), b_w_o[slot].astype(BF16)
            xp, sp = _mixer_b(xp, mp[:3], g_mix, w_in, w_o, tabs_p, ROW_TILE, None)
            xs, ss = _mixer_b(xs, ms[:3], g_mix, w_in, w_o, tabs_s, ns, tuple(buf[slot] for buf in b_caches))
        else:
            w_in, w_o = c_w_in[slot].astype(BF16), c_w_o[slot].astype(BF16)
            cargs = (c_cmp_w1[slot], c_cmp_b1[slot], c_cmp_w2[slot], c_cmp_pe[slot])
            xp, sp = _mixer_c(xp, mp[:3], g_mix, w_in, w_o, *cargs, tabs_p, pos_p, ROW_TILE, None)
            xs, ss = _mixer_c(xs, ms[:3], g_mix, w_in, w_o, *cargs, tabs_s, pos_s, ns,
                              (cache_c_win_kv[slot], cache_c_cmp_kv[slot], cache_c_slc_kv[slot], page_table))
        st_p[kind].append(sp)
        st_s[kind].append(ss)
        xp, xs = _moe(xp, xs, mp[3:], ms[3:], g_moe, moe_router[l], moe_bias[l], moe_w_gate[l], moe_w_up[l],
                      moe_w_down[l], shared_w_gate[l].astype(BF16), shared_w_up[l].astype(BF16),
                      shared_w_down[l].astype(BF16))
    y_prompt = final_norm(xp, final_g, ROW_TILE)
    y_sample = final_norm(xs, final_g, ns).reshape(x_sample.shape)

    outs = [y_prompt, y_sample, jnp.stack(st_p[0]), jnp.stack(st_s[0])]
    for i in range(N_GROUPS_B):
        outs += [jnp.stack([s[i] for s in st_p[1]]), jnp.stack([s[i] for s in st_s[1]])]
    outs += [jnp.stack([s[0] for s in st_p[2]]), jnp.stack([s[0] for s in st_s[2]])]
    outs += [jnp.stack([s[1] for s in st_p[2]]), jnp.stack([s[1] for s in st_s[2]])]
    outs += [jnp.stack([s[2] for s in st_p[2]]), jnp.stack([s[2] for s in st_s[2]])]
    return tuple(outs)
```

```python
import functools

import jax
import jax.numpy as jnp
from jax import lax
from jax.experimental import pallas as pl
from jax.experimental.pallas import tpu as pltpu

D_MODEL = 1024
DEPTH = 4
PAST_LEN = 8192
PAGE_SIZE = 128
HEAD_DIM = 64
ROT_DIM = HEAD_DIM // 4
ROPE_THETA = 500000.0
ATTN_SCALE = HEAD_DIM ** -0.5
RMS_EPS = 1e-6

N_MIXERS = 3
LAYER_KIND = tuple(i % N_MIXERS for i in range(DEPTH))
LAYER_SLOT = tuple(LAYER_KIND[:i].count(LAYER_KIND[i]) for i in range(DEPTH))

N_Q_A = 16
N_KV_A = 4
WINDOW_A = 128
B_PATTERNS = ((128, 1), (512, 4), (2048, 16))
N_GROUPS_B = len(B_PATTERNS)
B_HEADS_PER_GROUP = 4
B_WIDTH = N_GROUPS_B * B_HEADS_PER_GROUP * HEAD_DIM
N_Q_C = 16
N_KV_C = 4
CMP_LEN = 32
CMP_STRIDE = 16
CMP_HID = 64
SEL_BLOCK = 64
N_SEL = 16
N_FORCED = 3
WINDOW_C = 512
C_KV_WIDTH = 2 * N_KV_C * HEAD_DIM
C_Q_WIDTH = N_Q_C * HEAD_DIM
C_MAIN_WIDTH = C_Q_WIDTH + 3 * C_KV_WIDTH
N_GATES_C = 3 * N_Q_C

N_EXPERTS = 64
TOP_K = 8
N_EXPERT_GROUPS = 8
TOPK_GROUPS = 4
D_EXPERT = 256
ROUTED_SCALE = 2.5

LANES_V7X = 128
VMEM_LIMIT_V7X = 56 * 1024 * 1024

ROW_TILE = 512
COL_GROUP = 512
MOE_TILE = 512
MOE_OUT_TILE = 256
CMP_PAGES_PER_STEP = 32
SEL_Q_TILE = 128
SEL_K_TILE = 256
NEG_BIG = -1e30

BF16 = jnp.bfloat16
F32 = jnp.float32


def _cparams(*sem):
    return pltpu.CompilerParams(dimension_semantics=sem, vmem_limit_bytes=VMEM_LIMIT_V7X)


def _rope_tables(pos, rows):
    half = ROT_DIM // 2
    inv_freq = ROPE_THETA ** (-jnp.arange(half, dtype=F32) / half)
    ang = pos.astype(F32)[:, None] * inv_freq
    cos, sin = jnp.cos(ang), jnp.sin(ang)
    t = pos.shape[0]
    z8 = jnp.zeros((t, half), F32)
    rest1 = jnp.ones((t, HEAD_DIM - ROT_DIM), F32)
    rest0 = jnp.zeros((t, HEAD_DIM - ROT_DIM), F32)
    reps = LANES_V7X // HEAD_DIM
    cos_t = jnp.tile(jnp.concatenate([cos, cos, rest1], axis=1), (1, reps))
    sin_a = jnp.tile(jnp.concatenate([-sin, z8, rest0], axis=1), (1, reps))
    sin_b = jnp.tile(jnp.concatenate([z8, sin, rest0], axis=1), (1, reps))
    return tuple(jnp.broadcast_to(a, (rows, LANES_V7X)) for a in (cos_t, sin_a, sin_b))


def _rope_chunk(blk, cos_t, sin_a, sin_b):
    return (blk * cos_t + pltpu.roll(blk, LANES_V7X - ROT_DIM // 2, 1) * sin_a
            + pltpu.roll(blk, ROT_DIM // 2, 1) * sin_b)


def _build_plan(n_cols, specs):
    plan = []
    for c0 in range(0, n_cols, COL_GROUP):
        width = min(COL_GROUP, n_cols - c0)
        segs = []
        for (s0, s1, mode, oi, d0) in specs:
            lo, hi = max(s0, c0), min(s1, c0 + width)
            if lo >= hi:
                continue
            step = LANES_V7X if mode == 'rope' else hi - lo
            for a in range(lo, hi, step):
                segs.append((a - c0, min(step, hi - a), mode, oi, d0 + a - s0))
        plan.append((c0, width, tuple(segs)))
    return tuple(plan)


def _modulated_norm(x, g, scale, shift):
    var = jnp.mean(x * x, axis=-1, keepdims=True)
    y = x * lax.rsqrt(var + RMS_EPS) * g
    return y * (1 + scale) + shift


def _norm_linear_kernel(x_ref, g_ref, sh_ref, sc_ref, w_ref, cos_ref, sa_ref, sb_ref, *refs, plan, n_out):
    outs, h_ref = refs[:n_out], refs[n_out]
    h_ref[...] = _modulated_norm(x_ref[0], g_ref[...], sc_ref[0], sh_ref[0]).astype(BF16)
    for (c0, width, segs) in plan:
        acc = jnp.dot(h_ref[...], w_ref[:, c0:c0 + width], preferred_element_type=F32)
        for (off, wd, mode, oi, dst) in segs:
            blk = acc[:, off:off + wd]
            if mode == 'rope':
                blk = _rope_chunk(blk, cos_ref[...], sa_ref[...], sb_ref[...])
            elif mode == 'sigmoid':
                blk = jax.nn.sigmoid(blk)
            outs[oi][0, :, dst:dst + wd] = blk


def norm_linear(x, g, shift, scale, w_bf, tabs, specs, out_widths, tm):
    b, t, k = x.shape
    n = w_bf.shape[1]
    per_row = shift.shape[1] != 1
    mod_spec = (pl.BlockSpec((1, tm, k), lambda bi, i: (bi, i, 0)) if per_row
                else pl.BlockSpec((1, 1, k), lambda bi, i: (bi, 0, 0)))
    tab_spec = pl.BlockSpec((tm, LANES_V7X), lambda bi, i: (i, 0))
    plan = _build_plan(n, specs)
    outs = pl.pallas_call(
        functools.partial(_norm_linear_kernel, plan=plan, n_out=len(out_widths)),
        grid=(b, t // tm),
        in_specs=[pl.BlockSpec((1, tm, k), lambda bi, i: (bi, i, 0)),
                  pl.BlockSpec((1, k), lambda bi, i: (0, 0)),
                  mod_spec, mod_spec,
                  pl.BlockSpec((k, n), lambda bi, i: (0, 0)),
                  tab_spec, tab_spec, tab_spec],
        out_specs=[pl.BlockSpec((1, tm, wd), lambda bi, i: (bi, i, 0)) for wd in out_widths],
        out_shape=[jax.ShapeDtypeStruct((b, t, wd), F32) for wd in out_widths],
        scratch_shapes=[pltpu.VMEM((tm, k), BF16)],
        compiler_params=_cparams("parallel", "arbitrary"),
        name="norm_linear",
    )(x, g.reshape(1, k), shift, scale, w_bf, *tabs)
    return outs


def _linear_out_kernel(*refs, mode, n_o):
    o_refs = refs[:n_o]
    w_ref, x_ref, gate_ref, out_ref = refs[n_o:]
    if mode == 'plain':
        o = o_refs[0][0]
    elif mode == 'sum3':
        o = (o_refs[0][0] + o_refs[1][0]) + o_refs[2][0]
    else:
        ng = n_o // 2
        lses = [r[0] for r in o_refs[ng:]]
        m = functools.reduce(jnp.maximum, lses)
        es = [jnp.exp(l - m) for l in lses]
        den = functools.reduce(lambda a, c: a + c, es)
        o = jnp.concatenate([(e / den) * r[0] for e, r in zip(es, o_refs[:ng])], axis=1)
    acc = jnp.dot(o.astype(BF16), w_ref[...], preferred_element_type=F32)
    out_ref[0] = x_ref[0] + gate_ref[0] * acc


def linear_out(o_list, w_bf, x, gate, mode, tm):
    b, t, d = x.shape
    per_row = gate.shape[1] != 1
    gate_spec = (pl.BlockSpec((1, tm, d), lambda bi, i: (bi, i, 0)) if per_row
                 else pl.BlockSpec((1, 1, d), lambda bi, i: (bi, 0, 0)))
    row_spec = lambda wd: pl.BlockSpec((1, tm, wd), lambda bi, i: (bi, i, 0))
    return pl.pallas_call(
        functools.partial(_linear_out_kernel, mode=mode, n_o=len(o_list)),
        grid=(b, t // tm),
        in_specs=[row_spec(o.shape[-1]) for o in o_list]
        + [pl.BlockSpec(w_bf.shape, lambda bi, i: (0, 0)), row_spec(d), gate_spec],
        out_specs=row_spec(d),
        out_shape=jax.ShapeDtypeStruct((b, t, d), F32),
        compiler_params=_cparams("parallel", "parallel"),
        name="linear_out",
    )(*o_list, w_bf, x, gate)


def _band_kernel(*refs, n_kv, group, tq, has_sink, has_gate, gate_col, want_lse):
    q_ref, kp_ref, kc_ref, vp_ref, vc_ref = refs[:5]
    pos = 5
    sink_ref = gate_ref = lse_ref = None
    if has_sink:
        sink_ref = refs[pos]
        pos += 1
    if has_gate:
        gate_ref = refs[pos]
        pos += 1
    o_ref = refs[pos]
    if want_lse:
        lse_ref = refs[pos + 1]
    no_prev = jnp.where(pl.program_id(2) == 0, 2 * tq, 0)
    rows = lax.broadcasted_iota(jnp.int32, (tq, 2 * tq), 0)
    cols = lax.broadcasted_iota(jnp.int32, (tq, 2 * tq), 1)
    mask = ((cols < tq) & (cols >= rows + no_prev)) | ((cols >= tq) & ((cols - tq) <= rows))
    for j in range(n_kv):
        ks = slice(j * HEAD_DIM, (j + 1) * HEAD_DIM)
        k = jnp.concatenate([kp_ref[0, :, ks], kc_ref[0, :, ks]], axis=0).astype(BF16)
        v = jnp.concatenate([vp_ref[0, :, ks], vc_ref[0, :, ks]], axis=0).astype(BF16)
        for g in range(group):
            h = j * group + g
            hs = slice(h * HEAD_DIM, (h + 1) * HEAD_DIM)
            q = (q_ref[0, :, hs] * ATTN_SCALE).astype(BF16)
            s = lax.dot_general(q, k, (((1,), (1,)), ((), ())), preferred_element_type=F32)
            s = jnp.where(mask, s, -jnp.inf)
            m = jnp.max(s, axis=-1, keepdims=True)
            if has_sink:
                m = jnp.maximum(m, sink_ref[h])
            e = jnp.exp(s - m)
            den = jnp.sum(e, axis=-1, keepdims=True)
            if has_sink:
                den = den + jnp.exp(sink_ref[h] - m)
            o = jnp.dot(e.astype(BF16), v, preferred_element_type=F32) / den
            if has_gate:
                o = gate_ref[0, :, gate_col + h:gate_col + h + 1] * o
            o_ref[0, :, hs] = o
            if want_lse:
                lse_ref[0, :, hs] = jnp.broadcast_to(m + jnp.log(den), (tq, HEAD_DIM))


def band_attention(qa, ka, va, *, dil, n_kv, group, tq, q_idx, k_idx, v_idx, sink=None, gate=None,
                   gate_col=0, want_lse=False):
    b, s = qa.shape[:2]
    l = s // dil
    qw, kw = n_kv * group * HEAD_DIM, n_kv * HEAD_DIM
    q_rs, k_rs, v_rs = qa.shape[2] // qw, ka.shape[2] // kw, va.shape[2] // kw
    q2, k2, v2 = (a.reshape(b, l, dil * a.shape[2]) for a in (qa, ka, va))
    prev = lambda i: jnp.maximum(i - 1, 0)
    in_specs = [pl.BlockSpec((1, tq, qw), lambda bi, r, i: (bi, i, r * q_rs + q_idx)),
                pl.BlockSpec((1, tq, kw), lambda bi, r, i: (bi, prev(i), r * k_rs + k_idx)),
                pl.BlockSpec((1, tq, kw), lambda bi, r, i: (bi, i, r * k_rs + k_idx)),
                pl.BlockSpec((1, tq, kw), lambda bi, r, i: (bi, prev(i), r * v_rs + v_idx)),
                pl.BlockSpec((1, tq, kw), lambda bi, r, i: (bi, i, r * v_rs + v_idx))]
    args = [q2, k2, k2, v2, v2]
    if sink is not None:
        in_specs.append(pl.BlockSpec(memory_space=pltpu.SMEM))
        args.append(sink.astype(F32))
    if gate is not None:
        in_specs.append(pl.BlockSpec((1, tq, gate.shape[2]), lambda bi, r, i: (bi, i, 0)))
        args.append(gate)
    o_spec = pl.BlockSpec((1, tq, qw), lambda bi, r, i: (bi, i, r))
    o_shape = jax.ShapeDtypeStruct((b, l, dil * qw), F32)
    res = pl.pallas_call(
        functools.partial(_band_kernel, n_kv=n_kv, group=group, tq=tq, has_sink=sink is not None,
                          has_gate=gate is not None, gate_col=gate_col, want_lse=want_lse),
        grid=(b, dil, l // tq),
        in_specs=in_specs,
        out_specs=[o_spec, o_spec] if want_lse else o_spec,
        out_shape=[o_shape, o_shape] if want_lse else o_shape,
        compiler_params=_cparams("parallel", "parallel", "arbitrary"),
        name="band_attention",
    )(*args)
    if want_lse:
        return res[0].reshape(b, s, qw), res[1].reshape(b, s, qw)
    return res.reshape(b, s, qw)


def _sel_kernel(q_ref, k_ref, v_ref, sel_ref, gate_ref, o_ref, *, tq, tk, n_kv, group, n_blocks, gate_col):
    i = pl.program_id(1)
    n_chunks = (i * tq + tq + tk - 1) // tk
    qpos = i * tq + lax.broadcasted_iota(jnp.int32, (tq, tk), 0)
    for j in range(n_kv):
        ks = slice(j * HEAD_DIM, (j + 1) * HEAD_DIM)
        q = jnp.concatenate(
            [(q_ref[0, :, (j * group + g) * HEAD_DIM:(j * group + g + 1) * HEAD_DIM] * ATTN_SCALE).astype(BF16)
             for g in range(group)], axis=0)
        sel = sel_ref[0, j]

        def body(c, carry):
            m, l, acc = carry
            k0 = pl.multiple_of(c * tk, tk)
            k = k_ref[0, pl.ds(k0, tk), ks].astype(BF16)
            v = v_ref[0, pl.ds(k0, tk), ks].astype(BF16)
            kpos = k0 + lax.broadcasted_iota(jnp.int32, (tq, tk), 1)
            blk_of_key = lax.shift_right_logical(k0 + lax.broadcasted_iota(jnp.int32, (n_blocks, tk), 1),
                                                 SEL_BLOCK.bit_length() - 1)
            expand = (blk_of_key == lax.broadcasted_iota(jnp.int32, (n_blocks, tk), 0)).astype(BF16)
            chosen = jnp.dot(sel, expand, preferred_element_type=F32) > 0.5
            mask = chosen & (kpos <= qpos)
            mask = jnp.concatenate([mask] * group, axis=0)
            s = lax.dot_general(q, k, (((1,), (1,)), ((), ())), preferred_element_type=F32)
            s = jnp.where(mask, s, NEG_BIG)
            m_new = jnp.maximum(m, jnp.max(s, axis=-1, keepdims=True))
            p = jnp.where(mask, jnp.exp(s - m_new), 0.0)
            alpha = jnp.exp(m - m_new)
            l = alpha * l + jnp.sum(p, axis=-1, keepdims=True)
            acc = alpha * acc + jnp.dot(p.astype(BF16), v, preferred_element_type=F32)
            return m_new, l, acc

        init = (jnp.full((group * tq, 1), NEG_BIG, F32), jnp.zeros((group * tq, 1), F32),
                jnp.zeros((group * tq, HEAD_DIM), F32))
        _, l, acc = lax.fori_loop(0, n_chunks, body, init)
        o = acc / l
        for g in range(group):
            h = j * group + g
            o_ref[0, :, h * HEAD_DIM:(h + 1) * HEAD_DIM] = (
                gate_ref[0, :, gate_col + h:gate_col + h + 1] * o[g * tq:(g + 1) * tq])


def selected_attention_prompt(q_rope, y_main, selmask, gates, *, k_idx, v_idx, gate_col):
    b, t, qw = q_rope.shape
    n_blocks = selmask.shape[-1]
    kw = N_KV_C * HEAD_DIM
    tq, tk = SEL_Q_TILE, SEL_K_TILE
    return pl.pallas_call(
        functools.partial(_sel_kernel, tq=tq, tk=tk, n_kv=N_KV_C, group=N_Q_C // N_KV_C,
                          n_blocks=n_blocks, gate_col=gate_col),
        grid=(b, t // tq),
        in_specs=[pl.BlockSpec((1, tq, qw), lambda bi, i: (bi, i, 0)),
                  pl.BlockSpec((1, t, kw), lambda bi, i: (bi, 0, k_idx)),
                  pl.BlockSpec((1, t, kw), lambda bi, i: (bi, 0, v_idx)),
                  pl.BlockSpec((1, N_KV_C, tq, n_blocks), lambda bi, i: (bi, 0, i, 0)),
                  pl.BlockSpec((1, tq, gates.shape[2]), lambda bi, i: (bi, i, 0))],
        out_specs=pl.BlockSpec((1, tq, qw), lambda bi, i: (bi, i, 0)),
        out_shape=jax.ShapeDtypeStruct((b, t, qw), F32),
        compiler_params=_cparams("parallel", "arbitrary"),
        name="selected_attention",
    )(q_rope, y_main, y_main, selmask, gates)


def _sel_sample_kernel(page_ref, sub_ref, ok_ref, q_ref, qbd_ref, knew_ref, vnew_ref, gate_ref, p0, p1, p2, p3,
                       o_ref, m_s, l_s, acc_s, *, n_sel):
    bi, s = pl.program_id(0), pl.program_id(1)
    n_q, grp = N_Q_C, N_Q_C // N_KV_C

    @pl.when(s == 0)
    def _():
        m_s[...] = jnp.sum(q_ref[0] * ATTN_SCALE * knew_ref[0], axis=-1, keepdims=True)
        l_s[...] = jnp.ones_like(l_s)
        acc_s[...] = vnew_ref[0]

    @pl.when(s > 0)
    def _():
        pages = (p0, p1, p2, p3)
        k_t = jnp.concatenate([pages[j][0, 0, 0] for j in range(N_KV_C)], axis=0).astype(BF16)
        v_t = jnp.concatenate([pages[j][0, 1, 0] for j in range(N_KV_C)], axis=0).astype(BF16)
        sc = jnp.dot((qbd_ref[0] * ATTN_SCALE).astype(BF16), k_t, preferred_element_type=F32)
        shape = (n_q, PAGE_SIZE)
        key_half = lax.shift_right_logical(lax.broadcasted_iota(jnp.int32, shape, 1), SEL_BLOCK.bit_length() - 1)
        q_head = lax.shift_right_logical(lax.broadcasted_iota(jnp.int32, shape, 0), grp.bit_length() - 1)
        want = jnp.full(shape, -1, jnp.int32)
        for j in range(N_KV_C):
            slot = (bi * N_KV_C + j) * n_sel + s
            want = jnp.where(q_head == j, jnp.where(ok_ref[slot] > 0, sub_ref[slot], -1), want)
        mask = key_half == want
        sc = jnp.where(mask, sc, NEG_BIG)
        m_old = m_s[...]
        m_new = jnp.maximum(m_old, jnp.max(sc, axis=-1, keepdims=True))
        p = jnp.where(mask, jnp.exp(sc - m_new), 0.0)
        alpha = jnp.exp(m_old - m_new)
        pv = lax.dot_general(p.astype(BF16), v_t, (((1,), (1,)), ((), ())), preferred_element_type=F32)
        head_of_row = lax.shift_right_logical(lax.broadcasted_iota(jnp.int32, (n_q, HEAD_DIM), 0),
                                              grp.bit_length() - 1)
        o = jnp.zeros((n_q, HEAD_DIM), F32)
        for j in range(N_KV_C):
            o = jnp.where(head_of_row == j, pv[:, j * HEAD_DIM:(j + 1) * HEAD_DIM], o)
        m_s[...] = m_new
        l_s[...] = alpha * l_s[...] + jnp.sum(p, axis=-1, keepdims=True)
        acc_s[...] = alpha * acc_s[...] + o

    @pl.when(s == n_sel - 1)
    def _():
        o_ref[0] = gate_ref[0] * (acc_s[...] / l_s[...])


def selected_attention_sample(q, k_new, v_new, gate, pool, page, sub, ok):
    n, n_q, d = q.shape
    grp = n_q // N_KV_C
    n_sel = page.shape[0] // (n * N_KV_C)
    pool_t = pool.transpose(0, 2, 3, 4, 1)
    eye = jnp.repeat(jnp.eye(N_KV_C, dtype=q.dtype), grp, axis=0)
    q_bd = (q[:, :, None, :] * eye[None, :, :, None]).reshape(n, n_q, N_KV_C * d)
    row_spec = lambda w: pl.BlockSpec((1, n_q, w), lambda bi, s, *_: (bi, 0, 0))

    def page_spec(j):
        return pl.BlockSpec((1, 2, 1, d, PAGE_SIZE),
                            lambda bi, s, pg, *_: (pg[(bi * N_KV_C + j) * n_sel + s], 0, j, 0, 0))

    grid_spec = pltpu.PrefetchScalarGridSpec(
        num_scalar_prefetch=3,
        grid=(n, n_sel),
        in_specs=[row_spec(d), row_spec(N_KV_C * d), row_spec(d), row_spec(d), row_spec(1)]
        + [page_spec(j) for j in range(N_KV_C)],
        out_specs=row_spec(d),
        scratch_shapes=[pltpu.VMEM((n_q, 1), F32), pltpu.VMEM((n_q, 1), F32), pltpu.VMEM((n_q, d), F32)],
    )
    return pl.pallas_call(
        functools.partial(_sel_sample_kernel, n_sel=n_sel),
        grid_spec=grid_spec,
        out_shape=jax.ShapeDtypeStruct((n, n_q, d), F32),
        compiler_params=_cparams("parallel", "arbitrary"),
        name="selected_attention_sample",
    )(page, sub, ok, q, q_bd, k_new, v_new, gate, pool_t, pool_t, pool_t, pool_t)


def _half_proj_kernel(*refs, n_rows):
    n_pairs = N_KV_C // 2
    x_refs, w_ref, o_ref = refs[:2 * n_pairs], refs[2 * n_pairs], refs[2 * n_pairs + 1]
    out_w = 2 * 2 * CMP_HID
    for c in range(2):
        for p in range(n_pairs):
            x_ref = x_refs[c * n_pairs + p]
            acc = jnp.zeros((n_rows, out_w), F32)
            for sp in range(CMP_STRIDE // 2):
                lo = x_ref[0, pl.ds(2 * sp, n_rows, stride=CMP_STRIDE), :]
                hi = x_ref[0, pl.ds(2 * sp + 1, n_rows, stride=CMP_STRIDE), :]
                lhs = jnp.concatenate([lo, hi], axis=1).astype(BF16)
                acc = acc + jnp.dot(lhs, w_ref[c, sp], preferred_element_type=F32)
            o_ref[0, :, (c * n_pairs + p) * out_w:(c * n_pairs + p + 1) * out_w] = acc


def half_block_proj(rows_arr, col_idx, w_bd, tile_rows):
    b, l = rows_arr.shape[:2]
    n_half_tile = tile_rows // CMP_STRIDE
    out_w = 2 * 2 * N_KV_C * CMP_HID
    n_chunks = C_KV_WIDTH // LANES_V7X

    def chunk_spec(cp):
        return pl.BlockSpec((1, tile_rows, LANES_V7X), lambda bi, i: (bi, i, col_idx * n_chunks + cp))

    return pl.pallas_call(
        functools.partial(_half_proj_kernel, n_rows=n_half_tile),
        grid=(b, l // tile_rows),
        in_specs=[chunk_spec(cp) for cp in range(n_chunks)] + [pl.BlockSpec(w_bd.shape, lambda bi, i: (0, 0, 0, 0))],
        out_specs=pl.BlockSpec((1, n_half_tile, out_w), lambda bi, i: (bi, i, 0)),
        out_shape=jax.ShapeDtypeStruct((b, l // CMP_STRIDE, out_w), F32),
        compiler_params=_cparams("parallel", "parallel"),
        name="half_block_proj",
    )(*([rows_arr] * n_chunks), w_bd)


def _paged_half_proj_kernel(pt_ref, *refs, n_pages):
    del pt_ref
    n_pairs = N_KV_C // 2
    pages, w_ref, o_ref = refs[:n_pages], refs[n_pages], refs[n_pages + 1]
    rows = refs[n_pages + 2:]
    for j, page in enumerate(pages):
        for c in range(2):
            for p in range(n_pairs):
                x_t = page[0, c, 2 * p:2 * p + 2].reshape(2 * HEAD_DIM, PAGE_SIZE)
                rows[c * n_pairs + p][j * PAGE_SIZE:(j + 1) * PAGE_SIZE, :] = x_t.T
    n_rows = n_pages * PAGE_SIZE // CMP_STRIDE
    out_w = 2 * 2 * CMP_HID
    for c in range(2):
        for p in range(n_pairs):
            x_ref = rows[c * n_pairs + p]
            acc = jnp.zeros((n_rows, out_w), F32)
            for sp in range(CMP_STRIDE // 2):
                lo = x_ref[pl.ds(2 * sp, n_rows, stride=CMP_STRIDE), :]
                hi = x_ref[pl.ds(2 * sp + 1, n_rows, stride=CMP_STRIDE), :]
                lhs = jnp.concatenate([lo, hi], axis=1).astype(BF16)
                acc = acc + jnp.dot(lhs, w_ref[c, sp], preferred_element_type=F32)
            o_ref[0, :, (c * n_pairs + p) * out_w:(c * n_pairs + p + 1) * out_w] = acc


def paged_half_block_proj(pool, page_table, w_bd, n_pages):
    n, pages_per_req = page_table.shape
    pool_t = pool.transpose(0, 2, 3, 4, 1)
    steps = pages_per_req // n_pages
    n_half = n_pages * PAGE_SIZE // CMP_STRIDE
    out_w = 2 * 2 * N_KV_C * CMP_HID

    def page_spec(j):
        return pl.BlockSpec((1, 2, N_KV_C, HEAD_DIM, PAGE_SIZE),
                            lambda bi, i, pt: (pt[bi * pages_per_req + i * n_pages + j], 0, 0, 0, 0))

    grid_spec = pltpu.PrefetchScalarGridSpec(
        num_scalar_prefetch=1,
        grid=(n, steps),
        in_specs=[page_spec(j) for j in range(n_pages)] + [pl.BlockSpec(w_bd.shape, lambda bi, i, pt: (0, 0, 0, 0))],
        out_specs=pl.BlockSpec((1, n_half, out_w), lambda bi, i, pt: (bi, i, 0)),
        scratch_shapes=[pltpu.VMEM((n_pages * PAGE_SIZE, LANES_V7X), F32) for _ in range(C_KV_WIDTH // LANES_V7X)],
    )
    return pl.pallas_call(
        functools.partial(_paged_half_proj_kernel, n_pages=n_pages),
        grid_spec=grid_spec,
        out_shape=jax.ShapeDtypeStruct((n, pages_per_req * PAGE_SIZE // CMP_STRIDE, out_w), F32),
        compiler_params=_cparams("parallel", "arbitrary"),
        name="paged_half_block_proj",
    )(page_table.reshape(-1), *([pool_t] * n_pages), w_bd)


def _half_proj_weight(w1h):
    r = CMP_LEN // CMP_STRIDE
    eye = jnp.eye(2, dtype=F32)
    w = w1h.reshape(2, r, CMP_STRIDE // 2, 2, HEAD_DIM, CMP_HID)
    w = jnp.einsum('cjpldh,kq->cplkdqjh', w, eye)
    return w.reshape(2, CMP_STRIDE // 2, 4 * HEAD_DIM, 2 * r * CMP_HID).astype(BF16)


def _moe_in_kernel(x_ref, g_ref, sh_ref, sc_ref, rw_ref, sg_ref, su_ref, sd_ref, h_ref, score_ref, shared_ref):
    h = _modulated_norm(x_ref[0], g_ref[...], sc_ref[0], sh_ref[0])
    logits = jnp.dot(h, rw_ref[...], preferred_element_type=F32, precision=lax.Precision.HIGHEST)
    score_ref[0] = jax.nn.sigmoid(logits)
    h_ref[0] = h
    hb = h.astype(BF16)
    gate = jnp.dot(hb, sg_ref[...], preferred_element_type=F32)
    up = jnp.dot(hb, su_ref[...], preferred_element_type=F32)
    mid = (jax.nn.silu(gate) * up).astype(BF16)
    shared_ref[0] = jnp.dot(mid, sd_ref[...], preferred_element_type=F32)


def moe_in(x, g, shift, scale, router_w, sg_bf, su_bf, sd_bf, tm):
    b, t, k = x.shape
    per_row = shift.shape[1] != 1
    mod_spec = (pl.BlockSpec((1, tm, k), lambda bi, i: (bi, i, 0)) if per_row
                else pl.BlockSpec((1, 1, k), lambda bi, i: (bi, 0, 0)))
    row_spec = lambda wd: pl.BlockSpec((1, tm, wd), lambda bi, i: (bi, i, 0))
    full = lambda a: pl.BlockSpec(a.shape, lambda bi, i: (0,) * a.ndim)
    return pl.pallas_call(
        _moe_in_kernel,
        grid=(b, t // tm),
        in_specs=[row_spec(k), pl.BlockSpec((1, k), lambda bi, i: (0, 0)), mod_spec, mod_spec,
                  full(router_w), full(sg_bf), full(su_bf), full(sd_bf)],
        out_specs=[row_spec(k), row_spec(N_EXPERTS), row_spec(k)],
        out_shape=[jax.ShapeDtypeStruct((b, t, k), F32), jax.ShapeDtypeStruct((b, t, N_EXPERTS), F32),
                   jax.ShapeDtypeStruct((b, t, k), F32)],
        compiler_params=_cparams("parallel", "parallel"),
        name="moe_in",
    )(x, g.reshape(1, k), shift, scale, router_w, sg_bf, su_bf, sd_bf)


def _gmm_kernel(blk_e_ref, n_used_ref, x_ref, wg_ref, wu_ref, wd_ref, o_ref, wg_s, wu_s, wd_s):
    blk = pl.program_id(0)
    prev_e = blk_e_ref[jnp.maximum(blk - 1, 0)]
    new_expert = (blk == 0) | (blk_e_ref[blk] != prev_e)

    @pl.when(new_expert)
    def _():
        wg_s[...] = wg_ref[0].astype(BF16)
        wu_s[...] = wu_ref[0].astype(BF16)
        wd_s[...] = wd_ref[0].astype(BF16)

    @pl.when(blk < n_used_ref[0])
    def _():
        x = x_ref[...].astype(BF16)
        gate = jnp.dot(x, wg_s[...], preferred_element_type=F32)
        up = jnp.dot(x, wu_s[...], preferred_element_type=F32)
        mid = (jax.nn.silu(gate) * up).astype(BF16)
        o_ref[...] = jnp.dot(mid, wd_s[...], preferred_element_type=F32)

    @pl.when(blk >= n_used_ref[0])
    def _():
        o_ref[...] = jnp.zeros_like(o_ref)


def grouped_experts(xs, blk_e, n_used, w_gate, w_up, w_down):
    cap, k = xs.shape
    tm = MOE_TILE
    n_blk = cap // tm
    de = w_gate.shape[-1]
    grid_spec = pltpu.PrefetchScalarGridSpec(
        num_scalar_prefetch=2,
        grid=(n_blk,),
        in_specs=[pl.BlockSpec((tm, k), lambda i, be, nu: (i, 0)),
                  pl.BlockSpec((1, k, de), lambda i, be, nu: (be[i], 0, 0)),
                  pl.BlockSpec((1, k, de), lambda i, be, nu: (be[i], 0, 0)),
                  pl.BlockSpec((1, de, k), lambda i, be, nu: (be[i], 0, 0))],
        out_specs=pl.BlockSpec((tm, k), lambda i, be, nu: (i, 0)),
        scratch_shapes=[pltpu.VMEM((k, de), BF16), pltpu.VMEM((k, de), BF16), pltpu.VMEM((de, k), BF16)],
    )
    return pl.pallas_call(
        _gmm_kernel,
        grid_spec=grid_spec,
        out_shape=jax.ShapeDtypeStruct((cap, k), F32),
        compiler_params=_cparams("arbitrary"),
        name="grouped_experts",
    )(blk_e, n_used, xs, w_gate, w_up, w_down)


def _moe_out_kernel(x_ref, gate_ref, picked_ref, ew_ref, shared_ref, o_ref):
    routed = picked_ref[0] * ew_ref[:, 0:1]
    for k in range(1, TOP_K):
        routed = routed + picked_ref[k] * ew_ref[:, k:k + 1]
    o_ref[0] = x_ref[0] + gate_ref[0] * (routed + shared_ref[0])


def moe_out(x, gate, picked, ew, shared, tm, row0):
    b, t, d = x.shape
    per_row = gate.shape[1] != 1
    gate_spec = (pl.BlockSpec((1, tm, d), lambda bi, i: (bi, i, 0)) if per_row
                 else pl.BlockSpec((1, 1, d), lambda bi, i: (bi, 0, 0)))
    row_spec = pl.BlockSpec((1, tm, d), lambda bi, i: (bi, i, 0))
    tiles, off = t // tm, row0 // tm
    return pl.pallas_call(
        _moe_out_kernel,
        grid=(b, tiles),
        in_specs=[row_spec, gate_spec,
                  pl.BlockSpec((TOP_K, tm, d), lambda bi, i: (0, off + bi * tiles + i, 0)),
                  pl.BlockSpec((tm, TOP_K), lambda bi, i: (off + bi * tiles + i, 0)),
                  row_spec],
        out_specs=row_spec,
        out_shape=jax.ShapeDtypeStruct((b, t, d), F32),
        compiler_params=_cparams("parallel", "parallel"),
        name="moe_out",
    )(x, gate, picked, ew, shared)


def _final_norm_kernel(x_ref, g_ref, o_ref):
    x = x_ref[0]
    var = jnp.mean(x * x, axis=-1, keepdims=True)
    o_ref[0] = x * lax.rsqrt(var + RMS_EPS) * g_ref[...]


def final_norm(x, g, tm):
    b, t, d = x.shape
    row_spec = pl.BlockSpec((1, tm, d), lambda bi, i: (bi, i, 0))
    return pl.pallas_call(
        _final_norm_kernel,
        grid=(b, t // tm),
        in_specs=[row_spec, pl.BlockSpec((1, d), lambda bi, i: (0, 0))],
        out_specs=row_spec,
        out_shape=jax.ShapeDtypeStruct((b, t, d), F32),
        compiler_params=_cparams("parallel", "parallel"),
        name="final_norm",
    )(x, g.reshape(1, d))


def _ada_kernel(c_ref, w_ref, b_ref, o_ref):
    a = jax.nn.silu(c_ref[...]).astype(BF16)
    o_ref[0] = jnp.dot(a, w_ref[0].astype(BF16), preferred_element_type=F32) + b_ref[0]


def ada_mods(c_all, ada_w, ada_b):
    n, d = c_all.shape
    depth, _, width = ada_w.shape
    tn = 1024
    return pl.pallas_call(
        _ada_kernel,
        grid=(depth, width // tn),
        in_specs=[pl.BlockSpec((n, d), lambda l, j: (0, 0)),
                  pl.BlockSpec((1, d, tn), lambda l, j: (l, 0, j)),
                  pl.BlockSpec((1, 1, tn), lambda l, j: (l, 0, j))],
        out_specs=pl.BlockSpec((1, n, tn), lambda l, j: (l, 0, j)),
        out_shape=jax.ShapeDtypeStruct((depth, n, width), F32),
        compiler_params=_cparams("parallel", "parallel"),
        name="ada_mods",
    )(c_all, ada_w, ada_b.reshape(depth, 1, width))


def _masked_softmax(s, mask, sink=None):
    s = jnp.where(mask, s, -jnp.inf)
    m = jnp.max(s, axis=-1, keepdims=True)
    if sink is not None:
        m = jnp.maximum(m, sink)
    m = jnp.where(jnp.isfinite(m), m, 0.0)
    e = jnp.exp(s - m)
    den = jnp.sum(e, axis=-1, keepdims=True)
    if sink is not None:
        den = den + jnp.exp(sink - m)
    p = e / jnp.maximum(den, 1e-30)
    return p, (m + jnp.log(den))[..., 0]


def _attend(q, k, v, mask, sink=None):
    s = jnp.einsum('bqhgd,bkhd->bhgqk', q, k, preferred_element_type=F32) * ATTN_SCALE
    p, lse = _masked_softmax(s, mask, sink)
    return jnp.einsum('bhgqk,bkhd->bqhgd', p.astype(v.dtype), v), lse


def _attend_gathered(q, k, v, mask):
    s = jnp.einsum('bqhgd,bhqnd->bhgqn', q, k, preferred_element_type=F32) * ATTN_SCALE
    p, lse = _masked_softmax(s, mask)
    return jnp.einsum('bhgqn,bhqnd->bqhgd', p.astype(v.dtype), v), lse


def _window_attend_with_buffer(q, kv, buf, window, sink=None):
    lb, t = buf.shape[1], kv.shape[1]
    allkv = jnp.concatenate([buf, kv], axis=1)
    kpos = PAST_LEN - lb + jnp.arange(lb + t)
    qpos = PAST_LEN + jnp.arange(t)
    diff = qpos[:, None] - kpos[None, :]
    mask = (diff >= 0) & (diff <= window)
    o, lse = _attend(q, allkv[:, :, 0], allkv[:, :, 1], mask, sink)
    return o, lse, allkv[:, -min(window, PAST_LEN + t):]


def _dilated_sample(q, kv_new, buf, win, dil):
    lb, t = buf.shape[1], kv_new.shape[1]
    allkv = jnp.concatenate([buf, kv_new], axis=1)
    n_taps = win // dil + 1
    idx = lb + jnp.arange(t)[:, None] - dil * jnp.arange(n_taps)[None, :]
    valid = idx >= 0
    g = allkv[:, jnp.maximum(idx, 0)]
    k = g[:, :, :, 0].transpose(0, 3, 1, 2, 4)
    v = g[:, :, :, 1].transpose(0, 3, 1, 2, 4)
    o, lse = _attend_gathered(q[:, :, :, None], k, v, valid)
    return o[:, :, :, 0], lse[:, :, 0].transpose(0, 2, 1), allkv[:, -min(win, PAST_LEN + t):]


def _cmp_to_sel(p, n_slc):
    ratio = SEL_BLOCK // CMP_STRIDE
    left = CMP_LEN // CMP_STRIDE - 1
    pad = [(0, 0)] * (p.ndim - 1) + [(left, ratio * n_slc - p.shape[-1])]
    pp = jnp.pad(p, pad)
    out = pp[..., 0:ratio * n_slc:ratio]
    for o in range(1, ratio + left):
        out = out + pp[..., o:o + ratio * n_slc:ratio]
    return out


def _select_blocks(imp, pos, n_slc):
    k_top = N_SEL - N_FORCED
    cur = pos // SEL_BLOCK
    j = jnp.arange(n_slc)
    cand = (j[None, :] >= 1) & (j[None, :] <= cur[:, None] - 2)
    sc = jnp.where(cand, imp, -jnp.inf)
    if n_slc < k_top:
        sc = jnp.pad(sc, ((0, 0), (0, 0), (0, 0), (0, k_top - n_slc)), constant_values=-jnp.inf)
    vals, idx = lax.top_k(sc, k_top)
    forced = jnp.stack([cur, cur - 1, jnp.zeros_like(cur)], axis=-1)
    forced_ok = jnp.stack([cur >= 0, cur >= 1, cur >= 2], axis=-1)
    lead = imp.shape[:2]
    blocks = jnp.concatenate([jnp.broadcast_to(forced, lead + forced.shape), idx.astype(cur.dtype)], axis=-1)
    ok = jnp.concatenate([jnp.broadcast_to(forced_ok, lead + forced_ok.shape), vals > -jnp.inf], axis=-1)
    return jnp.clip(blocks, 0, n_slc - 1), ok


def _compress(hp, n_cmp, w1h, b1, w2, pe):
    b = hp.shape[0]
    r = CMP_LEN // CMP_STRIDE
    hp = hp.reshape(b, hp.shape[1], 2, N_KV_C, r, CMP_HID)
    pre = (jnp.einsum('cjsd,cjsdh->ch', pe.reshape(2, r, CMP_STRIDE, HEAD_DIM), w1h) + b1)[:, None, :]
    for j in range(r):
        pre = pre + hp[:, j:j + n_cmp, :, :, j]
    return jnp.einsum('bnckh,chd->bnckd', jax.nn.gelu(pre), w2)


def _compressed_attention(q_cmp, comp, pos):
    blk_end = jnp.arange(comp.shape[1]) * CMP_STRIDE + CMP_LEN - 1
    s = jnp.einsum('bqhgd,bnhd->bhgqn', q_cmp, comp[:, :, 0], preferred_element_type=F32) * ATTN_SCALE
    p_cmp, _ = _masked_softmax(s, blk_end[None, :] <= pos[:, None])
    o_cmp = jnp.einsum('bhgqn,bnhd->bqhgd', p_cmp.astype(comp.dtype), comp[:, :, 1])
    return o_cmp, p_cmp


A_SPECS = ((0, (N_Q_A + N_KV_A) * HEAD_DIM, 'rope', 0, 0),
           ((N_Q_A + N_KV_A) * HEAD_DIM, (N_Q_A + 2 * N_KV_A) * HEAD_DIM, 'raw', 0, (N_Q_A + N_KV_A) * HEAD_DIM))
B_SPECS = ((0, 2 * B_WIDTH, 'rope', 0, 0), (2 * B_WIDTH, 3 * B_WIDTH, 'raw', 0, 2 * B_WIDTH))
_C0 = C_Q_WIDTH + C_KV_WIDTH
_C1 = _C0 + C_KV_WIDTH
C_SPECS = ((0, _C0, 'raw', 0, 0),
           (_C0, _C0 + C_KV_WIDTH // 2, 'rope', 0, _C0), (_C0 + C_KV_WIDTH // 2, _C1, 'raw', 0, _C0 + C_KV_WIDTH // 2),
           (_C1, _C1 + C_KV_WIDTH // 2, 'rope', 0, _C1), (_C1 + C_KV_WIDTH // 2, C_MAIN_WIDTH, 'raw', 0, _C1 + C_KV_WIDTH // 2),
           (0, C_Q_WIDTH, 'rope', 1, 0),
           (C_MAIN_WIDTH, C_MAIN_WIDTH + N_GATES_C, 'sigmoid', 2, 0))


def _mixer_a(x, mods, g, w_qkv_bf, w_o_bf, sink, tabs, tm, past):
    shift, scale, gate = mods
    nq = N_Q_A * HEAD_DIM
    kvw = N_KV_A * HEAD_DIM
    (y,) = norm_linear(x, g, shift, scale, w_qkv_bf, tabs, A_SPECS, (nq + 2 * kvw,), tm)
    if past is None:
        o = band_attention(y, y, y, dil=1, n_kv=N_KV_A, group=N_Q_A // N_KV_A, tq=WINDOW_A,
                           q_idx=0, k_idx=nq // kvw, v_idx=nq // kvw + 1, sink=sink)
        b, t = y.shape[:2]
        new = y[:, -min(WINDOW_A, t):, nq:].reshape(b, -1, 2, N_KV_A, HEAD_DIM)
    else:
        n = y.shape[1]
        q = y[0, :, :nq].reshape(n, 1, N_KV_A, N_Q_A // N_KV_A, HEAD_DIM)
        kv = y[0, :, nq:].reshape(n, 1, 2, N_KV_A, HEAD_DIM)
        sink_b = sink.astype(F32).reshape(1, N_KV_A, N_Q_A // N_KV_A, 1, 1)
        o, _, new = _window_attend_with_buffer(q, kv, past, WINDOW_A, sink_b)
        o = o.reshape(1, n, nq)
    return linear_out([o], w_o_bf, x, gate, 'plain', tm), new


def _mixer_b(x, mods, g, w_qkv_bf, w_o_bf, tabs, tm, past):
    shift, scale, gate = mods
    hpg = B_HEADS_PER_GROUP
    gw = hpg * HEAD_DIM
    (y,) = norm_linear(x, g, shift, scale, w_qkv_bf, tabs, B_SPECS, (3 * B_WIDTH,), tm)
    outs, lses, news = [], [], []
    if past is None:
        b, t = y.shape[:2]
        for gi, (win, dil) in enumerate(B_PATTERNS):
            o, lse = band_attention(y, y, y, dil=dil, n_kv=hpg, group=1, tq=win // dil,
                                    q_idx=gi, k_idx=N_GROUPS_B + gi, v_idx=2 * N_GROUPS_B + gi, want_lse=True)
            outs.append(o)
            lses.append(lse)
            k = y[:, -min(win, t):, B_WIDTH + gi * gw:B_WIDTH + (gi + 1) * gw]
            v = y[:, -min(win, t):, 2 * B_WIDTH + gi * gw:2 * B_WIDTH + (gi + 1) * gw]
            news.append(jnp.stack([k, v], axis=2).reshape(b, -1, 2, hpg, HEAD_DIM))
    else:
        n = y.shape[1]
        y3 = y[0].reshape(n, 1, 3, N_GROUPS_B * hpg, HEAD_DIM)
        for gi, (win, dil) in enumerate(B_PATTERNS):
            hs = slice(gi * hpg, (gi + 1) * hpg)
            kv_g = jnp.stack([y3[:, :, 1, hs], y3[:, :, 2, hs]], axis=2)
            o, lse, new = _dilated_sample(y3[:, :, 0, hs], kv_g, past[gi], win, dil)
            outs.append(o.reshape(1, n, gw))
            lses.append(jnp.broadcast_to(lse.reshape(1, n, hpg, 1), (1, n, hpg, HEAD_DIM)).reshape(1, n, gw))
            news.append(new)
    return linear_out(outs + lses, w_o_bf, x, gate, 'mix3', tm), tuple(news)


def _mixer_c(x, mods, g, w_in_bf, w_o_bf, cmp_w1, cmp_b1, cmp_w2, cmp_pe, tabs, pos, tm, past):
    shift, scale, gate = mods
    grp = N_Q_C // N_KV_C
    kvw = N_KV_C * HEAD_DIM
    y, q_rope, gates = norm_linear(x, g, shift, scale, w_in_bf, tabs, C_SPECS,
                                   (C_MAIN_WIDTH, C_Q_WIDTH, N_GATES_C), tm)
    w1h = cmp_w1.reshape(2, CMP_LEN // CMP_STRIDE, CMP_STRIDE, HEAD_DIM, CMP_HID)
    w_bd = _half_proj_weight(w1h)
    cmp_idx, slc_idx, win_idx = C_Q_WIDTH // C_KV_WIDTH, _C0 // C_KV_WIDTH, _C1 // C_KV_WIDTH
    if past is None:
        b, t = y.shape[:2]
        hp = half_block_proj(y, cmp_idx, w_bd, t)
        n_cmp = (t - CMP_LEN) // CMP_STRIDE + 1
        comp = _compress(hp, n_cmp, w1h, cmp_b1, cmp_w2, cmp_pe)
        q_cmp = y[..., :C_Q_WIDTH].reshape(b, t, N_KV_C, grp, HEAD_DIM)
        o_cmp, p_cmp = _compressed_attention(q_cmp, comp, pos)
        gview = gates.reshape(b, t, 3, N_KV_C, grp, 1)
        o_cmp = (gview[:, :, 0] * o_cmp.astype(F32)).reshape(b, t, C_Q_WIDTH)
        n_slc = -(-t // SEL_BLOCK)
        blocks, ok = _select_blocks(_cmp_to_sel(p_cmp.sum(axis=2), n_slc), pos, n_slc)
        onehot = (blocks[..., None] == jnp.arange(n_slc)) & ok[..., None]
        selmask = jnp.any(onehot, axis=-2).astype(BF16)
        o_slc = selected_attention_prompt(q_rope, y, selmask, gates, k_idx=2 * slc_idx, v_idx=2 * slc_idx + 1,
                                          gate_col=N_Q_C)
        o_win = band_attention(q_rope, y, y, dil=1, n_kv=N_KV_C, group=grp, tq=WINDOW_C, q_idx=0,
                               k_idx=2 * win_idx, v_idx=2 * win_idx + 1, gate=gates, gate_col=2 * N_Q_C)
        new_win = y[:, -min(WINDOW_C, t):, _C1:].reshape(b, -1, 2, N_KV_C, HEAD_DIM)
        cmp_kv = y[..., C_Q_WIDTH:_C0].reshape(b, t, 2, N_KV_C, HEAD_DIM)
        slc_kv = y[..., _C0:_C1].reshape(b, t, 2, N_KV_C, HEAD_DIM)
    else:
        win_buf, cmp_pool, slc_pool, page_table = past
        n = y.shape[1]
        t = 1
        cmp_kv = y[0, :, C_Q_WIDTH:_C0].reshape(n, t, 2, N_KV_C, HEAD_DIM)
        slc_kv = y[0, :, _C0:_C1].reshape(n, t, 2, N_KV_C, HEAD_DIM)
        win_kv = y[0, :, _C1:].reshape(n, t, 2, N_KV_C, HEAD_DIM)
        q_cmp = y[0, :, :C_Q_WIDTH].reshape(n, t, N_KV_C, grp, HEAD_DIM)
        q = q_rope[0].reshape(n, t, N_KV_C, grp, HEAD_DIM)
        gview = gates[0].reshape(n, t, 3, N_KV_C, grp, 1)
        hp = paged_half_block_proj(cmp_pool, page_table, w_bd, CMP_PAGES_PER_STEP)
        length = PAST_LEN + t
        n_cmp = (length - CMP_LEN) // CMP_STRIDE + 1
        comp = _compress(hp, n_cmp, w1h, cmp_b1, cmp_w2, cmp_pe)
        o_cmp, p_cmp = _compressed_attention(q_cmp, comp, pos)
        n_slc = -(-length // SEL_BLOCK)
        blocks, ok = _select_blocks(_cmp_to_sel(p_cmp.sum(axis=2), n_slc), pos, n_slc)
        sub = PAGE_SIZE // SEL_BLOCK
        pb = jnp.minimum(blocks[:, :, 0], PAST_LEN // SEL_BLOCK - 1)
        page = jnp.take_along_axis(jnp.broadcast_to(page_table[:, None], (n, N_KV_C, page_table.shape[1])),
                                   pb // sub, axis=2)
        page = page.at[:, :, 0].set(page[:, :, 1])
        o_slc = selected_attention_sample(
            q_rope[0].reshape(n, N_Q_C, HEAD_DIM), jnp.repeat(slc_kv[:, 0, 0], grp, axis=1),
            jnp.repeat(slc_kv[:, 0, 1], grp, axis=1), gates[0, :, N_Q_C:2 * N_Q_C].reshape(n, N_Q_C, 1),
            slc_pool, page.reshape(-1).astype(jnp.int32), (pb % sub).reshape(-1).astype(jnp.int32),
            ok[:, :, 0].reshape(-1).astype(jnp.int32))
        o_slc = o_slc.reshape(1, n, C_Q_WIDTH)
        o_win, _, new_win = _window_attend_with_buffer(q, win_kv, win_buf, WINDOW_C)
        o_cmp = (gview[:, :, 0] * o_cmp.astype(F32)).reshape(1, n, C_Q_WIDTH)
        o_win = (gview[:, :, 2] * o_win.astype(F32)).reshape(1, n, C_Q_WIDTH)
    return linear_out([o_cmp, o_slc, o_win], w_o_bf, x, gate, 'sum3', tm), (new_win, cmp_kv, slc_kv)


def _route(scores, router_b):
    n = scores.shape[0]
    per = N_EXPERTS // N_EXPERT_GROUPS
    biased = scores + router_b.astype(F32)
    grp = lax.top_k(biased.reshape(n, N_EXPERT_GROUPS, per), 2)[0].sum(-1)
    _, gidx = lax.top_k(grp, TOPK_GROUPS)
    gmask = jnp.any(gidx[:, :, None] == jnp.arange(N_EXPERT_GROUPS), axis=1)
    _, eidx = lax.top_k(jnp.where(jnp.repeat(gmask, per, axis=1), biased, -jnp.inf), TOP_K)
    ew = jnp.take_along_axis(scores, eidx, axis=-1)
    ew = ew / jnp.sum(ew, axis=-1, keepdims=True) * ROUTED_SCALE
    return eidx, ew


def _dispatch_plan(eidx, n_blk):
    n = eidx.shape[0]
    tm = MOE_TILE
    chunk = LANES_V7X
    onehot = (eidx[:, :, None] == jnp.arange(N_EXPERTS)).astype(jnp.int32)
    sel = onehot.sum(axis=1).astype(F32).reshape(n // chunk, chunk, N_EXPERTS)
    tril = jnp.tril(jnp.ones((chunk, chunk), F32))
    within = jnp.einsum('ij,cjk->cik', tril, sel)
    chunk_tot = within[:, -1, :]
    before = jnp.cumsum(chunk_tot, axis=0) - chunk_tot
    rank = (within - sel + before[:, None, :]).reshape(n, N_EXPERTS).astype(jnp.int32)
    counts = jnp.sum(chunk_tot, axis=0).astype(jnp.int32)
    padded = (counts + tm - 1) // tm * tm
    ends = jnp.cumsum(padded)
    dest = jnp.sum(onehot * (ends - padded + rank)[:, None, :], axis=-1)
    tok = jnp.broadcast_to(jnp.arange(n, dtype=jnp.int32)[:, None], dest.shape)
    row_tok = jnp.zeros((n_blk * tm,), jnp.int32).at[dest.reshape(-1)].set(tok.reshape(-1))
    blk_start = jnp.arange(n_blk, dtype=jnp.int32) * tm
    blk_e = jnp.minimum(jnp.sum(ends[None, :] <= blk_start[:, None], axis=1), N_EXPERTS - 1).astype(jnp.int32)
    n_used = (ends[-1] // tm).astype(jnp.int32).reshape(1)
    return dest, row_tok, blk_e, n_used


def _moe(xp, xs, mods_p, mods_s, g, router_w, router_b, w_gate, w_up, w_down, sg_bf, su_bf, sd_bf):
    hp, sp, shp = moe_in(xp, g, mods_p[0], mods_p[1], router_w, sg_bf, su_bf, sd_bf, ROW_TILE)
    hs, ss, shs = moe_in(xs, g, mods_s[0], mods_s[1], router_w, sg_bf, su_bf, sd_bf, xs.shape[1])
    d = xp.shape[-1]
    n_p = xp.shape[0] * xp.shape[1]
    h_all = jnp.concatenate([hp.reshape(-1, d), hs.reshape(-1, d)], axis=0)
    scores = jnp.concatenate([sp.reshape(-1, N_EXPERTS), ss.reshape(-1, N_EXPERTS)], axis=0)
    eidx, ew = _route(scores, router_b)
    n = h_all.shape[0]
    n_blk = (n * TOP_K + N_EXPERTS * (MOE_TILE - 1)) // MOE_TILE + 1
    dest, row_tok, blk_e, n_used = _dispatch_plan(eidx, n_blk)
    ys = grouped_experts(h_all[row_tok], blk_e, n_used, w_gate, w_up, w_down)
    picked = ys[dest.T.reshape(-1)].reshape(TOP_K, n, d)
    xp = moe_out(xp, mods_p[2], picked, ew, shp, MOE_OUT_TILE, 0)
    xs = moe_out(xs, mods_s[2], picked, ew, shs, xs.shape[1], n_p)
    return xp, xs


def kernel(x_prompt, x_sample, c_prompt, c_sample, cache_a_kv, cache_b_kv_w128, cache_b_kv_w512, cache_b_kv_w2048, cache_c_win_kv, cache_c_cmp_kv, cache_c_slc_kv, page_table, norm_g, final_g, ada_w, ada_b, a_w_qkv, a_w_o, a_sink, b_w_qkv, b_w_o, c_w_in, c_w_o, c_cmp_w1, c_cmp_b1, c_cmp_w2, c_cmp_pe, moe_router, moe_bias, moe_w_gate, moe_w_up, moe_w_down, shared_w_gate, shared_w_up, shared_w_down):
    bp, seq, d = x_prompt.shape
    ns = x_sample.shape[0]
    b_caches = (cache_b_kv_w128, cache_b_kv_w512, cache_b_kv_w2048)
    pos_p = jnp.arange(seq, dtype=jnp.int32)
    pos_s = PAST_LEN + jnp.arange(x_sample.shape[1], dtype=jnp.int32)
    tabs_p = _rope_tables(pos_p, seq)
    tabs_s = _rope_tables(pos_s, ns)

    mods = ada_mods(jnp.concatenate([c_prompt, c_sample], axis=0), ada_w, ada_b)
    mods_p = mods[:, :bp].reshape(DEPTH, bp, 6, 1, d)
    mods_s = mods[:, bp:].reshape(DEPTH, 1, ns, 6, d)

    xp = x_prompt
    xs = x_sample.reshape(1, ns, d)
    st_p = {0: [], 1: [], 2: []}
    st_s = {0: [], 1: [], 2: []}
    for l in range(DEPTH):
        kind, slot = LAYER_KIND[l], LAYER_SLOT[l]
        mp = [mods_p[l, :, i] for i in range(6)]
        ms = [mods_s[l, :, :, i] for i in range(6)]
        g_mix, g_moe = norm_g[l, 0], norm_g[l, 1]
        if kind == 0:
            w_in, w_o = a_w_qkv[slot].astype(BF16), a_w_o[slot].astype(BF16)
            xp, sp = _mixer_a(xp, mp[:3], g_mix, w_in, w_o, a_sink[slot], tabs_p, ROW_TILE, None)
            xs, ss = _mixer_a(xs, ms[:3], g_mix, w_in, w_o, a_sink[slot], tabs_s, ns, cache_a_kv[slot])
        elif kind == 1:
            w_in, w_o = b_w_qkv[slot].astype(BF16), b_w_o[slot].astype(BF16)
            xp, sp = _mixer_b(xp, mp[:3], g_mix, w_in, w_o, tabs_p, ROW_TILE, None)
            xs, ss = _mixer_b(xs, ms[:3], g_mix, w_in, w_o, tabs_s, ns, tuple(buf[slot] for buf in b_caches))
        else:
            w_in, w_o = c_w_in[slot].astype(BF16), c_w_o[slot].astype(BF16)
            cargs = (c_cmp_w1[slot], c_cmp_b1[slot], c_cmp_w2[slot], c_cmp_pe[slot])
            xp, sp = _mixer_c(xp, mp[:3], g_mix, w_in, w_o, *cargs, tabs_p, pos_p, ROW_TILE, None)
            xs, ss = _mixer_c(xs, ms[:3], g_mix, w_in, w_o, *cargs, tabs_s, pos_s, ns,
                              (cache_c_win_kv[slot], cache_c_cmp_kv[slot], cache_c_slc_kv[slot], page_table))
        st_p[kind].append(sp)
        st_s[kind].append(ss)
        xp, xs = _moe(xp, xs, mp[3:], ms[3:], g_moe, moe_router[l], moe_bias[l], moe_w_gate[l], moe_w_up[l],
                      moe_w_down[l], shared_w_gate[l].astype(BF16), shared_w_up[l].astype(BF16),
                      shared_w_down[l].astype(BF16))
    y_prompt = final_norm(xp, final_g, ROW_TILE)
    y_sample = final_norm(xs, final_g, ns).reshape(x_sample.shape)

    outs = [y_prompt, y_sample, jnp.stack(st_p[0]), jnp.stack(st_s[0])]
    for i in range(N_GROUPS_B):
        outs += [jnp.stack([s[i] for s in st_p[1]]), jnp.stack([s[i] for s in st_s[1]])]
    outs += [jnp.stack([s[0] for s in st_p[2]]), jnp.stack([s[0] for s in st_s[2]])]
    outs += [jnp.stack([s[1] for s in st_p[2]]), jnp.stack([s[1] for s in st_s[2]])]
    outs += [jnp.stack([s[2] for s in st_p[2]]), jnp.stack([s[2] for s in st_s[2]])]
    return tuple(outs)
```

```python
import functools

import jax
import jax.numpy as jnp
from jax import lax
from jax.experimental import pallas as pl
from jax.experimental.pallas import tpu as pltpu

D_MODEL = 1024
DEPTH = 4
PAST_LEN = 8192
PAGE_SIZE = 128
HEAD_DIM = 64
ROT_DIM = HEAD_DIM // 4
ROPE_THETA = 500000.0
ATTN_SCALE = HEAD_DIM ** -0.5
RMS_EPS = 1e-6

N_MIXERS = 3
LAYER_KIND = tuple(i % N_MIXERS for i in range(DEPTH))
LAYER_SLOT = tuple(LAYER_KIND[:i].count(LAYER_KIND[i]) for i in range(DEPTH))

N_Q_A = 16
N_KV_A = 4
WINDOW_A = 128
B_PATTERNS = ((128, 1), (512, 4), (2048, 16))
N_GROUPS_B = len(B_PATTERNS)
B_HEADS_PER_GROUP = 4
B_WIDTH = N_GROUPS_B * B_HEADS_PER_GROUP * HEAD_DIM
N_Q_C = 16
N_KV_C = 4
CMP_LEN = 32
CMP_STRIDE = 16
CMP_HID = 64
SEL_BLOCK = 64
N_SEL = 16
N_FORCED = 3
WINDOW_C = 512
C_KV_WIDTH = 2 * N_KV_C * HEAD_DIM
C_Q_WIDTH = N_Q_C * HEAD_DIM
C_MAIN_WIDTH = C_Q_WIDTH + 3 * C_KV_WIDTH
N_GATES_C = 3 * N_Q_C

N_EXPERTS = 64
TOP_K = 8
N_EXPERT_GROUPS = 8
TOPK_GROUPS = 4
D_EXPERT = 256
ROUTED_SCALE = 2.5

LANES_V7X = 128
VMEM_LIMIT_V7X = 56 * 1024 * 1024

ROW_TILE = 512
COL_GROUP = 512
MOE_TILE = 512
MOE_OUT_TILE = 256
CMP_PAGES_PER_STEP = 32
SEL_Q_TILE = 128
SEL_K_TILE = 256
NEG_BIG = -1e30

BF16 = jnp.bfloat16
F32 = jnp.float32


def _cparams(*sem):
    return pltpu.CompilerParams(dimension_semantics=sem, vmem_limit_bytes=VMEM_LIMIT_V7X)


def _rope_tables(pos, rows):
    half = ROT_DIM // 2
    inv_freq = ROPE_THETA ** (-jnp.arange(half, dtype=F32) / half)
    ang = pos.astype(F32)[:, None] * inv_freq
    cos, sin = jnp.cos(ang), jnp.sin(ang)
    t = pos.shape[0]
    z8 = jnp.zeros((t, half), F32)
    rest1 = jnp.ones((t, HEAD_DIM - ROT_DIM), F32)
    rest0 = jnp.zeros((t, HEAD_DIM - ROT_DIM), F32)
    reps = LANES_V7X // HEAD_DIM
    cos_t = jnp.tile(jnp.concatenate([cos, cos, rest1], axis=1), (1, reps))
    sin_a = jnp.tile(jnp.concatenate([-sin, z8, rest0], axis=1), (1, reps))
    sin_b = jnp.tile(jnp.concatenate([z8, sin, rest0], axis=1), (1, reps))
    return tuple(jnp.broadcast_to(a, (rows, LANES_V7X)) for a in (cos_t, sin_a, sin_b))


def _rope_chunk(blk, cos_t, sin_a, sin_b):
    return (blk * cos_t + pltpu.roll(blk, LANES_V7X - ROT_DIM // 2, 1) * sin_a
            + pltpu.roll(blk, ROT_DIM // 2, 1) * sin_b)


def _build_plan(n_cols, specs):
    plan = []
    for c0 in range(0, n_cols, COL_GROUP):
        width = min(COL_GROUP, n_cols - c0)
        segs = []
        for (s0, s1, mode, oi, d0) in specs:
            lo, hi = max(s0, c0), min(s1, c0 + width)
            if lo >= hi:
                continue
            step = LANES_V7X if mode == 'rope' else hi - lo
            for a in range(lo, hi, step):
                segs.append((a - c0, min(step, hi - a), mode, oi, d0 + a - s0))
        plan.append((c0, width, tuple(segs)))
    return tuple(plan)


def _modulated_norm(x, g, scale, shift):
    var = jnp.mean(x * x, axis=-1, keepdims=True)
    y = x * lax.rsqrt(var + RMS_EPS) * g
    return y * (1 + scale) + shift


def _norm_linear_kernel(x_ref, g_ref, sh_ref, sc_ref, w_ref, cos_ref, sa_ref, sb_ref, *refs, plan, n_out):
    outs, h_ref = refs[:n_out], refs[n_out]
    h_ref[...] = _modulated_norm(x_ref[0], g_ref[...], sc_ref[0], sh_ref[0]).astype(BF16)
    for (c0, width, segs) in plan:
        acc = jnp.dot(h_ref[...], w_ref[:, c0:c0 + width], preferred_element_type=F32)
        for (off, wd, mode, oi, dst) in segs:
            blk = acc[:, off:off + wd]
            if mode == 'rope':
                blk = _rope_chunk(blk, cos_ref[...], sa_ref[...], sb_ref[...])
            elif mode == 'sigmoid':
                blk = jax.nn.sigmoid(blk)
            outs[oi][0, :, dst:dst + wd] = blk


def norm_linear(x, g, shift, scale, w_bf, tabs, specs, out_widths, tm):
    b, t, k = x.shape
    n = w_bf.shape[1]
    per_row = shift.shape[1] != 1
    mod_spec = (pl.BlockSpec((1, tm, k), lambda bi, i: (bi, i, 0)) if per_row
                else pl.BlockSpec((1, 1, k), lambda bi, i: (bi, 0, 0)))
    tab_spec = pl.BlockSpec((tm, LANES_V7X), lambda bi, i: (i, 0))
    plan = _build_plan(n, specs)
    outs = pl.pallas_call(
        functools.partial(_norm_linear_kernel, plan=plan, n_out=len(out_widths)),
        grid=(b, t // tm),
        in_specs=[pl.BlockSpec((1, tm, k), lambda bi, i: (bi, i, 0)),
                  pl.BlockSpec((1, k), lambda bi, i: (0, 0)),
                  mod_spec, mod_spec,
                  pl.BlockSpec((k, n), lambda bi, i: (0, 0)),
                  tab_spec, tab_spec, tab_spec],
        out_specs=[pl.BlockSpec((1, tm, wd), lambda bi, i: (bi, i, 0)) for wd in out_widths],
        out_shape=[jax.ShapeDtypeStruct((b, t, wd), F32) for wd in out_widths],
        scratch_shapes=[pltpu.VMEM((tm, k), BF16)],
        compiler_params=_cparams("parallel", "arbitrary"),
        name="norm_linear",
    )(x, g.reshape(1, k), shift, scale, w_bf, *tabs)
    return outs


def _linear_out_kernel(*refs, mode, n_o):
    o_refs = refs[:n_o]
    w_ref, x_ref, gate_ref, out_ref = refs[n_o:]
    if mode == 'plain':
        o = o_refs[0][0]
    elif mode == 'sum3':
        o = (o_refs[0][0] + o_refs[1][0]) + o_refs[2][0]
    else:
        ng = n_o // 2
        lses = [r[0] for r in o_refs[ng:]]
        m = functools.reduce(jnp.maximum, lses)
        es = [jnp.exp(l - m) for l in lses]
        den = functools.reduce(lambda a, c: a + c, es)
        o = jnp.concatenate([(e / den) * r[0] for e, r in zip(es, o_refs[:ng])], axis=1)
    acc = jnp.dot(o.astype(BF16), w_ref[...], preferred_element_type=F32)
    out_ref[0] = x_ref[0] + gate_ref[0] * acc


def linear_out(o_list, w_bf, x, gate, mode, tm):
    b, t, d = x.shape
    per_row = gate.shape[1] != 1
    gate_spec = (pl.BlockSpec((1, tm, d), lambda bi, i: (bi, i, 0)) if per_row
                 else pl.BlockSpec((1, 1, d), lambda bi, i: (bi, 0, 0)))
    row_spec = lambda wd: pl.BlockSpec((1, tm, wd), lambda bi, i: (bi, i, 0))
    return pl.pallas_call(
        functools.partial(_linear_out_kernel, mode=mode, n_o=len(o_list)),
        grid=(b, t // tm),
        in_specs=[row_spec(o.shape[-1]) for o in o_list]
        + [pl.BlockSpec(w_bf.shape, lambda bi, i: (0, 0)), row_spec(d), gate_spec],
        out_specs=row_spec(d),
        out_shape=jax.ShapeDtypeStruct((b, t, d), F32),
        compiler_params=_cparams("parallel", "parallel"),
        name="linear_out",
    )(*o_list, w_bf, x, gate)


def _band_kernel(*refs, n_kv, group, tq, has_sink, has_gate, gate_col, want_lse):
    q_ref, kp_ref, kc_ref, vp_ref, vc_ref = refs[:5]
    pos = 5
    sink_ref = gate_ref = lse_ref = None
    if has_sink:
        sink_ref = refs[pos]
        pos += 1
    if has_gate:
        gate_ref = refs[pos]
        pos += 1
    o_ref = refs[pos]
    if want_lse:
        lse_ref = refs[pos + 1]
    no_prev = jnp.where(pl.program_id(2) == 0, 2 * tq, 0)
    rows = lax.broadcasted_iota(jnp.int32, (tq, 2 * tq), 0)
    cols = lax.broadcasted_iota(jnp.int32, (tq, 2 * tq), 1)
    mask = ((cols < tq) & (cols >= rows + no_prev)) | ((cols >= tq) & ((cols - tq) <= rows))
    for j in range(n_kv):
        ks = slice(j * HEAD_DIM, (j + 1) * HEAD_DIM)
        k = jnp.concatenate([kp_ref[0, :, ks], kc_ref[0, :, ks]], axis=0).astype(BF16)
        v = jnp.concatenate([vp_ref[0, :, ks], vc_ref[0, :, ks]], axis=0).astype(BF16)
        for g in range(group):
            h = j * group + g
            hs = slice(h * HEAD_DIM, (h + 1) * HEAD_DIM)
            q = (q_ref[0, :, hs] * ATTN_SCALE).astype(BF16)
            s = lax.dot_general(q, k, (((1,), (1,)), ((), ())), preferred_element_type=F32)
            s = jnp.where(mask, s, -jnp.inf)
            m = jnp.max(s, axis=-1, keepdims=True)
            if has_sink:
                m = jnp.maximum(m, sink_ref[h])
            e = jnp.exp(s - m)
            den = jnp.sum(e, axis=-1, keepdims=True)
            if has_sink:
                den = den + jnp.exp(sink_ref[h] - m)
            o = jnp.dot(e.astype(BF16), v, preferred_element_type=F32) / den
            if has_gate:
                o = gate_ref[0, :, gate_col + h:gate_col + h + 1] * o
            o_ref[0, :, hs] = o
            if want_lse:
                lse_ref[0, :, hs] = jnp.broadcast_to(m + jnp.log(den), (tq, HEAD_DIM))


def band_attention(qa, ka, va, *, dil, n_kv, group, tq, q_idx, k_idx, v_idx, sink=None, gate=None,
                   gate_col=0, want_lse=False):
    b, s = qa.shape[:2]
    l = s // dil
    qw, kw = n_kv * group * HEAD_DIM, n_kv * HEAD_DIM
    q_rs, k_rs, v_rs = qa.shape[2] // qw, ka.shape[2] // kw, va.shape[2] // kw
    q2, k2, v2 = (a.reshape(b, l, dil * a.shape[2]) for a in (qa, ka, va))
    prev = lambda i: jnp.maximum(i - 1, 0)
    in_specs = [pl.BlockSpec((1, tq, qw), lambda bi, r, i: (bi, i, r * q_rs + q_idx)),
                pl.BlockSpec((1, tq, kw), lambda bi, r, i: (bi, prev(i), r * k_rs + k_idx)),
                pl.BlockSpec((1, tq, kw), lambda bi, r, i: (bi, i, r * k_rs + k_idx)),
                pl.BlockSpec((1, tq, kw), lambda bi, r, i: (bi, prev(i), r * v_rs + v_idx)),
                pl.BlockSpec((1, tq, kw), lambda bi, r, i: (bi, i, r * v_rs + v_idx))]
    args = [q2, k2, k2, v2, v2]
    if sink is not None:
        in_specs.append(pl.BlockSpec(memory_space=pltpu.SMEM))
        args.append(sink.astype(F32))
    if gate is not None:
        in_specs.append(pl.BlockSpec((1, tq, gate.shape[2]), lambda bi, r, i: (bi, i, 0)))
        args.append(gate)
    o_spec = pl.BlockSpec((1, tq, qw), lambda bi, r, i: (bi, i, r))
    o_shape = jax.ShapeDtypeStruct((b, l, dil * qw), F32)
    res = pl.pallas_call(
        functools.partial(_band_kernel, n_kv=n_kv, group=group, tq=tq, has_sink=sink is not None,
                          has_gate=gate is not None, gate_col=gate_col, want_lse=want_lse),
        grid=(b, dil, l // tq),
        in_specs=in_specs,
        out_specs=[o_spec, o_spec] if want_lse else o_spec,
        out_shape=[o_shape, o_shape] if want_lse else o_shape,
        compiler_params=_cparams("parallel", "parallel", "arbitrary"),
        name="band_attention",
    )(*args)
    if want_lse:
        return res[0].reshape(b, s, qw), res[1].reshape(b, s, qw)
    return res.reshape(b, s, qw)


def _decode_kernel(*refs, n_kv, group, width, dil, has_sink, has_gate, want_lse):
    q_ref, knew_ref, vnew_ref, kcol_ref, vcol_ref, cache_ref = refs[:6]
    pos = 6
    sink_ref = gate_ref = lse_ref = None
    if has_sink:
        sink_ref = refs[pos]
        pos += 1
    if has_gate:
        gate_ref = refs[pos]
        pos += 1
    o_ref = refs[pos]
    pos += 1
    if want_lse:
        lse_ref = refs[pos]
        pos += 1
    newc_ref = refs[pos]
    lane_g = lax.broadcasted_iota(jnp.int32, (group, width), 1)
    lane_d = lax.broadcasted_iota(jnp.int32, (HEAD_DIM, width), 1)
    tap = ((width - lane_g) & (dil - 1)) == 0
    for j in range(n_kv):
        rows = slice(j * group, (j + 1) * group)
        k_t, v_t = cache_ref[0, 0, j], cache_ref[0, 1, j]
        q = q_ref[0, rows] * ATTN_SCALE
        sc = jnp.dot(q.astype(BF16), k_t.astype(BF16), preferred_element_type=F32)
        s_new = jnp.sum(q * knew_ref[0, j:j + 1], axis=-1, keepdims=True)
        sc = jnp.where(tap, sc, NEG_BIG)
        m = jnp.maximum(jnp.max(sc, axis=-1, keepdims=True), s_new)
        if has_sink:
            row_g = lax.broadcasted_iota(jnp.int32, (group, 1), 0)
            sink = jnp.zeros((group, 1), F32)
            for g in range(group):
                sink = jnp.where(row_g == g, sink_ref[j * group + g], sink)
            m = jnp.maximum(m, sink)
        p = jnp.where(tap, jnp.exp(sc - m), 0.0)
        p_new = jnp.exp(s_new - m)
        den = jnp.sum(p, axis=-1, keepdims=True) + p_new
        if has_sink:
            den = den + jnp.exp(sink - m)
        pv = lax.dot_general(p.astype(BF16), v_t.astype(BF16), (((1,), (1,)), ((), ())), preferred_element_type=F32)
        o = (pv + p_new * vnew_ref[0, j:j + 1]) / den
        if has_gate:
            o = gate_ref[0, rows] * o
        o_ref[0, rows] = o
        if want_lse:
            lse_ref[0, rows] = jnp.broadcast_to(m + jnp.log(den), (group, HEAD_DIM))
        newc_ref[0, 0, j] = jnp.where(lane_d == width - 1, kcol_ref[0, j], pltpu.roll(k_t, width - 1, 1))
        newc_ref[0, 1, j] = jnp.where(lane_d == width - 1, vcol_ref[0, j], pltpu.roll(v_t, width - 1, 1))


def decode_window_attention(q, k_new, v_new, cache, *, dil=1, sink=None, gate=None, want_lse=False):
    n, n_q, d = q.shape
    width, n_kv = cache.shape[1], cache.shape[3]
    group = n_q // n_kv
    cache_t = cache.transpose(0, 2, 3, 4, 1)
    row_spec = lambda a: pl.BlockSpec((1,) + a.shape[1:], lambda bi: (bi,) + (0,) * (a.ndim - 1))
    k_col, v_col = k_new[..., None], v_new[..., None]
    args = [q, k_new, v_new, k_col, v_col, cache_t]
    in_specs = [row_spec(a) for a in args]
    if sink is not None:
        in_specs.append(pl.BlockSpec(memory_space=pltpu.SMEM))
        args.append(sink.astype(F32))
    if gate is not None:
        in_specs.append(row_spec(gate))
        args.append(gate)
    o_shape = jax.ShapeDtypeStruct((n, n_q, d), F32)
    out_shape = [o_shape] + ([o_shape] if want_lse else []) + [jax.ShapeDtypeStruct(cache_t.shape, F32)]
    res = pl.pallas_call(
        functools.partial(_decode_kernel, n_kv=n_kv, group=group, width=width, dil=dil, has_sink=sink is not None,
                          has_gate=gate is not None, want_lse=want_lse),
        grid=(n,),
        in_specs=in_specs,
        out_specs=[row_spec(s) for s in out_shape],
        out_shape=out_shape,
        compiler_params=_cparams("parallel"),
        name="decode_window_attention",
    )(*args)
    return tuple(res[:-1]) + (res[-1].transpose(0, 4, 1, 2, 3),)


def _sel_kernel(q_ref, k_ref, v_ref, sel_ref, gate_ref, o_ref, *, tq, tk, n_kv, group, n_blocks, gate_col):
    i = pl.program_id(1)
    n_chunks = (i * tq + tq + tk - 1) // tk
    qpos = i * tq + lax.broadcasted_iota(jnp.int32, (tq, tk), 0)
    for j in range(n_kv):
        ks = slice(j * HEAD_DIM, (j + 1) * HEAD_DIM)
        q = jnp.concatenate(
            [(q_ref[0, :, (j * group + g) * HEAD_DIM:(j * group + g + 1) * HEAD_DIM] * ATTN_SCALE).astype(BF16)
             for g in range(group)], axis=0)
        sel = sel_ref[0, j]

        def body(c, carry):
            m, l, acc = carry
            k0 = pl.multiple_of(c * tk, tk)
            k = k_ref[0, pl.ds(k0, tk), ks].astype(BF16)
            v = v_ref[0, pl.ds(k0, tk), ks].astype(BF16)
            kpos = k0 + lax.broadcasted_iota(jnp.int32, (tq, tk), 1)
            blk_of_key = lax.shift_right_logical(k0 + lax.broadcasted_iota(jnp.int32, (n_blocks, tk), 1),
                                                 SEL_BLOCK.bit_length() - 1)
            expand = (blk_of_key == lax.broadcasted_iota(jnp.int32, (n_blocks, tk), 0)).astype(BF16)
            chosen = jnp.dot(sel, expand, preferred_element_type=F32) > 0.5
            mask = chosen & (kpos <= qpos)
            mask = jnp.concatenate([mask] * group, axis=0)
            s = lax.dot_general(q, k, (((1,), (1,)), ((), ())), preferred_element_type=F32)
            s = jnp.where(mask, s, NEG_BIG)
            m_new = jnp.maximum(m, jnp.max(s, axis=-1, keepdims=True))
            p = jnp.where(mask, jnp.exp(s - m_new), 0.0)
            alpha = jnp.exp(m - m_new)
            l = alpha * l + jnp.sum(p, axis=-1, keepdims=True)
            acc = alpha * acc + jnp.dot(p.astype(BF16), v, preferred_element_type=F32)
            return m_new, l, acc

        init = (jnp.full((group * tq, 1), NEG_BIG, F32), jnp.zeros((group * tq, 1), F32),
                jnp.zeros((group * tq, HEAD_DIM), F32))
        _, l, acc = lax.fori_loop(0, n_chunks, body, init)
        o = acc / l
        for g in range(group):
            h = j * group + g
            o_ref[0, :, h * HEAD_DIM:(h + 1) * HEAD_DIM] = (
                gate_ref[0, :, gate_col + h:gate_col + h + 1] * o[g * tq:(g + 1) * tq])


def selected_attention_prompt(q_rope, y_main, selmask, gates, *, k_idx, v_idx, gate_col):
    b, t, qw = q_rope.shape
    n_blocks = selmask.shape[-1]
    kw = N_KV_C * HEAD_DIM
    tq, tk = SEL_Q_TILE, SEL_K_TILE
    return pl.pallas_call(
        functools.partial(_sel_kernel, tq=tq, tk=tk, n_kv=N_KV_C, group=N_Q_C // N_KV_C,
                          n_blocks=n_blocks, gate_col=gate_col),
        grid=(b, t // tq),
        in_specs=[pl.BlockSpec((1, tq, qw), lambda bi, i: (bi, i, 0)),
                  pl.BlockSpec((1, t, kw), lambda bi, i: (bi, 0, k_idx)),
                  pl.BlockSpec((1, t, kw), lambda bi, i: (bi, 0, v_idx)),
                  pl.BlockSpec((1, N_KV_C, tq, n_blocks), lambda bi, i: (bi, 0, i, 0)),
                  pl.BlockSpec((1, tq, gates.shape[2]), lambda bi, i: (bi, i, 0))],
        out_specs=pl.BlockSpec((1, tq, qw), lambda bi, i: (bi, i, 0)),
        out_shape=jax.ShapeDtypeStruct((b, t, qw), F32),
        compiler_params=_cparams("parallel", "arbitrary"),
        name="selected_attention",
    )(q_rope, y_main, y_main, selmask, gates)


def _sel_sample_kernel(page_ref, q_ref, qbd_ref, knew_ref, vnew_ref, gate_ref, want_ref, *refs, n_sel):
    del page_ref
    n_q, grp, n_past = N_Q_C, N_Q_C // N_KV_C, n_sel - 1
    pages, o_ref = refs[:n_past * N_KV_C], refs[n_past * N_KV_C]

    def stacked(c):
        return jnp.concatenate(
            [jnp.concatenate([pages[s * N_KV_C + j][0, c, 0] for j in range(N_KV_C)], axis=0) for s in range(n_past)],
            axis=1).astype(BF16)

    width = n_past * PAGE_SIZE
    page_shift = PAGE_SIZE.bit_length() - 1
    sc = jnp.dot((qbd_ref[0] * ATTN_SCALE).astype(BF16), stacked(0), preferred_element_type=F32)
    lane = lax.broadcasted_iota(jnp.int32, (n_q, width), 1)
    key_half = lax.shift_right_logical(lane, SEL_BLOCK.bit_length() - 1) & (PAGE_SIZE // SEL_BLOCK - 1)
    slot_of_lane = lax.shift_right_logical(lax.broadcasted_iota(jnp.int32, (n_sel, width), 1), page_shift) + 1
    expand = (slot_of_lane == lax.broadcasted_iota(jnp.int32, (n_sel, width), 0)).astype(BF16)
    want = jnp.dot(want_ref[0].astype(BF16), expand, preferred_element_type=F32)
    mask = key_half.astype(F32) == want
    s_new = jnp.sum(q_ref[0] * ATTN_SCALE * knew_ref[0], axis=-1, keepdims=True)
    sc = jnp.where(mask, sc, NEG_BIG)
    m = jnp.maximum(jnp.max(sc, axis=-1, keepdims=True), s_new)
    p = jnp.where(mask, jnp.exp(sc - m), 0.0)
    p_new = jnp.exp(s_new - m)
    den = jnp.sum(p, axis=-1, keepdims=True) + p_new
    pv = lax.dot_general(p.astype(BF16), stacked(1), (((1,), (1,)), ((), ())), preferred_element_type=F32)
    head_of_row = lax.shift_right_logical(lax.broadcasted_iota(jnp.int32, (n_q, HEAD_DIM), 0), grp.bit_length() - 1)
    o = jnp.zeros((n_q, HEAD_DIM), F32)
    for j in range(N_KV_C):
        o = jnp.where(head_of_row == j, pv[:, j * HEAD_DIM:(j + 1) * HEAD_DIM], o)
    o_ref[0] = gate_ref[0] * ((o + p_new * vnew_ref[0]) / den)


def selected_attention_sample(q, k_new, v_new, gate, pool, page, want):
    n, n_q, d = q.shape
    grp = n_q // N_KV_C
    n_sel = want.shape[2]
    pool_t = pool.transpose(0, 2, 3, 4, 1)
    eye = jnp.repeat(jnp.eye(N_KV_C, dtype=q.dtype), grp, axis=0)
    q_bd = (q[:, :, None, :] * eye[None, :, :, None]).reshape(n, n_q, N_KV_C * d)
    row_spec = lambda w: pl.BlockSpec((1, n_q, w), lambda bi, pg: (bi, 0, 0))

    def page_spec(s, j):
        return pl.BlockSpec((1, 2, 1, d, PAGE_SIZE), lambda bi, pg: (pg[(bi * N_KV_C + j) * n_sel + s], 0, j, 0, 0))

    page_specs = [page_spec(s, j) for s in range(1, n_sel) for j in range(N_KV_C)]
    grid_spec = pltpu.PrefetchScalarGridSpec(
        num_scalar_prefetch=1,
        grid=(n,),
        in_specs=[row_spec(d), row_spec(N_KV_C * d), row_spec(d), row_spec(d), row_spec(1), row_spec(n_sel)] + page_specs,
        out_specs=row_spec(d),
    )
    return pl.pallas_call(
        functools.partial(_sel_sample_kernel, n_sel=n_sel),
        grid_spec=grid_spec,
        out_shape=jax.ShapeDtypeStruct((n, n_q, d), F32),
        compiler_params=_cparams("parallel"),
        name="selected_attention_sample",
    )(page, q, q_bd, k_new, v_new, gate, want, *([pool_t] * len(page_specs)))


def _half_proj_kernel(*refs, n_rows):
    n_pairs = N_KV_C // 2
    x_refs, w_ref, o_ref = refs[:2 * n_pairs], refs[2 * n_pairs], refs[2 * n_pairs + 1]
    out_w = 2 * 2 * CMP_HID
    for c in range(2):
        for p in range(n_pairs):
            x_ref = x_refs[c * n_pairs + p]
            acc = jnp.zeros((n_rows, out_w), F32)
            for sp in range(CMP_STRIDE // 2):
                lo = x_ref[0, pl.ds(2 * sp, n_rows, stride=CMP_STRIDE), :]
                hi = x_ref[0, pl.ds(2 * sp + 1, n_rows, stride=CMP_STRIDE), :]
                lhs = jnp.concatenate([lo, hi], axis=1).astype(BF16)
                acc = acc + jnp.dot(lhs, w_ref[c, sp], preferred_element_type=F32)
            o_ref[0, :, (c * n_pairs + p) * out_w:(c * n_pairs + p + 1) * out_w] = acc


def half_block_proj(rows_arr, col_idx, w_bd, tile_rows):
    b, l = rows_arr.shape[:2]
    n_half_tile = tile_rows // CMP_STRIDE
    out_w = 2 * 2 * N_KV_C * CMP_HID
    n_chunks = C_KV_WIDTH // LANES_V7X

    def chunk_spec(cp):
        return pl.BlockSpec((1, tile_rows, LANES_V7X), lambda bi, i: (bi, i, col_idx * n_chunks + cp))

    return pl.pallas_call(
        functools.partial(_half_proj_kernel, n_rows=n_half_tile),
        grid=(b, l // tile_rows),
        in_specs=[chunk_spec(cp) for cp in range(n_chunks)] + [pl.BlockSpec(w_bd.shape, lambda bi, i: (0, 0, 0, 0))],
        out_specs=pl.BlockSpec((1, n_half_tile, out_w), lambda bi, i: (bi, i, 0)),
        out_shape=jax.ShapeDtypeStruct((b, l // CMP_STRIDE, out_w), F32),
        compiler_params=_cparams("parallel", "parallel"),
        name="half_block_proj",
    )(*([rows_arr] * n_chunks), w_bd)


def _paged_half_proj_kernel(pt_ref, *refs, n_pages):
    del pt_ref
    n_pairs = N_KV_C // 2
    pages, w_ref, o_ref = refs[:n_pages], refs[n_pages], refs[n_pages + 1]
    rows = refs[n_pages + 2:]
    for j, page in enumerate(pages):
        for c in range(2):
            for p in range(n_pairs):
                x_t = page[0, c, 2 * p:2 * p + 2].reshape(2 * HEAD_DIM, PAGE_SIZE)
                rows[c * n_pairs + p][j * PAGE_SIZE:(j + 1) * PAGE_SIZE, :] = x_t.T
    n_rows = n_pages * PAGE_SIZE // CMP_STRIDE
    out_w = 2 * 2 * CMP_HID
    for c in range(2):
        for p in range(n_pairs):
            x_ref = rows[c * n_pairs + p]
            acc = jnp.zeros((n_rows, out_w), F32)
            for sp in range(CMP_STRIDE // 2):
                lo = x_ref[pl.ds(2 * sp, n_rows, stride=CMP_STRIDE), :]
                hi = x_ref[pl.ds(2 * sp + 1, n_rows, stride=CMP_STRIDE), :]
                lhs = jnp.concatenate([lo, hi], axis=1).astype(BF16)
                acc = acc + jnp.dot(lhs, w_ref[c, sp], preferred_element_type=F32)
            o_ref[0, :, (c * n_pairs + p) * out_w:(c * n_pairs + p + 1) * out_w] = acc


def paged_half_block_proj(pool, page_table, w_bd, n_pages):
    n, pages_per_req = page_table.shape
    pool_t = pool.transpose(0, 2, 3, 4, 1)
    steps = pages_per_req // n_pages
    n_half = n_pages * PAGE_SIZE // CMP_STRIDE
    out_w = 2 * 2 * N_KV_C * CMP_HID

    def page_spec(j):
        return pl.BlockSpec((1, 2, N_KV_C, HEAD_DIM, PAGE_SIZE),
                            lambda bi, i, pt: (pt[bi * pages_per_req + i * n_pages + j], 0, 0, 0, 0))

    grid_spec = pltpu.PrefetchScalarGridSpec(
        num_scalar_prefetch=1,
        grid=(n, steps),
        in_specs=[page_spec(j) for j in range(n_pages)] + [pl.BlockSpec(w_bd.shape, lambda bi, i, pt: (0, 0, 0, 0))],
        out_specs=pl.BlockSpec((1, n_half, out_w), lambda bi, i, pt: (bi, i, 0)),
        scratch_shapes=[pltpu.VMEM((n_pages * PAGE_SIZE, LANES_V7X), F32) for _ in range(C_KV_WIDTH // LANES_V7X)],
    )
    return pl.pallas_call(
        functools.partial(_paged_half_proj_kernel, n_pages=n_pages),
        grid_spec=grid_spec,
        out_shape=jax.ShapeDtypeStruct((n, pages_per_req * PAGE_SIZE // CMP_STRIDE, out_w), F32),
        compiler_params=_cparams("parallel", "arbitrary"),
        name="paged_half_block_proj",
    )(page_table.reshape(-1), *([pool_t] * n_pages), w_bd)


def _half_proj_weight(w1h):
    r = CMP_LEN // CMP_STRIDE
    eye = jnp.eye(2, dtype=F32)
    w = w1h.reshape(2, r, CMP_STRIDE // 2, 2, HEAD_DIM, CMP_HID)
    w = jnp.einsum('cjpldh,kq->cplkdqjh', w, eye)
    return w.reshape(2, CMP_STRIDE // 2, 4 * HEAD_DIM, 2 * r * CMP_HID).astype(BF16)


def _first_argmax(v, iota, size):
    m = jnp.max(v, axis=0, keepdims=True)
    return m, jnp.min(jnp.where(v == m, iota, size), axis=0, keepdims=True)


def _route_tokens(scores, bias):
    tm = scores.shape[1]
    per = N_EXPERTS // N_EXPERT_GROUPS
    biased = scores + bias
    iota_per = lax.broadcasted_iota(jnp.int32, (per, tm), 0)
    iota_grp = lax.broadcasted_iota(jnp.int32, (N_EXPERT_GROUPS, tm), 0)
    grp = jnp.zeros((N_EXPERT_GROUPS, tm), F32)
    slabs = [biased[g * per:(g + 1) * per] for g in range(N_EXPERT_GROUPS)]
    for g, v in enumerate(slabs):
        m1, i1 = _first_argmax(v, iota_per, per)
        m2 = jnp.max(jnp.where(iota_per == i1, -jnp.inf, v), axis=0, keepdims=True)
        grp = jnp.where(iota_grp == g, m1 + m2, grp)
    keep = jnp.zeros((N_EXPERT_GROUPS, tm), jnp.int32)
    for _ in range(TOPK_GROUPS):
        _, gi = _first_argmax(grp, iota_grp, N_EXPERT_GROUPS)
        keep = jnp.where(iota_grp == gi, 1, keep)
        grp = jnp.where(iota_grp == gi, -jnp.inf, grp)
    v = jnp.concatenate([jnp.where(keep[g:g + 1] > 0, slabs[g], -jnp.inf) for g in range(N_EXPERT_GROUPS)], axis=0)
    iota_e = lax.broadcasted_iota(jnp.int32, (N_EXPERTS, tm), 0)
    iota_k = lax.broadcasted_iota(jnp.int32, (TOP_K, tm), 0)
    eidx = jnp.zeros((TOP_K, tm), jnp.int32)
    ew = jnp.zeros((TOP_K, tm), F32)
    for k in range(TOP_K):
        _, ei = _first_argmax(v, iota_e, N_EXPERTS)
        hit = iota_e == ei
        eidx = jnp.where(iota_k == k, ei, eidx)
        ew = jnp.where(iota_k == k, jnp.sum(jnp.where(hit, scores, 0.0), axis=0, keepdims=True), ew)
        v = jnp.where(hit, -jnp.inf, v)
    return eidx, ew / jnp.sum(ew, axis=0, keepdims=True) * ROUTED_SCALE


def _moe_in_kernel(x_ref, g_ref, sh_ref, sc_ref, rwt_ref, rb_ref, sg_ref, su_ref, sd_ref,
                   h_ref, eidx_ref, ew_ref, shared_ref):
    h = _modulated_norm(x_ref[0], g_ref[...], sc_ref[0], sh_ref[0])
    logits_t = lax.dot_general(rwt_ref[...], h, (((1,), (1,)), ((), ())), preferred_element_type=F32,
                               precision=lax.Precision.HIGHEST)
    eidx, ew = _route_tokens(jax.nn.sigmoid(logits_t), rb_ref[...])
    eidx_ref[...] = eidx
    ew_ref[...] = ew
    h_ref[0] = h
    hb = h.astype(BF16)
    gate = jnp.dot(hb, sg_ref[...], preferred_element_type=F32)
    up = jnp.dot(hb, su_ref[...], preferred_element_type=F32)
    mid = (jax.nn.silu(gate) * up).astype(BF16)
    shared_ref[0] = jnp.dot(mid, sd_ref[...], preferred_element_type=F32)


def moe_in(x, g, shift, scale, router_w, router_b, sg_bf, su_bf, sd_bf, tm):
    b, t, k = x.shape
    per_row = shift.shape[1] != 1
    mod_spec = (pl.BlockSpec((1, tm, k), lambda bi, i: (bi, i, 0)) if per_row
                else pl.BlockSpec((1, 1, k), lambda bi, i: (bi, 0, 0)))
    row_spec = lambda wd: pl.BlockSpec((1, tm, wd), lambda bi, i: (bi, i, 0))
    full = lambda a: pl.BlockSpec(a.shape, lambda bi, i: (0,) * a.ndim)
    tiles = t // tm
    tok_spec = pl.BlockSpec((TOP_K, tm), lambda bi, i: (0, bi * tiles + i))
    rwt = router_w.T
    rb = router_b.astype(F32).reshape(N_EXPERTS, 1)
    return pl.pallas_call(
        _moe_in_kernel,
        grid=(b, tiles),
        in_specs=[row_spec(k), pl.BlockSpec((1, k), lambda bi, i: (0, 0)), mod_spec, mod_spec,
                  full(rwt), full(rb), full(sg_bf), full(su_bf), full(sd_bf)],
        out_specs=[row_spec(k), tok_spec, tok_spec, row_spec(k)],
        out_shape=[jax.ShapeDtypeStruct((b, t, k), F32), jax.ShapeDtypeStruct((TOP_K, b * t), jnp.int32),
                   jax.ShapeDtypeStruct((TOP_K, b * t), F32), jax.ShapeDtypeStruct((b, t, k), F32)],
        compiler_params=_cparams("parallel", "parallel"),
        name="moe_in",
    )(x, g.reshape(1, k), shift, scale, rwt, rb, sg_bf, su_bf, sd_bf)


def _gmm_kernel(blk_e_ref, n_used_ref, x_ref, wg_ref, wu_ref, wd_ref, o_ref, wg_s, wu_s, wd_s):
    blk = pl.program_id(0)
    prev_e = blk_e_ref[jnp.maximum(blk - 1, 0)]
    new_expert = (blk == 0) | (blk_e_ref[blk] != prev_e)

    @pl.when(new_expert)
    def _():
        wg_s[...] = wg_ref[0].astype(BF16)
        wu_s[...] = wu_ref[0].astype(BF16)
        wd_s[...] = wd_ref[0].astype(BF16)

    @pl.when(blk < n_used_ref[0])
    def _():
        x = x_ref[...].astype(BF16)
        gate = jnp.dot(x, wg_s[...], preferred_element_type=F32)
        up = jnp.dot(x, wu_s[...], preferred_element_type=F32)
        mid = (jax.nn.silu(gate) * up).astype(BF16)
        o_ref[...] = jnp.dot(mid, wd_s[...], preferred_element_type=F32)

    @pl.when(blk >= n_used_ref[0])
    def _():
        o_ref[...] = jnp.zeros_like(o_ref)


def grouped_experts(xs, blk_e, n_used, w_gate, w_up, w_down):
    cap, k = xs.shape
    tm = MOE_TILE
    n_blk = cap // tm
    de = w_gate.shape[-1]
    grid_spec = pltpu.PrefetchScalarGridSpec(
        num_scalar_prefetch=2,
        grid=(n_blk,),
        in_specs=[pl.BlockSpec((tm, k), lambda i, be, nu: (i, 0)),
                  pl.BlockSpec((1, k, de), lambda i, be, nu: (be[i], 0, 0)),
                  pl.BlockSpec((1, k, de), lambda i, be, nu: (be[i], 0, 0)),
                  pl.BlockSpec((1, de, k), lambda i, be, nu: (be[i], 0, 0))],
        out_specs=pl.BlockSpec((tm, k), lambda i, be, nu: (i, 0)),
        scratch_shapes=[pltpu.VMEM((k, de), BF16), pltpu.VMEM((k, de), BF16), pltpu.VMEM((de, k), BF16)],
    )
    return pl.pallas_call(
        _gmm_kernel,
        grid_spec=grid_spec,
        out_shape=jax.ShapeDtypeStruct((cap, k), F32),
        compiler_params=_cparams("arbitrary"),
        name="grouped_experts",
    )(blk_e, n_used, xs, w_gate, w_up, w_down)


def _moe_out_kernel(x_ref, gate_ref, picked_ref, ew_ref, shared_ref, o_ref):
    routed = picked_ref[0] * ew_ref[:, 0:1]
    for k in range(1, TOP_K):
        routed = routed + picked_ref[k] * ew_ref[:, k:k + 1]
    o_ref[0] = x_ref[0] + gate_ref[0] * (routed + shared_ref[0])


def moe_out(x, gate, picked, ew, shared, tm, row0):
    b, t, d = x.shape
    per_row = gate.shape[1] != 1
    gate_spec = (pl.BlockSpec((1, tm, d), lambda bi, i: (bi, i, 0)) if per_row
                 else pl.BlockSpec((1, 1, d), lambda bi, i: (bi, 0, 0)))
    row_spec = pl.BlockSpec((1, tm, d), lambda bi, i: (bi, i, 0))
    tiles, off = t // tm, row0 // tm
    return pl.pallas_call(
        _moe_out_kernel,
        grid=(b, tiles),
        in_specs=[row_spec, gate_spec,
                  pl.BlockSpec((TOP_K, tm, d), lambda bi, i: (0, off + bi * tiles + i, 0)),
                  pl.BlockSpec((tm, TOP_K), lambda bi, i: (off + bi * tiles + i, 0)),
                  row_spec],
        out_specs=row_spec,
        out_shape=jax.ShapeDtypeStruct((b, t, d), F32),
        compiler_params=_cparams("parallel", "parallel"),
        name="moe_out",
    )(x, gate, picked, ew, shared)


def _final_norm_kernel(x_ref, g_ref, o_ref):
    x = x_ref[0]
    var = jnp.mean(x * x, axis=-1, keepdims=True)
    o_ref[0] = x * lax.rsqrt(var + RMS_EPS) * g_ref[...]


def final_norm(x, g, tm):
    b, t, d = x.shape
    row_spec = pl.BlockSpec((1, tm, d), lambda bi, i: (bi, i, 0))
    return pl.pallas_call(
        _final_norm_kernel,
        grid=(b, t // tm),
        in_specs=[row_spec, pl.BlockSpec((1, d), lambda bi, i: (0, 0))],
        out_specs=row_spec,
        out_shape=jax.ShapeDtypeStruct((b, t, d), F32),
        compiler_params=_cparams("parallel", "parallel"),
        name="final_norm",
    )(x, g.reshape(1, d))


def _ada_kernel(c_ref, w_ref, b_ref, o_ref):
    a = jax.nn.silu(c_ref[...]).astype(BF16)
    o_ref[0] = jnp.dot(a, w_ref[0].astype(BF16), preferred_element_type=F32) + b_ref[0]


def ada_mods(c_all, ada_w, ada_b):
    n, d = c_all.shape
    depth, _, width = ada_w.shape
    tn = 1024
    return pl.pallas_call(
        _ada_kernel,
        grid=(depth, width // tn),
        in_specs=[pl.BlockSpec((n, d), lambda l, j: (0, 0)),
                  pl.BlockSpec((1, d, tn), lambda l, j: (l, 0, j)),
                  pl.BlockSpec((1, 1, tn), lambda l, j: (l, 0, j))],
        out_specs=pl.BlockSpec((1, n, tn), lambda l, j: (l, 0, j)),
        out_shape=jax.ShapeDtypeStruct((depth, n, width), F32),
        compiler_params=_cparams("parallel", "parallel"),
        name="ada_mods",
    )(c_all, ada_w, ada_b.reshape(depth, 1, width))


def _masked_softmax(s, mask, sink=None):
    s = jnp.where(mask, s, -jnp.inf)
    m = jnp.max(s, axis=-1, keepdims=True)
    if sink is not None:
        m = jnp.maximum(m, sink)
    m = jnp.where(jnp.isfinite(m), m, 0.0)
    e = jnp.exp(s - m)
    den = jnp.sum(e, axis=-1, keepdims=True)
    if sink is not None:
        den = den + jnp.exp(sink - m)
    p = e / jnp.maximum(den, 1e-30)
    return p, (m + jnp.log(den))[..., 0]


def _cmp_to_sel(p, n_slc):
    ratio = SEL_BLOCK // CMP_STRIDE
    left = CMP_LEN // CMP_STRIDE - 1
    pad = [(0, 0)] * (p.ndim - 1) + [(left, ratio * n_slc - p.shape[-1])]
    pp = jnp.pad(p, pad)
    out = pp[..., 0:ratio * n_slc:ratio]
    for o in range(1, ratio + left):
        out = out + pp[..., o:o + ratio * n_slc:ratio]
    return out


def _select_blocks(imp, pos, n_slc):
    k_top = N_SEL - N_FORCED
    cur = pos // SEL_BLOCK
    j = jnp.arange(n_slc)
    cand = (j[None, :] >= 1) & (j[None, :] <= cur[:, None] - 2)
    sc = jnp.where(cand, imp, -jnp.inf)
    if n_slc < k_top:
        sc = jnp.pad(sc, ((0, 0), (0, 0), (0, 0), (0, k_top - n_slc)), constant_values=-jnp.inf)
    vals, idx = lax.top_k(sc, k_top)
    forced = jnp.stack([cur, cur - 1, jnp.zeros_like(cur)], axis=-1)
    forced_ok = jnp.stack([cur >= 0, cur >= 1, cur >= 2], axis=-1)
    lead = imp.shape[:2]
    blocks = jnp.concatenate([jnp.broadcast_to(forced, lead + forced.shape), idx.astype(cur.dtype)], axis=-1)
    ok = jnp.concatenate([jnp.broadcast_to(forced_ok, lead + forced_ok.shape), vals > -jnp.inf], axis=-1)
    return jnp.clip(blocks, 0, n_slc - 1), ok


def _compress(hp, n_cmp, w1h, b1, w2, pe):
    b = hp.shape[0]
    r = CMP_LEN // CMP_STRIDE
    hp = hp.reshape(b, hp.shape[1], 2, N_KV_C, r, CMP_HID)
    pre = (jnp.einsum('cjsd,cjsdh->ch', pe.reshape(2, r, CMP_STRIDE, HEAD_DIM), w1h) + b1)[:, None, :]
    for j in range(r):
        pre = pre + hp[:, j:j + n_cmp, :, :, j]
    return jnp.einsum('bnckh,chd->bnckd', jax.nn.gelu(pre), w2)


def _compressed_attention(q_cmp, comp, pos):
    blk_end = jnp.arange(comp.shape[1]) * CMP_STRIDE + CMP_LEN - 1
    s = jnp.einsum('bqhgd,bnhd->bhgqn', q_cmp, comp[:, :, 0], preferred_element_type=F32) * ATTN_SCALE
    p_cmp, _ = _masked_softmax(s, blk_end[None, :] <= pos[:, None])
    o_cmp = jnp.einsum('bhgqn,bnhd->bqhgd', p_cmp.astype(comp.dtype), comp[:, :, 1])
    return o_cmp, p_cmp


A_SPECS = ((0, (N_Q_A + N_KV_A) * HEAD_DIM, 'rope', 0, 0),
           ((N_Q_A + N_KV_A) * HEAD_DIM, (N_Q_A + 2 * N_KV_A) * HEAD_DIM, 'raw', 0, (N_Q_A + N_KV_A) * HEAD_DIM))
B_SPECS = ((0, 2 * B_WIDTH, 'rope', 0, 0), (2 * B_WIDTH, 3 * B_WIDTH, 'raw', 0, 2 * B_WIDTH))
_C0 = C_Q_WIDTH + C_KV_WIDTH
_C1 = _C0 + C_KV_WIDTH
C_SPECS = ((0, _C0, 'raw', 0, 0),
           (_C0, _C0 + C_KV_WIDTH // 2, 'rope', 0, _C0), (_C0 + C_KV_WIDTH // 2, _C1, 'raw', 0, _C0 + C_KV_WIDTH // 2),
           (_C1, _C1 + C_KV_WIDTH // 2, 'rope', 0, _C1), (_C1 + C_KV_WIDTH // 2, C_MAIN_WIDTH, 'raw', 0, _C1 + C_KV_WIDTH // 2),
           (0, C_Q_WIDTH, 'rope', 1, 0),
           (C_MAIN_WIDTH, C_MAIN_WIDTH + N_GATES_C, 'sigmoid', 2, 0))


def _mixer_a(x, mods, g, w_qkv_bf, w_o_bf, sink, tabs, tm, past):
    shift, scale, gate = mods
    nq = N_Q_A * HEAD_DIM
    kvw = N_KV_A * HEAD_DIM
    (y,) = norm_linear(x, g, shift, scale, w_qkv_bf, tabs, A_SPECS, (nq + 2 * kvw,), tm)
    if past is None:
        o = band_attention(y, y, y, dil=1, n_kv=N_KV_A, group=N_Q_A // N_KV_A, tq=WINDOW_A,
                           q_idx=0, k_idx=nq // kvw, v_idx=nq // kvw + 1, sink=sink)
        b, t = y.shape[:2]
        new = y[:, -min(WINDOW_A, t):, nq:].reshape(b, -1, 2, N_KV_A, HEAD_DIM)
    else:
        n = y.shape[1]
        o, new = decode_window_attention(y[0, :, :nq].reshape(n, N_Q_A, HEAD_DIM),
                                         y[0, :, nq:nq + kvw].reshape(n, N_KV_A, HEAD_DIM),
                                         y[0, :, nq + kvw:].reshape(n, N_KV_A, HEAD_DIM), past, sink=sink)
        o = o.reshape(1, n, nq)
    return linear_out([o], w_o_bf, x, gate, 'plain', tm), new


def _mixer_b(x, mods, g, w_qkv_bf, w_o_bf, tabs, tm, past):
    shift, scale, gate = mods
    hpg = B_HEADS_PER_GROUP
    gw = hpg * HEAD_DIM
    (y,) = norm_linear(x, g, shift, scale, w_qkv_bf, tabs, B_SPECS, (3 * B_WIDTH,), tm)
    outs, lses, news = [], [], []
    if past is None:
        b, t = y.shape[:2]
        for gi, (win, dil) in enumerate(B_PATTERNS):
            o, lse = band_attention(y, y, y, dil=dil, n_kv=hpg, group=1, tq=win // dil,
                                    q_idx=gi, k_idx=N_GROUPS_B + gi, v_idx=2 * N_GROUPS_B + gi, want_lse=True)
            outs.append(o)
            lses.append(lse)
            k = y[:, -min(win, t):, B_WIDTH + gi * gw:B_WIDTH + (gi + 1) * gw]
            v = y[:, -min(win, t):, 2 * B_WIDTH + gi * gw:2 * B_WIDTH + (gi + 1) * gw]
            news.append(jnp.stack([k, v], axis=2).reshape(b, -1, 2, hpg, HEAD_DIM))
    else:
        n = y.shape[1]
        for gi, (win, dil) in enumerate(B_PATTERNS):
            part = lambda c: y[0, :, c * B_WIDTH + gi * gw:c * B_WIDTH + (gi + 1) * gw].reshape(n, hpg, HEAD_DIM)
            o, lse, new = decode_window_attention(part(0), part(1), part(2), past[gi], dil=dil, want_lse=True)
            outs.append(o.reshape(1, n, gw))
            lses.append(lse.reshape(1, n, gw))
            news.append(new)
    return linear_out(outs + lses, w_o_bf, x, gate, 'mix3', tm), tuple(news)


def _mixer_c(x, mods, g, w_in_bf, w_o_bf, cmp_w1, cmp_b1, cmp_w2, cmp_pe, tabs, pos, tm, past):
    shift, scale, gate = mods
    grp = N_Q_C // N_KV_C
    kvw = N_KV_C * HEAD_DIM
    y, q_rope, gates = norm_linear(x, g, shift, scale, w_in_bf, tabs, C_SPECS,
                                   (C_MAIN_WIDTH, C_Q_WIDTH, N_GATES_C), tm)
    w1h = cmp_w1.reshape(2, CMP_LEN // CMP_STRIDE, CMP_STRIDE, HEAD_DIM, CMP_HID)
    w_bd = _half_proj_weight(w1h)
    cmp_idx, slc_idx, win_idx = C_Q_WIDTH // C_KV_WIDTH, _C0 // C_KV_WIDTH, _C1 // C_KV_WIDTH
    if past is None:
        b, t = y.shape[:2]
        hp = half_block_proj(y, cmp_idx, w_bd, t)
        n_cmp = (t - CMP_LEN) // CMP_STRIDE + 1
        comp = _compress(hp, n_cmp, w1h, cmp_b1, cmp_w2, cmp_pe)
        q_cmp = y[..., :C_Q_WIDTH].reshape(b, t, N_KV_C, grp, HEAD_DIM)
        o_cmp, p_cmp = _compressed_attention(q_cmp, comp, pos)
        gview = gates.reshape(b, t, 3, N_KV_C, grp, 1)
        o_cmp = (gview[:, :, 0] * o_cmp.astype(F32)).reshape(b, t, C_Q_WIDTH)
        n_slc = -(-t // SEL_BLOCK)
        blocks, ok = _select_blocks(_cmp_to_sel(p_cmp.sum(axis=2), n_slc), pos, n_slc)
        onehot = (blocks[..., None] == jnp.arange(n_slc)) & ok[..., None]
        selmask = jnp.any(onehot, axis=-2).astype(BF16)
        o_slc = selected_attention_prompt(q_rope, y, selmask, gates, k_idx=2 * slc_idx, v_idx=2 * slc_idx + 1,
                                          gate_col=N_Q_C)
        o_win = band_attention(q_rope, y, y, dil=1, n_kv=N_KV_C, group=grp, tq=WINDOW_C, q_idx=0,
                               k_idx=2 * win_idx, v_idx=2 * win_idx + 1, gate=gates, gate_col=2 * N_Q_C)
        new_win = y[:, -min(WINDOW_C, t):, _C1:].reshape(b, -1, 2, N_KV_C, HEAD_DIM)
        cmp_kv = y[..., C_Q_WIDTH:_C0].reshape(b, t, 2, N_KV_C, HEAD_DIM)
        slc_kv = y[..., _C0:_C1].reshape(b, t, 2, N_KV_C, HEAD_DIM)
    else:
        win_buf, cmp_pool, slc_pool, page_table = past
        n = y.shape[1]
        t = 1
        cmp_kv = y[0, :, C_Q_WIDTH:_C0].reshape(n, t, 2, N_KV_C, HEAD_DIM)
        slc_kv = y[0, :, _C0:_C1].reshape(n, t, 2, N_KV_C, HEAD_DIM)
        win_kv = y[0, :, _C1:].reshape(n, t, 2, N_KV_C, HEAD_DIM)
        q_cmp = y[0, :, :C_Q_WIDTH].reshape(n, t, N_KV_C, grp, HEAD_DIM)
        gview = gates[0].reshape(n, t, 3, N_KV_C, grp, 1)
        hp = paged_half_block_proj(cmp_pool, page_table, w_bd, CMP_PAGES_PER_STEP)
        length = PAST_LEN + t
        n_cmp = (length - CMP_LEN) // CMP_STRIDE + 1
        comp = _compress(hp, n_cmp, w1h, cmp_b1, cmp_w2, cmp_pe)
        o_cmp, p_cmp = _compressed_attention(q_cmp, comp, pos)
        n_slc = -(-length // SEL_BLOCK)
        blocks, ok = _select_blocks(_cmp_to_sel(p_cmp.sum(axis=2), n_slc), pos, n_slc)
        sub = PAGE_SIZE // SEL_BLOCK
        pb = jnp.minimum(blocks[:, :, 0], PAST_LEN // SEL_BLOCK - 1)
        page = jnp.take_along_axis(jnp.broadcast_to(page_table[:, None], (n, N_KV_C, page_table.shape[1])),
                                   pb // sub, axis=2)
        want = jnp.where(ok[:, :, 0], pb % sub, -1).astype(F32)
        o_slc = selected_attention_sample(
            q_rope[0].reshape(n, N_Q_C, HEAD_DIM), jnp.repeat(slc_kv[:, 0, 0], grp, axis=1),
            jnp.repeat(slc_kv[:, 0, 1], grp, axis=1), gates[0, :, N_Q_C:2 * N_Q_C].reshape(n, N_Q_C, 1),
            slc_pool, page.reshape(-1).astype(jnp.int32), jnp.repeat(want, grp, axis=1))
        o_slc = o_slc.reshape(1, n, C_Q_WIDTH)
        o_win, new_win = decode_window_attention(
            q_rope[0].reshape(n, N_Q_C, HEAD_DIM), win_kv[:, 0, 0], win_kv[:, 0, 1], win_buf,
            gate=gates[0, :, 2 * N_Q_C:].reshape(n, N_Q_C, 1))
        o_win = o_win.reshape(1, n, C_Q_WIDTH)
        o_cmp = (gview[:, :, 0] * o_cmp.astype(F32)).reshape(1, n, C_Q_WIDTH)
    return linear_out([o_cmp, o_slc, o_win], w_o_bf, x, gate, 'sum3', tm), (new_win, cmp_kv, slc_kv)


def _dispatch_plan(eidx, n_blk):
    n = eidx.shape[0]
    tm = MOE_TILE
    chunk = LANES_V7X
    onehot = (eidx[:, :, None] == jnp.arange(N_EXPERTS)).astype(jnp.int32)
    sel = onehot.sum(axis=1).astype(F32).reshape(n // chunk, chunk, N_EXPERTS)
    tril = jnp.tril(jnp.ones((chunk, chunk), F32))
    within = jnp.einsum('ij,cjk->cik', tril, sel)
    chunk_tot = within[:, -1, :]
    before = jnp.cumsum(chunk_tot, axis=0) - chunk_tot
    rank = (within - sel + before[:, None, :]).reshape(n, N_EXPERTS).astype(jnp.int32)
    counts = jnp.sum(chunk_tot, axis=0).astype(jnp.int32)
    padded = (counts + tm - 1) // tm * tm
    ends = jnp.cumsum(padded)
    dest = jnp.sum(onehot * (ends - padded + rank)[:, None, :], axis=-1)
    tok = jnp.broadcast_to(jnp.arange(n, dtype=jnp.int32)[:, None], dest.shape)
    row_tok = jnp.zeros((n_blk * tm,), jnp.int32).at[dest.reshape(-1)].set(tok.reshape(-1))
    blk_start = jnp.arange(n_blk, dtype=jnp.int32) * tm
    blk_e = jnp.minimum(jnp.sum(ends[None, :] <= blk_start[:, None], axis=1), N_EXPERTS - 1).astype(jnp.int32)
    n_used = (ends[-1] // tm).astype(jnp.int32).reshape(1)
    return dest, row_tok, blk_e, n_used


def _moe(xp, xs, mods_p, mods_s, g, router_w, router_b, w_gate, w_up, w_down, sg_bf, su_bf, sd_bf):
    hp, ep, wp, shp = moe_in(xp, g, mods_p[0], mods_p[1], router_w, router_b, sg_bf, su_bf, sd_bf, ROW_TILE)
    hs, es, ws, shs = moe_in(xs, g, mods_s[0], mods_s[1], router_w, router_b, sg_bf, su_bf, sd_bf, xs.shape[1])
    d = xp.shape[-1]
    n_p = xp.shape[0] * xp.shape[1]
    h_all = jnp.concatenate([hp.reshape(-1, d), hs.reshape(-1, d)], axis=0)
    eidx = jnp.concatenate([ep, es], axis=1).T
    ew = jnp.concatenate([wp, ws], axis=1).T
    n = h_all.shape[0]
    n_blk = (n * TOP_K + N_EXPERTS * (MOE_TILE - 1)) // MOE_TILE + 1
    dest, row_tok, blk_e, n_used = _dispatch_plan(eidx, n_blk)
    ys = grouped_experts(h_all[row_tok], blk_e, n_used, w_gate, w_up, w_down)
    picked = ys[dest.T.reshape(-1)].reshape(TOP_K, n, d)
    xp = moe_out(xp, mods_p[2], picked, ew, shp, MOE_OUT_TILE, 0)
    xs = moe_out(xs, mods_s[2], picked, ew, shs, xs.shape[1], n_p)
    return xp, xs


def kernel(x_prompt, x_sample, c_prompt, c_sample, cache_a_kv, cache_b_kv_w128, cache_b_kv_w512, cache_b_kv_w2048, cache_c_win_kv, cache_c_cmp_kv, cache_c_slc_kv, page_table, norm_g, final_g, ada_w, ada_b, a_w_qkv, a_w_o, a_sink, b_w_qkv, b_w_o, c_w_in, c_w_o, c_cmp_w1, c_cmp_b1, c_cmp_w2, c_cmp_pe, moe_router, moe_bias, moe_w_gate, moe_w_up, moe_w_down, shared_w_gate, shared_w_up, shared_w_down):
    bp, seq, d = x_prompt.shape
    ns = x_sample.shape[0]
    b_caches = (cache_b_kv_w128, cache_b_kv_w512, cache_b_kv_w2048)
    pos_p = jnp.arange(seq, dtype=jnp.int32)
    pos_s = PAST_LEN + jnp.arange(x_sample.shape[1], dtype=jnp.int32)
    tabs_p = _rope_tables(pos_p, seq)
    tabs_s = _rope_tables(pos_s, ns)

    mods = ada_mods(jnp.concatenate([c_prompt, c_sample], axis=0), ada_w, ada_b)
    mods_p = mods[:, :bp].reshape(DEPTH, bp, 6, 1, d)
    mods_s = mods[:, bp:].reshape(DEPTH, 1, ns, 6, d)

    xp = x_prompt
    xs = x_sample.reshape(1, ns, d)
    st_p = {0: [], 1: [], 2: []}
    st_s = {0: [], 1: [], 2: []}
    for l in range(DEPTH):
        kind, slot = LAYER_KIND[l], LAYER_SLOT[l]
        mp = [mods_p[l, :, i] for i in range(6)]
        ms = [mods_s[l, :, :, i] for i in range(6)]
        g_mix, g_moe = norm_g[l, 0], norm_g[l, 1]
        if kind == 0:
            w_in, w_o = a_w_qkv[slot].astype(BF16), a_w_o[slot].astype(BF16)
            xp, sp = _mixer_a(xp, mp[:3], g_mix, w_in, w_o, a_sink[slot], tabs_p, ROW_TILE, None)
            xs, ss = _mixer_a(xs, ms[:3], g_mix, w_in, w_o, a_sink[slot], tabs_s, ns, cache_a_kv[slot])
        elif kind == 1:
            w_in, w_o = b_w_qkv[slot].astype(BF16), b_w_o[slot].astype(BF16)
            xp, sp = _mixer_b(xp, mp[:3], g_mix, w_in, w_o, tabs_p, ROW_TILE, None)
            xs, ss = _mixer_b(xs, ms[:3], g_mix, w_in, w_o, tabs_s, ns, tuple(buf[slot] for buf in b_caches))
        else:
            w_in, w_o = c_w_in[slot].astype(BF16), c_w_o[slot].astype(BF16)
            cargs = (c_cmp_w1[slot], c_cmp_b1[slot], c_cmp_w2[slot], c_cmp_pe[slot])
            xp, sp = _mixer_c(xp, mp[:3], g_mix, w_in, w_o, *cargs, tabs_p, pos_p, ROW_TILE, None)
            xs, ss = _mixer_c(xs, ms[:3], g_mix, w_in, w_o, *cargs, tabs_s, pos_s, ns,
                              (cache_c_win_kv[slot], cache_c_cmp_kv[slot], cache_c_slc_kv[slot], page_table))
        st_p[kind].append(sp)
        st_s[kind].append(ss)
        xp, xs = _moe(xp, xs, mp[3:], ms[3:], g_moe, moe_router[l], moe_bias[l], moe_w_gate[l], moe_w_up[l],
                      moe_w_down[l], shared_w_gate[l].astype(BF16), shared_w_up[l].astype(BF16),
                      shared_w_down[l].astype(BF16))
    y_prompt = final_norm(xp, final_g, ROW_TILE)
    y_sample = final_norm(xs, final_g, ns).reshape(x_sample.shape)

    outs = [y_prompt, y_sample, jnp.stack(st_p[0]), jnp.stack(st_s[0])]
    for i in range(N_GROUPS_B):
        outs += [jnp.stack([s[i] for s in st_p[1]]), jnp.stack([s[i] for s in st_s[1]])]
    outs += [jnp.stack([s[0] for s in st_p[2]]), jnp.stack([s[0] for s in st_s[2]])]
    outs += [jnp.stack([s[1] for s in st_p[2]]), jnp.stack([s[1] for s in st_s[2]])]
    outs += [jnp.stack([s[2] for s in st_p[2]]), jnp.stack([s[2] for s in st_s[2]])]
    return tuple(outs)
```

```python
import functools

import jax
import jax.numpy as jnp
from jax import lax
from jax.experimental import pallas as pl
from jax.experimental.pallas import tpu as pltpu

D_MODEL = 1024
DEPTH = 4
PAST_LEN = 8192
PAGE_SIZE = 128
HEAD_DIM = 64
ROT_DIM = HEAD_DIM // 4
ROPE_THETA = 500000.0
ATTN_SCALE = HEAD_DIM ** -0.5
RMS_EPS = 1e-6

N_MIXERS = 3
LAYER_KIND = tuple(i % N_MIXERS for i in range(DEPTH))
LAYER_SLOT = tuple(LAYER_KIND[:i].count(LAYER_KIND[i]) for i in range(DEPTH))

N_Q_A = 16
N_KV_A = 4
WINDOW_A = 128
B_PATTERNS = ((128, 1), (512, 4), (2048, 16))
N_GROUPS_B = len(B_PATTERNS)
B_HEADS_PER_GROUP = 4
B_WIDTH = N_GROUPS_B * B_HEADS_PER_GROUP * HEAD_DIM
N_Q_C = 16
N_KV_C = 4
CMP_LEN = 32
CMP_STRIDE = 16
CMP_HID = 64
SEL_BLOCK = 64
N_SEL = 16
N_FORCED = 3
WINDOW_C = 512
C_KV_WIDTH = 2 * N_KV_C * HEAD_DIM
C_Q_WIDTH = N_Q_C * HEAD_DIM
C_MAIN_WIDTH = C_Q_WIDTH + 3 * C_KV_WIDTH
N_GATES_C = 3 * N_Q_C

N_EXPERTS = 64
TOP_K = 8
N_EXPERT_GROUPS = 8
TOPK_GROUPS = 4
D_EXPERT = 256
ROUTED_SCALE = 2.5

LANES_V7X = 128
VMEM_LIMIT_V7X = 56 * 1024 * 1024

ROW_TILE = 512
COL_GROUP = 512
MOE_TILE = 512
MOE_OUT_TILE = 256
CMP_PAGES_PER_STEP = 32
SEL_Q_TILE = 128
SEL_K_TILE = 256
NEG_BIG = -1e30

BF16 = jnp.bfloat16
F32 = jnp.float32


def _cparams(*sem):
    return pltpu.CompilerParams(dimension_semantics=sem, vmem_limit_bytes=VMEM_LIMIT_V7X)


def _rope_tables(pos, rows):
    half = ROT_DIM // 2
    inv_freq = ROPE_THETA ** (-jnp.arange(half, dtype=F32) / half)
    ang = pos.astype(F32)[:, None] * inv_freq
    cos, sin = jnp.cos(ang), jnp.sin(ang)
    t = pos.shape[0]
    z8 = jnp.zeros((t, half), F32)
    rest1 = jnp.ones((t, HEAD_DIM - ROT_DIM), F32)
    rest0 = jnp.zeros((t, HEAD_DIM - ROT_DIM), F32)
    reps = LANES_V7X // HEAD_DIM
    cos_t = jnp.tile(jnp.concatenate([cos, cos, rest1], axis=1), (1, reps))
    sin_a = jnp.tile(jnp.concatenate([-sin, z8, rest0], axis=1), (1, reps))
    sin_b = jnp.tile(jnp.concatenate([z8, sin, rest0], axis=1), (1, reps))
    return tuple(jnp.broadcast_to(a, (rows, LANES_V7X)) for a in (cos_t, sin_a, sin_b))


def _rope_chunk(blk, cos_t, sin_a, sin_b):
    return (blk * cos_t + pltpu.roll(blk, LANES_V7X - ROT_DIM // 2, 1) * sin_a
            + pltpu.roll(blk, ROT_DIM // 2, 1) * sin_b)


def _build_plan(n_cols, specs):
    plan = []
    for c0 in range(0, n_cols, COL_GROUP):
        width = min(COL_GROUP, n_cols - c0)
        segs = []
        for (s0, s1, mode, oi, d0) in specs:
            lo, hi = max(s0, c0), min(s1, c0 + width)
            if lo >= hi:
                continue
            step = LANES_V7X if mode == 'rope' else hi - lo
            for a in range(lo, hi, step):
                segs.append((a - c0, min(step, hi - a), mode, oi, d0 + a - s0))
        plan.append((c0, width, tuple(segs)))
    return tuple(plan)


def _modulated_norm(x, g, scale, shift):
    var = jnp.mean(x * x, axis=-1, keepdims=True)
    y = x * lax.rsqrt(var + RMS_EPS) * g
    return y * (1 + scale) + shift


def _norm_linear_kernel(x_ref, g_ref, sh_ref, sc_ref, w_ref, cos_ref, sa_ref, sb_ref, *refs, plan, layouts):
    outs, h_ref = refs[:len(layouts)], refs[len(layouts)]
    h_ref[...] = _modulated_norm(x_ref[0], g_ref[...], sc_ref[0], sh_ref[0]).astype(BF16)
    for (c0, width, segs) in plan:
        acc = jnp.dot(h_ref[...], w_ref[:, c0:c0 + width], preferred_element_type=F32)
        for (off, wd, mode, oi, dst) in segs:
            blk = acc[:, off:off + wd]
            if mode == 'rope':
                blk = _rope_chunk(blk, cos_ref[...], sa_ref[...], sb_ref[...])
            elif mode == 'sigmoid':
                blk = jax.nn.sigmoid(blk)
            if layouts[oi] == 'rows':
                outs[oi][0, :, dst:dst + wd] = blk
            else:
                outs[oi][0, dst:dst + wd, :] = blk.T


def norm_linear(x, g, shift, scale, w_bf, tabs, specs, out_descs, tm):
    b, t, k = x.shape
    n = w_bf.shape[1]
    per_row = shift.shape[1] != 1
    mod_spec = (pl.BlockSpec((1, tm, k), lambda bi, i: (bi, i, 0)) if per_row
                else pl.BlockSpec((1, 1, k), lambda bi, i: (bi, 0, 0)))
    tab_spec = pl.BlockSpec((tm, LANES_V7X), lambda bi, i: (i, 0))
    plan = _build_plan(n, specs)
    out_specs, out_shape = [], []
    for wd, layout in out_descs:
        if layout == 'rows':
            out_specs.append(pl.BlockSpec((1, tm, wd), lambda bi, i: (bi, i, 0)))
            out_shape.append(jax.ShapeDtypeStruct((b, t, wd), F32))
        elif layout == 'cols':
            out_specs.append(pl.BlockSpec((1, wd, tm), lambda bi, i: (bi, 0, i)))
            out_shape.append(jax.ShapeDtypeStruct((b, wd, t), F32))
        else:
            out_specs.append(pl.BlockSpec((1, wd, tm), lambda bi, i: (bi, 0, 0)))
            out_shape.append(jax.ShapeDtypeStruct((b, wd, tm), F32))
    outs = pl.pallas_call(
        functools.partial(_norm_linear_kernel, plan=plan, layouts=tuple(l for _, l in out_descs)),
        grid=(b, t // tm),
        in_specs=[pl.BlockSpec((1, tm, k), lambda bi, i: (bi, i, 0)),
                  pl.BlockSpec((1, k), lambda bi, i: (0, 0)),
                  mod_spec, mod_spec,
                  pl.BlockSpec((k, n), lambda bi, i: (0, 0)),
                  tab_spec, tab_spec, tab_spec],
        out_specs=out_specs,
        out_shape=out_shape,
        scratch_shapes=[pltpu.VMEM((tm, k), BF16)],
        compiler_params=_cparams("parallel", "arbitrary"),
        name="norm_linear",
    )(x, g.reshape(1, k), shift, scale, w_bf, *tabs)
    return outs


def _linear_out_kernel(*refs, mode, n_o):
    o_refs = refs[:n_o]
    w_ref, x_ref, gate_ref, out_ref = refs[n_o:]
    if mode == 'plain':
        o = o_refs[0][0]
    elif mode == 'sum3':
        o = (o_refs[0][0] + o_refs[1][0]) + o_refs[2][0]
    else:
        ng = n_o // 2
        lses = [r[0] for r in o_refs[ng:]]
        m = functools.reduce(jnp.maximum, lses)
        es = [jnp.exp(l - m) for l in lses]
        den = functools.reduce(lambda a, c: a + c, es)
        o = jnp.concatenate([(e / den) * r[0] for e, r in zip(es, o_refs[:ng])], axis=1)
    acc = jnp.dot(o.astype(BF16), w_ref[...], preferred_element_type=F32)
    out_ref[0] = x_ref[0] + gate_ref[0] * acc


def linear_out(o_list, w_bf, x, gate, mode, tm):
    b, t, d = x.shape
    per_row = gate.shape[1] != 1
    gate_spec = (pl.BlockSpec((1, tm, d), lambda bi, i: (bi, i, 0)) if per_row
                 else pl.BlockSpec((1, 1, d), lambda bi, i: (bi, 0, 0)))
    row_spec = lambda wd: pl.BlockSpec((1, tm, wd), lambda bi, i: (bi, i, 0))
    return pl.pallas_call(
        functools.partial(_linear_out_kernel, mode=mode, n_o=len(o_list)),
        grid=(b, t // tm),
        in_specs=[row_spec(o.shape[-1]) for o in o_list]
        + [pl.BlockSpec(w_bf.shape, lambda bi, i: (0, 0)), row_spec(d), gate_spec],
        out_specs=row_spec(d),
        out_shape=jax.ShapeDtypeStruct((b, t, d), F32),
        compiler_params=_cparams("parallel", "parallel"),
        name="linear_out",
    )(*o_list, w_bf, x, gate)


def _band_kernel(*refs, n_kv, group, tq, has_sink, has_gate, gate_col, want_lse):
    q_ref, kp_ref, kc_ref, vp_ref, vc_ref = refs[:5]
    pos = 5
    sink_ref = gate_ref = lse_ref = None
    if has_sink:
        sink_ref = refs[pos]
        pos += 1
    if has_gate:
        gate_ref = refs[pos]
        pos += 1
    o_ref = refs[pos]
    if want_lse:
        lse_ref = refs[pos + 1]
    no_prev = jnp.where(pl.program_id(2) == 0, 2 * tq, 0)
    rows = lax.broadcasted_iota(jnp.int32, (tq, 2 * tq), 0)
    cols = lax.broadcasted_iota(jnp.int32, (tq, 2 * tq), 1)
    mask = ((cols < tq) & (cols >= rows + no_prev)) | ((cols >= tq) & ((cols - tq) <= rows))
    for j in range(n_kv):
        ks = slice(j * HEAD_DIM, (j + 1) * HEAD_DIM)
        k = jnp.concatenate([kp_ref[0, :, ks], kc_ref[0, :, ks]], axis=0).astype(BF16)
        v = jnp.concatenate([vp_ref[0, :, ks], vc_ref[0, :, ks]], axis=0).astype(BF16)
        for g in range(group):
            h = j * group + g
            hs = slice(h * HEAD_DIM, (h + 1) * HEAD_DIM)
            q = (q_ref[0, :, hs] * ATTN_SCALE).astype(BF16)
            s = lax.dot_general(q, k, (((1,), (1,)), ((), ())), preferred_element_type=F32)
            s = jnp.where(mask, s, -jnp.inf)
            m = jnp.max(s, axis=-1, keepdims=True)
            if has_sink:
                m = jnp.maximum(m, sink_ref[h])
            e = jnp.exp(s - m)
            den = jnp.sum(e, axis=-1, keepdims=True)
            if has_sink:
                den = den + jnp.exp(sink_ref[h] - m)
            o = jnp.dot(e.astype(BF16), v, preferred_element_type=F32) / den
            if has_gate:
                o = gate_ref[0, :, gate_col + h:gate_col + h + 1] * o
            o_ref[0, :, hs] = o
            if want_lse:
                lse_ref[0, :, hs] = jnp.broadcast_to(m + jnp.log(den), (tq, HEAD_DIM))


def band_attention(qa, ka, va, *, dil, n_kv, group, tq, q_idx, k_idx, v_idx, sink=None, gate=None,
                   gate_col=0, want_lse=False):
    b, s = qa.shape[:2]
    l = s // dil
    qw, kw = n_kv * group * HEAD_DIM, n_kv * HEAD_DIM
    q_rs, k_rs, v_rs = qa.shape[2] // qw, ka.shape[2] // kw, va.shape[2] // kw
    q2, k2, v2 = (a.reshape(b, l, dil * a.shape[2]) for a in (qa, ka, va))
    prev = lambda i: jnp.maximum(i - 1, 0)
    in_specs = [pl.BlockSpec((1, tq, qw), lambda bi, r, i: (bi, i, r * q_rs + q_idx)),
                pl.BlockSpec((1, tq, kw), lambda bi, r, i: (bi, prev(i), r * k_rs + k_idx)),
                pl.BlockSpec((1, tq, kw), lambda bi, r, i: (bi, i, r * k_rs + k_idx)),
                pl.BlockSpec((1, tq, kw), lambda bi, r, i: (bi, prev(i), r * v_rs + v_idx)),
                pl.BlockSpec((1, tq, kw), lambda bi, r, i: (bi, i, r * v_rs + v_idx))]
    args = [q2, k2, k2, v2, v2]
    if sink is not None:
        in_specs.append(pl.BlockSpec(memory_space=pltpu.SMEM))
        args.append(sink.astype(F32))
    if gate is not None:
        in_specs.append(pl.BlockSpec((1, tq, gate.shape[2]), lambda bi, r, i: (bi, i, 0)))
        args.append(gate)
    o_spec = pl.BlockSpec((1, tq, qw), lambda bi, r, i: (bi, i, r))
    o_shape = jax.ShapeDtypeStruct((b, l, dil * qw), F32)
    res = pl.pallas_call(
        functools.partial(_band_kernel, n_kv=n_kv, group=group, tq=tq, has_sink=sink is not None,
                          has_gate=gate is not None, gate_col=gate_col, want_lse=want_lse),
        grid=(b, dil, l // tq),
        in_specs=in_specs,
        out_specs=[o_spec, o_spec] if want_lse else o_spec,
        out_shape=[o_shape, o_shape] if want_lse else o_shape,
        compiler_params=_cparams("parallel", "parallel", "arbitrary"),
        name="band_attention",
    )(*args)
    if want_lse:
        return res[0].reshape(b, s, qw), res[1].reshape(b, s, qw)
    return res.reshape(b, s, qw)


def _decode_kernel(*refs, n_kv, group, width, dil, has_sink, has_gate, want_lse):
    q_ref, knew_ref, vnew_ref, kcol_ref, vcol_ref, cache_ref = refs[:6]
    pos = 6
    sink_ref = gate_ref = lse_ref = None
    if has_sink:
        sink_ref = refs[pos]
        pos += 1
    if has_gate:
        gate_ref = refs[pos]
        pos += 1
    o_ref = refs[pos]
    pos += 1
    if want_lse:
        lse_ref = refs[pos]
        pos += 1
    newc_ref = refs[pos]
    lane_g = lax.broadcasted_iota(jnp.int32, (group, width), 1)
    lane_d = lax.broadcasted_iota(jnp.int32, (HEAD_DIM, width), 1)
    tap = ((width - lane_g) & (dil - 1)) == 0
    for j in range(n_kv):
        rows = slice(j * group, (j + 1) * group)
        k_t, v_t = cache_ref[0, 0, j], cache_ref[0, 1, j]
        q = q_ref[0, rows] * ATTN_SCALE
        sc = jnp.dot(q.astype(BF16), k_t.astype(BF16), preferred_element_type=F32)
        s_new = jnp.sum(q * knew_ref[0, j:j + 1], axis=-1, keepdims=True)
        sc = jnp.where(tap, sc, NEG_BIG)
        m = jnp.maximum(jnp.max(sc, axis=-1, keepdims=True), s_new)
        if has_sink:
            row_g = lax.broadcasted_iota(jnp.int32, (group, 1), 0)
            sink = jnp.zeros((group, 1), F32)
            for g in range(group):
                sink = jnp.where(row_g == g, sink_ref[j * group + g], sink)
            m = jnp.maximum(m, sink)
        p = jnp.where(tap, jnp.exp(sc - m), 0.0)
        p_new = jnp.exp(s_new - m)
        den = jnp.sum(p, axis=-1, keepdims=True) + p_new
        if has_sink:
            den = den + jnp.exp(sink - m)
        pv = lax.dot_general(p.astype(BF16), v_t.astype(BF16), (((1,), (1,)), ((), ())), preferred_element_type=F32)
        o = (pv + p_new * vnew_ref[0, j:j + 1]) / den
        if has_gate:
            o = gate_ref[0, rows] * o
        o_ref[0, rows] = o
        if want_lse:
            lse_ref[0, rows] = jnp.broadcast_to(m + jnp.log(den), (group, HEAD_DIM))
        newc_ref[0, 0, j] = jnp.where(lane_d == width - 1, kcol_ref[0, j], pltpu.roll(k_t, width - 1, 1))
        newc_ref[0, 1, j] = jnp.where(lane_d == width - 1, vcol_ref[0, j], pltpu.roll(v_t, width - 1, 1))


def decode_window_attention(q, k_new, v_new, cache, *, dil=1, sink=None, gate=None, want_lse=False):
    n, n_q, d = q.shape
    width, n_kv = cache.shape[1], cache.shape[3]
    group = n_q // n_kv
    cache_t = cache.transpose(0, 2, 3, 4, 1)
    row_spec = lambda a: pl.BlockSpec((1,) + a.shape[1:], lambda bi: (bi,) + (0,) * (a.ndim - 1))
    k_col, v_col = k_new[..., None], v_new[..., None]
    args = [q, k_new, v_new, k_col, v_col, cache_t]
    in_specs = [row_spec(a) for a in args]
    if sink is not None:
        in_specs.append(pl.BlockSpec(memory_space=pltpu.SMEM))
        args.append(sink.astype(F32))
    if gate is not None:
        in_specs.append(row_spec(gate))
        args.append(gate)
    o_shape = jax.ShapeDtypeStruct((n, n_q, d), F32)
    out_shape = [o_shape] + ([o_shape] if want_lse else []) + [jax.ShapeDtypeStruct(cache_t.shape, F32)]
    res = pl.pallas_call(
        functools.partial(_decode_kernel, n_kv=n_kv, group=group, width=width, dil=dil, has_sink=sink is not None,
                          has_gate=gate is not None, want_lse=want_lse),
        grid=(n,),
        in_specs=in_specs,
        out_specs=[row_spec(s) for s in out_shape],
        out_shape=out_shape,
        compiler_params=_cparams("parallel"),
        name="decode_window_attention",
    )(*args)
    return tuple(res[:-1]) + (res[-1].transpose(0, 4, 1, 2, 3),)


CMP_PAD = LANES_V7X


def _cmp_select_kernel(q_ref, ck_ref, cv_ref, fold_ref, gate_ref, o_ref, sel_ref, *, tq, n_cmp, n_kv, group):
    i = pl.program_id(1)
    wide = (CMP_PAD, group * tq)
    n_idx = lax.broadcasted_iota(jnp.int32, wide, 0)
    qpos_w = i * tq + (lax.broadcasted_iota(jnp.int32, wide, 1) & (tq - 1))
    valid = (n_idx * CMP_STRIDE + (CMP_LEN - 1) <= qpos_w) & (n_idx < n_cmp)
    n_slc_pad = fold_ref.shape[0]
    blk = lax.broadcasted_iota(jnp.int32, (n_slc_pad, tq), 0)
    cur = lax.shift_right_logical(i * tq + lax.broadcasted_iota(jnp.int32, (n_slc_pad, tq), 1),
                                  SEL_BLOCK.bit_length() - 1)
    cand = (blk >= 1) & (blk <= cur - 2)
    forced = (blk == cur) | ((blk == cur - 1) & (cur >= 1)) | ((blk == 0) & (cur >= 2))
    for j in range(n_kv):
        q = jnp.concatenate(
            [(q_ref[0, :, (j * group + g) * HEAD_DIM:(j * group + g + 1) * HEAD_DIM] * ATTN_SCALE).astype(BF16)
             for g in range(group)], axis=0)
        s_t = lax.dot_general(ck_ref[0, j].astype(BF16), q, (((1,), (1,)), ((), ())), preferred_element_type=F32)
        s_t = jnp.where(valid, s_t, -jnp.inf)
        m = jnp.max(s_t, axis=0, keepdims=True)
        m = jnp.where(m == -jnp.inf, 0.0, m)
        e = jnp.exp(s_t - m)
        p_t = e / jnp.maximum(jnp.sum(e, axis=0, keepdims=True), 1e-30)
        cv = cv_ref[0, j].astype(BF16)
        p_sum = jnp.zeros((CMP_PAD, tq), F32)
        for g in range(group):
            h = j * group + g
            p_g = p_t[:, g * tq:(g + 1) * tq]
            p_sum = p_sum + p_g
            o = jnp.dot(p_g.T.astype(BF16), cv, preferred_element_type=F32)
            o_ref[0, :, h * HEAD_DIM:(h + 1) * HEAD_DIM] = gate_ref[0, :, h:h + 1] * o
        imp = jnp.dot(fold_ref[...], p_sum, preferred_element_type=F32, precision=lax.Precision.HIGHEST)
        sc = jnp.where(cand, imp, -jnp.inf)
        chosen = forced
        for _ in range(N_SEL - N_FORCED):
            best, bi = _first_argmax(sc, blk, n_slc_pad)
            hit = blk == bi
            chosen = chosen | (hit & (best > -jnp.inf))
            sc = jnp.where(hit, -jnp.inf, sc)
        sel_t = jnp.concatenate([jnp.where(chosen, 1.0, 0.0), jnp.zeros((CMP_PAD - n_slc_pad, tq), F32)], axis=0)
        sel_ref[0, j] = sel_t.T.astype(BF16)


def cmp_select_prompt(y_main, comp, gates, tq=LANES_V7X):
    b, t = y_main.shape[:2]
    n_cmp = comp.shape[1]
    n_slc = -(-t // SEL_BLOCK)
    n_slc_pad = -(-n_slc // 8) * 8
    grp = N_Q_C // N_KV_C
    ckv = jnp.pad(comp, ((0, 0), (0, CMP_PAD - n_cmp), (0, 0), (0, 0), (0, 0))).transpose(2, 0, 3, 1, 4)
    ratio, left = SEL_BLOCK // CMP_STRIDE, CMP_LEN // CMP_STRIDE - 1
    mi, bi = jnp.arange(CMP_PAD)[None, :], jnp.arange(n_slc_pad)[:, None]
    fold = ((mi >= ratio * bi - left) & (mi <= ratio * bi + ratio - 1) & (mi < n_cmp) & (bi < n_slc)).astype(F32)
    return pl.pallas_call(
        functools.partial(_cmp_select_kernel, tq=tq, n_cmp=n_cmp, n_kv=N_KV_C, group=grp),
        grid=(b, t // tq),
        in_specs=[pl.BlockSpec((1, tq, C_Q_WIDTH), lambda bi_, i: (bi_, i, 0)),
                  pl.BlockSpec((1, N_KV_C, CMP_PAD, HEAD_DIM), lambda bi_, i: (bi_, 0, 0, 0)),
                  pl.BlockSpec((1, N_KV_C, CMP_PAD, HEAD_DIM), lambda bi_, i: (bi_, 0, 0, 0)),
                  pl.BlockSpec(fold.shape, lambda bi_, i: (0, 0)),
                  pl.BlockSpec((1, tq, gates.shape[2]), lambda bi_, i: (bi_, i, 0))],
        out_specs=[pl.BlockSpec((1, tq, C_Q_WIDTH), lambda bi_, i: (bi_, i, 0)),
                   pl.BlockSpec((1, N_KV_C, tq, CMP_PAD), lambda bi_, i: (bi_, 0, i, 0))],
        out_shape=[jax.ShapeDtypeStruct((b, t, C_Q_WIDTH), F32),
                   jax.ShapeDtypeStruct((b, N_KV_C, t, CMP_PAD), BF16)],
        compiler_params=_cparams("parallel", "parallel"),
        name="cmp_select",
    )(y_main, ckv[0], ckv[1], fold, gates)


def _sel_kernel(q_ref, k_ref, v_ref, sel_ref, gate_ref, o_ref, *, tq, tk, n_kv, group, n_blocks, gate_col):
    i = pl.program_id(1)
    n_chunks = (i * tq + tq + tk - 1) // tk
    qpos = i * tq + lax.broadcasted_iota(jnp.int32, (tq, tk), 0)
    qs = [jnp.concatenate(
        [(q_ref[0, :, (j * group + g) * HEAD_DIM:(j * group + g + 1) * HEAD_DIM] * ATTN_SCALE).astype(BF16)
         for g in range(group)], axis=0) for j in range(n_kv)]

    def body(c, carry):
        k0 = pl.multiple_of(c * tk, tk)
        causal = (k0 + lax.broadcasted_iota(jnp.int32, (tq, tk), 1)) <= qpos
        blk_of_key = lax.shift_right_logical(k0 + lax.broadcasted_iota(jnp.int32, (n_blocks, tk), 1),
                                             SEL_BLOCK.bit_length() - 1)
        expand = (blk_of_key == lax.broadcasted_iota(jnp.int32, (n_blocks, tk), 0)).astype(BF16)
        out = []
        for j in range(n_kv):
            m, l, acc = carry[j]
            ks = slice(j * HEAD_DIM, (j + 1) * HEAD_DIM)
            k = k_ref[0, pl.ds(k0, tk), ks].astype(BF16)
            v = v_ref[0, pl.ds(k0, tk), ks].astype(BF16)
            mask = (jnp.dot(sel_ref[0, j], expand, preferred_element_type=F32) > 0.5) & causal
            s = lax.dot_general(qs[j], k, (((1,), (1,)), ((), ())), preferred_element_type=F32)
            s = jnp.where(mask[None], s.reshape(group, tq, tk), -jnp.inf).reshape(group * tq, tk)
            m_new = jnp.maximum(m, jnp.max(s, axis=-1, keepdims=True))
            p = jnp.exp(s - m_new)
            alpha = jnp.exp(m - m_new)
            l = alpha * l + jnp.sum(p, axis=-1, keepdims=True)
            acc = alpha * acc + jnp.dot(p.astype(BF16), v, preferred_element_type=F32)
            out.append((m_new, l, acc))
        return tuple(out)

    init = tuple((jnp.full((group * tq, 1), NEG_BIG, F32), jnp.zeros((group * tq, 1), F32),
                  jnp.zeros((group * tq, HEAD_DIM), F32)) for _ in range(n_kv))
    final = lax.fori_loop(0, n_chunks, body, init)
    for j in range(n_kv):
        _, l, acc = final[j]
        o = acc / l
        for g in range(group):
            h = j * group + g
            o_ref[0, :, h * HEAD_DIM:(h + 1) * HEAD_DIM] = (
                gate_ref[0, :, gate_col + h:gate_col + h + 1] * o[g * tq:(g + 1) * tq])


def selected_attention_prompt(q_rope, y_main, selmask, gates, *, k_idx, v_idx, gate_col):
    b, t, qw = q_rope.shape
    n_blocks = selmask.shape[-1]
    kw = N_KV_C * HEAD_DIM
    tq, tk = SEL_Q_TILE, SEL_K_TILE
    return pl.pallas_call(
        functools.partial(_sel_kernel, tq=tq, tk=tk, n_kv=N_KV_C, group=N_Q_C // N_KV_C,
                          n_blocks=n_blocks, gate_col=gate_col),
        grid=(b, t // tq),
        in_specs=[pl.BlockSpec((1, tq, qw), lambda bi, i: (bi, i, 0)),
                  pl.BlockSpec((1, t, kw), lambda bi, i: (bi, 0, k_idx)),
                  pl.BlockSpec((1, t, kw), lambda bi, i: (bi, 0, v_idx)),
                  pl.BlockSpec((1, N_KV_C, tq, n_blocks), lambda bi, i: (bi, 0, i, 0)),
                  pl.BlockSpec((1, tq, gates.shape[2]), lambda bi, i: (bi, i, 0))],
        out_specs=pl.BlockSpec((1, tq, qw), lambda bi, i: (bi, i, 0)),
        out_shape=jax.ShapeDtypeStruct((b, t, qw), F32),
        compiler_params=_cparams("parallel", "arbitrary"),
        name="selected_attention",
    )(q_rope, y_main, y_main, selmask, gates)


def _sel_sample_kernel(page_ref, q_ref, qbd_ref, knew_ref, vnew_ref, gate_ref, want_ref, *refs, n_sel):
    del page_ref
    n_q, grp, n_past = N_Q_C, N_Q_C // N_KV_C, n_sel - 1
    pages, o_ref = refs[:n_past * N_KV_C], refs[n_past * N_KV_C]

    def stacked(c):
        return jnp.concatenate(
            [jnp.concatenate([pages[s * N_KV_C + j][0, c, 0] for j in range(N_KV_C)], axis=0) for s in range(n_past)],
            axis=1).astype(BF16)

    width = n_past * PAGE_SIZE
    page_shift = PAGE_SIZE.bit_length() - 1
    sc = jnp.dot((qbd_ref[0] * ATTN_SCALE).astype(BF16), stacked(0), preferred_element_type=F32)
    lane = lax.broadcasted_iota(jnp.int32, (n_q, width), 1)
    key_half = lax.shift_right_logical(lane, SEL_BLOCK.bit_length() - 1) & (PAGE_SIZE // SEL_BLOCK - 1)
    slot_of_lane = lax.shift_right_logical(lax.broadcasted_iota(jnp.int32, (n_sel, width), 1), page_shift) + 1
    expand = (slot_of_lane == lax.broadcasted_iota(jnp.int32, (n_sel, width), 0)).astype(BF16)
    want = jnp.dot(want_ref[0].astype(BF16), expand, preferred_element_type=F32)
    mask = key_half.astype(F32) == want
    s_new = jnp.sum(q_ref[0] * ATTN_SCALE * knew_ref[0], axis=-1, keepdims=True)
    sc = jnp.where(mask, sc, NEG_BIG)
    m = jnp.maximum(jnp.max(sc, axis=-1, keepdims=True), s_new)
    p = jnp.where(mask, jnp.exp(sc - m), 0.0)
    p_new = jnp.exp(s_new - m)
    den = jnp.sum(p, axis=-1, keepdims=True) + p_new
    pv = lax.dot_general(p.astype(BF16), stacked(1), (((1,), (1,)), ((), ())), preferred_element_type=F32)
    head_of_row = lax.shift_right_logical(lax.broadcasted_iota(jnp.int32, (n_q, HEAD_DIM), 0), grp.bit_length() - 1)
    o = jnp.zeros((n_q, HEAD_DIM), F32)
    for j in range(N_KV_C):
        o = jnp.where(head_of_row == j, pv[:, j * HEAD_DIM:(j + 1) * HEAD_DIM], o)
    o_ref[0] = gate_ref[0] * ((o + p_new * vnew_ref[0]) / den)


def selected_attention_sample(q, k_new, v_new, gate, pool, page, want):
    n, n_q, d = q.shape
    grp = n_q // N_KV_C
    n_sel = want.shape[2]
    pool_t = pool.transpose(0, 2, 3, 4, 1)
    eye = jnp.repeat(jnp.eye(N_KV_C, dtype=q.dtype), grp, axis=0)
    q_bd = (q[:, :, None, :] * eye[None, :, :, None]).reshape(n, n_q, N_KV_C * d)
    row_spec = lambda w: pl.BlockSpec((1, n_q, w), lambda bi, pg: (bi, 0, 0))

    def page_spec(s, j):
        return pl.BlockSpec((1, 2, 1, d, PAGE_SIZE), lambda bi, pg: (pg[(bi * N_KV_C + j) * n_sel + s], 0, j, 0, 0))

    page_specs = [page_spec(s, j) for s in range(1, n_sel) for j in range(N_KV_C)]
    grid_spec = pltpu.PrefetchScalarGridSpec(
        num_scalar_prefetch=1,
        grid=(n,),
        in_specs=[row_spec(d), row_spec(N_KV_C * d), row_spec(d), row_spec(d), row_spec(1), row_spec(n_sel)] + page_specs,
        out_specs=row_spec(d),
    )
    return pl.pallas_call(
        functools.partial(_sel_sample_kernel, n_sel=n_sel),
        grid_spec=grid_spec,
        out_shape=jax.ShapeDtypeStruct((n, n_q, d), F32),
        compiler_params=_cparams("parallel"),
        name="selected_attention_sample",
    )(page, q, q_bd, k_new, v_new, gate, want, *([pool_t] * len(page_specs)))


def _half_proj_kernel(*refs, n_rows):
    n_pairs = N_KV_C // 2
    x_refs, w_ref, o_ref = refs[:2 * n_pairs], refs[2 * n_pairs], refs[2 * n_pairs + 1]
    out_w = 2 * 2 * CMP_HID
    for c in range(2):
        for p in range(n_pairs):
            x_ref = x_refs[c * n_pairs + p]
            acc = jnp.zeros((n_rows, out_w), F32)
            for sp in range(CMP_STRIDE // 2):
                lo = x_ref[0, pl.ds(2 * sp, n_rows, stride=CMP_STRIDE), :]
                hi = x_ref[0, pl.ds(2 * sp + 1, n_rows, stride=CMP_STRIDE), :]
                lhs = jnp.concatenate([lo, hi], axis=1).astype(BF16)
                acc = acc + jnp.dot(lhs, w_ref[c, sp], preferred_element_type=F32)
            o_ref[0, :, (c * n_pairs + p) * out_w:(c * n_pairs + p + 1) * out_w] = acc


def half_block_proj(rows_arr, col_idx, w_bd, tile_rows):
    b, l = rows_arr.shape[:2]
    n_half_tile = tile_rows // CMP_STRIDE
    out_w = 2 * 2 * N_KV_C * CMP_HID
    n_chunks = C_KV_WIDTH // LANES_V7X

    def chunk_spec(cp):
        return pl.BlockSpec((1, tile_rows, LANES_V7X), lambda bi, i: (bi, i, col_idx * n_chunks + cp))

    return pl.pallas_call(
        functools.partial(_half_proj_kernel, n_rows=n_half_tile),
        grid=(b, l // tile_rows),
        in_specs=[chunk_spec(cp) for cp in range(n_chunks)] + [pl.BlockSpec(w_bd.shape, lambda bi, i: (0, 0, 0, 0))],
        out_specs=pl.BlockSpec((1, n_half_tile, out_w), lambda bi, i: (bi, i, 0)),
        out_shape=jax.ShapeDtypeStruct((b, l // CMP_STRIDE, out_w), F32),
        compiler_params=_cparams("parallel", "parallel"),
        name="half_block_proj",
    )(*([rows_arr] * n_chunks), w_bd)


def _paged_half_proj_kernel(pt_ref, *refs, n_pages):
    del pt_ref
    n_pairs = N_KV_C // 2
    pages, w_ref, o_ref = refs[:n_pages], refs[n_pages], refs[n_pages + 1]
    rows = refs[n_pages + 2:]
    for j, page in enumerate(pages):
        for c in range(2):
            for p in range(n_pairs):
                x_t = page[0, c, 2 * p:2 * p + 2].reshape(2 * HEAD_DIM, PAGE_SIZE)
                rows[c * n_pairs + p][j * PAGE_SIZE:(j + 1) * PAGE_SIZE, :] = x_t.T
    n_rows = n_pages * PAGE_SIZE // CMP_STRIDE
    out_w = 2 * 2 * CMP_HID
    for c in range(2):
        for p in range(n_pairs):
            x_ref = rows[c * n_pairs + p]
            acc = jnp.zeros((n_rows, out_w), F32)
            for sp in range(CMP_STRIDE // 2):
                lo = x_ref[pl.ds(2 * sp, n_rows, stride=CMP_STRIDE), :]
                hi = x_ref[pl.ds(2 * sp + 1, n_rows, stride=CMP_STRIDE), :]
                lhs = jnp.concatenate([lo, hi], axis=1).astype(BF16)
                acc = acc + jnp.dot(lhs, w_ref[c, sp], preferred_element_type=F32)
            o_ref[0, :, (c * n_pairs + p) * out_w:(c * n_pairs + p + 1) * out_w] = acc


def paged_half_block_proj(pool, page_table, w_bd, n_pages):
    n, pages_per_req = page_table.shape
    pool_t = pool.transpose(0, 2, 3, 4, 1)
    steps = pages_per_req // n_pages
    n_half = n_pages * PAGE_SIZE // CMP_STRIDE
    out_w = 2 * 2 * N_KV_C * CMP_HID

    def page_spec(j):
        return pl.BlockSpec((1, 2, N_KV_C, HEAD_DIM, PAGE_SIZE),
                            lambda bi, i, pt: (pt[bi * pages_per_req + i * n_pages + j], 0, 0, 0, 0))

    grid_spec = pltpu.PrefetchScalarGridSpec(
        num_scalar_prefetch=1,
        grid=(n, steps),
        in_specs=[page_spec(j) for j in range(n_pages)] + [pl.BlockSpec(w_bd.shape, lambda bi, i, pt: (0, 0, 0, 0))],
        out_specs=pl.BlockSpec((1, n_half, out_w), lambda bi, i, pt: (bi, i, 0)),
        scratch_shapes=[pltpu.VMEM((n_pages * PAGE_SIZE, LANES_V7X), F32) for _ in range(C_KV_WIDTH // LANES_V7X)],
    )
    return pl.pallas_call(
        functools.partial(_paged_half_proj_kernel, n_pages=n_pages),
        grid_spec=grid_spec,
        out_shape=jax.ShapeDtypeStruct((n, pages_per_req * PAGE_SIZE // CMP_STRIDE, out_w), F32),
        compiler_params=_cparams("parallel", "arbitrary"),
        name="paged_half_block_proj",
    )(page_table.reshape(-1), *([pool_t] * n_pages), w_bd)


def _half_proj_weight(w1h):
    r = CMP_LEN // CMP_STRIDE
    eye = jnp.eye(2, dtype=F32)
    w = w1h.reshape(2, r, CMP_STRIDE // 2, 2, HEAD_DIM, CMP_HID)
    w = jnp.einsum('cjpldh,kq->cplkdqjh', w, eye)
    return w.reshape(2, CMP_STRIDE // 2, 4 * HEAD_DIM, 2 * r * CMP_HID).astype(BF16)


def _first_argmax(v, iota, size):
    m = jnp.max(v, axis=0, keepdims=True)
    return m, jnp.min(jnp.where(v == m, iota, size), axis=0, keepdims=True)


def _route_tokens(scores, bias):
    tm = scores.shape[1]
    per = N_EXPERTS // N_EXPERT_GROUPS
    biased = scores + bias
    iota_per = lax.broadcasted_iota(jnp.int32, (per, tm), 0)
    iota_grp = lax.broadcasted_iota(jnp.int32, (N_EXPERT_GROUPS, tm), 0)
    grp = jnp.zeros((N_EXPERT_GROUPS, tm), F32)
    slabs = [biased[g * per:(g + 1) * per] for g in range(N_EXPERT_GROUPS)]
    for g, v in enumerate(slabs):
        m1, i1 = _first_argmax(v, iota_per, per)
        m2 = jnp.max(jnp.where(iota_per == i1, -jnp.inf, v), axis=0, keepdims=True)
        grp = jnp.where(iota_grp == g, m1 + m2, grp)
    keep = jnp.zeros((N_EXPERT_GROUPS, tm), jnp.int32)
    for _ in range(TOPK_GROUPS):
        _, gi = _first_argmax(grp, iota_grp, N_EXPERT_GROUPS)
        keep = jnp.where(iota_grp == gi, 1, keep)
        grp = jnp.where(iota_grp == gi, -jnp.inf, grp)
    v = jnp.concatenate([jnp.where(keep[g:g + 1] > 0, slabs[g], -jnp.inf) for g in range(N_EXPERT_GROUPS)], axis=0)
    iota_e = lax.broadcasted_iota(jnp.int32, (N_EXPERTS, tm), 0)
    iota_k = lax.broadcasted_iota(jnp.int32, (TOP_K, tm), 0)
    eidx = jnp.zeros((TOP_K, tm), jnp.int32)
    ew = jnp.zeros((TOP_K, tm), F32)
    for k in range(TOP_K):
        _, ei = _first_argmax(v, iota_e, N_EXPERTS)
        hit = iota_e == ei
        eidx = jnp.where(iota_k == k, ei, eidx)
        ew = jnp.where(iota_k == k, jnp.sum(jnp.where(hit, scores, 0.0), axis=0, keepdims=True), ew)
        v = jnp.where(hit, -jnp.inf, v)
    return eidx, ew / jnp.sum(ew, axis=0, keepdims=True) * ROUTED_SCALE


def _moe_in_kernel(x_ref, g_ref, sh_ref, sc_ref, rwt_ref, rb_ref, sg_ref, su_ref, sd_ref,
                   h_ref, eidx_ref, ew_ref, shared_ref):
    h = _modulated_norm(x_ref[0], g_ref[...], sc_ref[0], sh_ref[0])
    logits_t = lax.dot_general(rwt_ref[...], h, (((1,), (1,)), ((), ())), preferred_element_type=F32,
                               precision=lax.Precision.HIGHEST)
    eidx, ew = _route_tokens(jax.nn.sigmoid(logits_t), rb_ref[...])
    eidx_ref[...] = eidx
    ew_ref[...] = ew
    h_ref[0] = h
    hb = h.astype(BF16)
    gate = jnp.dot(hb, sg_ref[...], preferred_element_type=F32)
    up = jnp.dot(hb, su_ref[...], preferred_element_type=F32)
    mid = (jax.nn.silu(gate) * up).astype(BF16)
    shared_ref[0] = jnp.dot(mid, sd_ref[...], preferred_element_type=F32)


def moe_in(x, g, shift, scale, router_w, router_b, sg_bf, su_bf, sd_bf, tm):
    b, t, k = x.shape
    per_row = shift.shape[1] != 1
    mod_spec = (pl.BlockSpec((1, tm, k), lambda bi, i: (bi, i, 0)) if per_row
                else pl.BlockSpec((1, 1, k), lambda bi, i: (bi, 0, 0)))
    row_spec = lambda wd: pl.BlockSpec((1, tm, wd), lambda bi, i: (bi, i, 0))
    full = lambda a: pl.BlockSpec(a.shape, lambda bi, i: (0,) * a.ndim)
    tiles = t // tm
    tok_spec = pl.BlockSpec((TOP_K, tm), lambda bi, i: (0, bi * tiles + i))
    rwt = router_w.T
    rb = router_b.astype(F32).reshape(N_EXPERTS, 1)
    return pl.pallas_call(
        _moe_in_kernel,
        grid=(b, tiles),
        in_specs=[row_spec(k), pl.BlockSpec((1, k), lambda bi, i: (0, 0)), mod_spec, mod_spec,
                  full(rwt), full(rb), full(sg_bf), full(su_bf), full(sd_bf)],
        out_specs=[row_spec(k), tok_spec, tok_spec, row_spec(k)],
        out_shape=[jax.ShapeDtypeStruct((b, t, k), F32), jax.ShapeDtypeStruct((TOP_K, b * t), jnp.int32),
                   jax.ShapeDtypeStruct((TOP_K, b * t), F32), jax.ShapeDtypeStruct((b, t, k), F32)],
        compiler_params=_cparams("parallel", "parallel"),
        name="moe_in",
    )(x, g.reshape(1, k), shift, scale, rwt, rb, sg_bf, su_bf, sd_bf)


def _gmm_kernel(blk_e_ref, n_used_ref, x_ref, wg_ref, wu_ref, wd_ref, o_ref, wg_s, wu_s, wd_s):
    blk = pl.program_id(0)
    prev_e = blk_e_ref[jnp.maximum(blk - 1, 0)]
    new_expert = (blk == 0) | (blk_e_ref[blk] != prev_e)

    @pl.when(new_expert)
    def _():
        wg_s[...] = wg_ref[0].astype(BF16)
        wu_s[...] = wu_ref[0].astype(BF16)
        wd_s[...] = wd_ref[0].astype(BF16)

    @pl.when(blk < n_used_ref[0])
    def _():
        x = x_ref[...].astype(BF16)
        gate = jnp.dot(x, wg_s[...], preferred_element_type=F32)
        up = jnp.dot(x, wu_s[...], preferred_element_type=F32)
        mid = (jax.nn.silu(gate) * up).astype(BF16)
        o_ref[...] = jnp.dot(mid, wd_s[...], preferred_element_type=F32)

    @pl.when(blk >= n_used_ref[0])
    def _():
        o_ref[...] = jnp.zeros_like(o_ref)


def grouped_experts(xs, blk_e, n_used, w_gate, w_up, w_down):
    cap, k = xs.shape
    tm = MOE_TILE
    n_blk = cap // tm
    de = w_gate.shape[-1]
    grid_spec = pltpu.PrefetchScalarGridSpec(
        num_scalar_prefetch=2,
        grid=(n_blk,),
        in_specs=[pl.BlockSpec((tm, k), lambda i, be, nu: (i, 0)),
                  pl.BlockSpec((1, k, de), lambda i, be, nu: (be[i], 0, 0)),
                  pl.BlockSpec((1, k, de), lambda i, be, nu: (be[i], 0, 0)),
                  pl.BlockSpec((1, de, k), lambda i, be, nu: (be[i], 0, 0))],
        out_specs=pl.BlockSpec((tm, k), lambda i, be, nu: (i, 0)),
        scratch_shapes=[pltpu.VMEM((k, de), BF16), pltpu.VMEM((k, de), BF16), pltpu.VMEM((de, k), BF16)],
    )
    return pl.pallas_call(
        _gmm_kernel,
        grid_spec=grid_spec,
        out_shape=jax.ShapeDtypeStruct((cap, k), F32),
        compiler_params=_cparams("arbitrary"),
        name="grouped_experts",
    )(blk_e, n_used, xs, w_gate, w_up, w_down)


def _moe_out_kernel(x_ref, gate_ref, picked_ref, ew_ref, shared_ref, o_ref):
    routed = picked_ref[0] * ew_ref[:, 0:1]
    for k in range(1, TOP_K):
        routed = routed + picked_ref[k] * ew_ref[:, k:k + 1]
    o_ref[0] = x_ref[0] + gate_ref[0] * (routed + shared_ref[0])


def moe_out(x, gate, picked, ew, shared, tm, row0):
    b, t, d = x.shape
    per_row = gate.shape[1] != 1
    gate_spec = (pl.BlockSpec((1, tm, d), lambda bi, i: (bi, i, 0)) if per_row
                 else pl.BlockSpec((1, 1, d), lambda bi, i: (bi, 0, 0)))
    row_spec = pl.BlockSpec((1, tm, d), lambda bi, i: (bi, i, 0))
    tiles, off = t // tm, row0 // tm
    return pl.pallas_call(
        _moe_out_kernel,
        grid=(b, tiles),
        in_specs=[row_spec, gate_spec,
                  pl.BlockSpec((TOP_K, tm, d), lambda bi, i: (0, off + bi * tiles + i, 0)),
                  pl.BlockSpec((tm, TOP_K), lambda bi, i: (off + bi * tiles + i, 0)),
                  row_spec],
        out_specs=row_spec,
        out_shape=jax.ShapeDtypeStruct((b, t, d), F32),
        compiler_params=_cparams("parallel", "parallel"),
        name="moe_out",
    )(x, gate, picked, ew, shared)


def _final_norm_kernel(x_ref, g_ref, o_ref):
    x = x_ref[0]
    var = jnp.mean(x * x, axis=-1, keepdims=True)
    o_ref[0] = x * lax.rsqrt(var + RMS_EPS) * g_ref[...]


def final_norm(x, g, tm):
    b, t, d = x.shape
    row_spec = pl.BlockSpec((1, tm, d), lambda bi, i: (bi, i, 0))
    return pl.pallas_call(
        _final_norm_kernel,
        grid=(b, t // tm),
        in_specs=[row_spec, pl.BlockSpec((1, d), lambda bi, i: (0, 0))],
        out_specs=row_spec,
        out_shape=jax.ShapeDtypeStruct((b, t, d), F32),
        compiler_params=_cparams("parallel", "parallel"),
        name="final_norm",
    )(x, g.reshape(1, d))


def _ada_kernel(c_ref, w_ref, b_ref, o_ref):
    a = jax.nn.silu(c_ref[...]).astype(BF16)
    o_ref[0] = jnp.dot(a, w_ref[0].astype(BF16), preferred_element_type=F32) + b_ref[0]


def ada_mods(c_all, ada_w, ada_b):
    n, d = c_all.shape
    depth, _, width = ada_w.shape
    tn = 1024
    return pl.pallas_call(
        _ada_kernel,
        grid=(depth, width // tn),
        in_specs=[pl.BlockSpec((n, d), lambda l, j: (0, 0)),
                  pl.BlockSpec((1, d, tn), lambda l, j: (l, 0, j)),
                  pl.BlockSpec((1, 1, tn), lambda l, j: (l, 0, j))],
        out_specs=pl.BlockSpec((1, n, tn), lambda l, j: (l, 0, j)),
        out_shape=jax.ShapeDtypeStruct((depth, n, width), F32),
        compiler_params=_cparams("parallel", "parallel"),
        name="ada_mods",
    )(c_all, ada_w, ada_b.reshape(depth, 1, width))


def _masked_softmax(s, mask, sink=None):
    s = jnp.where(mask, s, -jnp.inf)
    m = jnp.max(s, axis=-1, keepdims=True)
    if sink is not None:
        m = jnp.maximum(m, sink)
    m = jnp.where(jnp.isfinite(m), m, 0.0)
    e = jnp.exp(s - m)
    den = jnp.sum(e, axis=-1, keepdims=True)
    if sink is not None:
        den = den + jnp.exp(sink - m)
    p = e / jnp.maximum(den, 1e-30)
    return p, (m + jnp.log(den))[..., 0]


def _cmp_to_sel(p, n_slc):
    ratio = SEL_BLOCK // CMP_STRIDE
    left = CMP_LEN // CMP_STRIDE - 1
    pad = [(0, 0)] * (p.ndim - 1) + [(left, ratio * n_slc - p.shape[-1])]
    pp = jnp.pad(p, pad)
    out = pp[..., 0:ratio * n_slc:ratio]
    for o in range(1, ratio + left):
        out = out + pp[..., o:o + ratio * n_slc:ratio]
    return out


def _select_blocks(imp, pos, n_slc):
    k_top = N_SEL - N_FORCED
    cur = pos // SEL_BLOCK
    j = jnp.arange(n_slc)
    cand = (j[None, :] >= 1) & (j[None, :] <= cur[:, None] - 2)
    sc = jnp.where(cand, imp, -jnp.inf)
    if n_slc < k_top:
        sc = jnp.pad(sc, ((0, 0), (0, 0), (0, 0), (0, k_top - n_slc)), constant_values=-jnp.inf)
    vals, idx = lax.top_k(sc, k_top)
    forced = jnp.stack([cur, cur - 1, jnp.zeros_like(cur)], axis=-1)
    forced_ok = jnp.stack([cur >= 0, cur >= 1, cur >= 2], axis=-1)
    lead = imp.shape[:2]
    blocks = jnp.concatenate([jnp.broadcast_to(forced, lead + forced.shape), idx.astype(cur.dtype)], axis=-1)
    ok = jnp.concatenate([jnp.broadcast_to(forced_ok, lead + forced_ok.shape), vals > -jnp.inf], axis=-1)
    return jnp.clip(blocks, 0, n_slc - 1), ok


def _compress(hp, n_cmp, w1h, b1, w2, pe):
    b = hp.shape[0]
    r = CMP_LEN // CMP_STRIDE
    hp = hp.reshape(b, hp.shape[1], 2, N_KV_C, r, CMP_HID)
    pre = (jnp.einsum('cjsd,cjsdh->ch', pe.reshape(2, r, CMP_STRIDE, HEAD_DIM), w1h) + b1)[:, None, :]
    for j in range(r):
        pre = pre + hp[:, j:j + n_cmp, :, :, j]
    return jnp.einsum('bnckh,chd->bnckd', jax.nn.gelu(pre), w2)


def _compressed_attention(q_cmp, comp, pos):
    blk_end = jnp.arange(comp.shape[1]) * CMP_STRIDE + CMP_LEN - 1
    s = jnp.einsum('bqhgd,bnhd->bhgqn', q_cmp, comp[:, :, 0], preferred_element_type=F32) * ATTN_SCALE
    p_cmp, _ = _masked_softmax(s, blk_end[None, :] <= pos[:, None])
    o_cmp = jnp.einsum('bhgqn,bnhd->bqhgd', p_cmp.astype(comp.dtype), comp[:, :, 1])
    return o_cmp, p_cmp


A_SPECS = ((0, (N_Q_A + N_KV_A) * HEAD_DIM, 'rope', 0, 0),
           ((N_Q_A + N_KV_A) * HEAD_DIM, (N_Q_A + 2 * N_KV_A) * HEAD_DIM, 'raw', 0, (N_Q_A + N_KV_A) * HEAD_DIM))
B_SPECS = ((0, 2 * B_WIDTH, 'rope', 0, 0), (2 * B_WIDTH, 3 * B_WIDTH, 'raw', 0, 2 * B_WIDTH))
_C0 = C_Q_WIDTH + C_KV_WIDTH
_C1 = _C0 + C_KV_WIDTH
C_SPECS = ((0, _C0, 'raw', 0, 0),
           (_C0, _C0 + C_KV_WIDTH // 2, 'rope', 0, _C0), (_C0 + C_KV_WIDTH // 2, _C1, 'raw', 0, _C0 + C_KV_WIDTH // 2),
           (_C1, _C1 + C_KV_WIDTH // 2, 'rope', 0, _C1), (_C1 + C_KV_WIDTH // 2, C_MAIN_WIDTH, 'raw', 0, _C1 + C_KV_WIDTH // 2),
           (0, C_Q_WIDTH, 'rope', 1, 0),
           (C_MAIN_WIDTH, C_MAIN_WIDTH + N_GATES_C, 'sigmoid', 2, 0))
_AK = N_Q_A * HEAD_DIM
_AW = N_KV_A * HEAD_DIM
A_SPECS_P = A_SPECS + ((_AK, _AK + _AW, 'rope', 1, 0), (_AK + _AW, _AK + 2 * _AW, 'raw', 1, _AW))
_BW = B_HEADS_PER_GROUP * HEAD_DIM
B_SPECS_P = B_SPECS + tuple(
    seg for gi in range(N_GROUPS_B) for seg in
    ((B_WIDTH + gi * _BW, B_WIDTH + (gi + 1) * _BW, 'rope', 1 + gi, 0),
     (2 * B_WIDTH + gi * _BW, 2 * B_WIDTH + (gi + 1) * _BW, 'raw', 1 + gi, _BW)))
_CH = C_KV_WIDTH // 2
C_SPECS_P = C_SPECS + ((C_Q_WIDTH, _C0, 'raw', 3, 0),
                       (_C0, _C0 + _CH, 'rope', 4, 0), (_C0 + _CH, _C1, 'raw', 4, _CH),
                       (_C1, _C1 + _CH, 'rope', 5, 0), (_C1 + _CH, C_MAIN_WIDTH, 'raw', 5, _CH))


def _kv_from_cols(cols, n_kv, rows):
    b = cols.shape[0]
    return cols[:, :, cols.shape[2] - rows:].reshape(b, 2, n_kv, HEAD_DIM, rows).transpose(0, 4, 1, 2, 3)


def _mixer_a(x, mods, g, w_qkv_bf, w_o_bf, sink, tabs, tm, past):
    shift, scale, gate = mods
    nq = N_Q_A * HEAD_DIM
    kvw = N_KV_A * HEAD_DIM
    if past is None:
        y, kv_cols = norm_linear(x, g, shift, scale, w_qkv_bf, tabs, A_SPECS_P,
                                 ((nq + 2 * kvw, 'rows'), (2 * kvw, 'cols_last')), tm)
        o = band_attention(y, y, y, dil=1, n_kv=N_KV_A, group=N_Q_A // N_KV_A, tq=WINDOW_A,
                           q_idx=0, k_idx=nq // kvw, v_idx=nq // kvw + 1, sink=sink)
        new = _kv_from_cols(kv_cols, N_KV_A, min(WINDOW_A, y.shape[1]))
    else:
        (y,) = norm_linear(x, g, shift, scale, w_qkv_bf, tabs, A_SPECS, ((nq + 2 * kvw, 'rows'),), tm)
        n = y.shape[1]
        o, new = decode_window_attention(y[0, :, :nq].reshape(n, N_Q_A, HEAD_DIM),
                                         y[0, :, nq:nq + kvw].reshape(n, N_KV_A, HEAD_DIM),
                                         y[0, :, nq + kvw:].reshape(n, N_KV_A, HEAD_DIM), past, sink=sink)
        o = o.reshape(1, n, nq)
    return linear_out([o], w_o_bf, x, gate, 'plain', tm), new


def _mixer_b(x, mods, g, w_qkv_bf, w_o_bf, tabs, tm, past):
    shift, scale, gate = mods
    hpg = B_HEADS_PER_GROUP
    gw = hpg * HEAD_DIM
    outs, lses, news = [], [], []
    if past is None:
        t = x.shape[1]
        descs = [(3 * B_WIDTH, 'rows')] + [(2 * gw, 'cols_last' if win <= tm else 'cols') for win, _ in B_PATTERNS]
        y, *kv_cols = norm_linear(x, g, shift, scale, w_qkv_bf, tabs, B_SPECS_P, descs, tm)
        for gi, (win, dil) in enumerate(B_PATTERNS):
            o, lse = band_attention(y, y, y, dil=dil, n_kv=hpg, group=1, tq=win // dil,
                                    q_idx=gi, k_idx=N_GROUPS_B + gi, v_idx=2 * N_GROUPS_B + gi, want_lse=True)
            outs.append(o)
            lses.append(lse)
            news.append(_kv_from_cols(kv_cols[gi], hpg, min(win, t)))
    else:
        (y,) = norm_linear(x, g, shift, scale, w_qkv_bf, tabs, B_SPECS, ((3 * B_WIDTH, 'rows'),), tm)
        n = y.shape[1]
        for gi, (win, dil) in enumerate(B_PATTERNS):
            part = lambda c: y[0, :, c * B_WIDTH + gi * gw:c * B_WIDTH + (gi + 1) * gw].reshape(n, hpg, HEAD_DIM)
            o, lse, new = decode_window_attention(part(0), part(1), part(2), past[gi], dil=dil, want_lse=True)
            outs.append(o.reshape(1, n, gw))
            lses.append(lse.reshape(1, n, gw))
            news.append(new)
    return linear_out(outs + lses, w_o_bf, x, gate, 'mix3', tm), tuple(news)


def _mixer_c(x, mods, g, w_in_bf, w_o_bf, cmp_w1, cmp_b1, cmp_w2, cmp_pe, tabs, pos, tm, past):
    shift, scale, gate = mods
    grp = N_Q_C // N_KV_C
    kvw = N_KV_C * HEAD_DIM
    descs = [(C_MAIN_WIDTH, 'rows'), (C_Q_WIDTH, 'rows'), (N_GATES_C, 'rows')]
    w1h = cmp_w1.reshape(2, CMP_LEN // CMP_STRIDE, CMP_STRIDE, HEAD_DIM, CMP_HID)
    w_bd = _half_proj_weight(w1h)
    cmp_idx, slc_idx, win_idx = C_Q_WIDTH // C_KV_WIDTH, _C0 // C_KV_WIDTH, _C1 // C_KV_WIDTH
    if past is None:
        descs += [(C_KV_WIDTH, 'cols'), (C_KV_WIDTH, 'cols'), (C_KV_WIDTH, 'cols_last')]
        y, q_rope, gates, cmp_cols, slc_cols, win_cols = norm_linear(x, g, shift, scale, w_in_bf, tabs, C_SPECS_P,
                                                                     descs, tm)
        b, t = y.shape[:2]
        hp = half_block_proj(y, cmp_idx, w_bd, t)
        n_cmp = (t - CMP_LEN) // CMP_STRIDE + 1
        comp = _compress(hp, n_cmp, w1h, cmp_b1, cmp_w2, cmp_pe)
        o_cmp, selmask = cmp_select_prompt(y, comp, gates)
        o_slc = selected_attention_prompt(q_rope, y, selmask, gates, k_idx=2 * slc_idx, v_idx=2 * slc_idx + 1,
                                          gate_col=N_Q_C)
        o_win = band_attention(q_rope, y, y, dil=1, n_kv=N_KV_C, group=grp, tq=WINDOW_C, q_idx=0,
                               k_idx=2 * win_idx, v_idx=2 * win_idx + 1, gate=gates, gate_col=2 * N_Q_C)
        new_win = _kv_from_cols(win_cols, N_KV_C, min(WINDOW_C, t))
        cmp_kv = _kv_from_cols(cmp_cols, N_KV_C, t)
        slc_kv = _kv_from_cols(slc_cols, N_KV_C, t)
    else:
        y, q_rope, gates = norm_linear(x, g, shift, scale, w_in_bf, tabs, C_SPECS, descs, tm)
        win_buf, cmp_pool, slc_pool, page_table = past
        n = y.shape[1]
        t = 1
        cmp_kv = y[0, :, C_Q_WIDTH:_C0].reshape(n, t, 2, N_KV_C, HEAD_DIM)
        slc_kv = y[0, :, _C0:_C1].reshape(n, t, 2, N_KV_C, HEAD_DIM)
        win_kv = y[0, :, _C1:].reshape(n, t, 2, N_KV_C, HEAD_DIM)
        q_cmp = y[0, :, :C_Q_WIDTH].reshape(n, t, N_KV_C, grp, HEAD_DIM)
        gview = gates[0].reshape(n, t, 3, N_KV_C, grp, 1)
        hp = paged_half_block_proj(cmp_pool, page_table, w_bd, CMP_PAGES_PER_STEP)
        length = PAST_LEN + t
        n_cmp = (length - CMP_LEN) // CMP_STRIDE + 1
        comp = _compress(hp, n_cmp, w1h, cmp_b1, cmp_w2, cmp_pe)
        o_cmp, p_cmp = _compressed_attention(q_cmp, comp, pos)
        n_slc = -(-length // SEL_BLOCK)
        blocks, ok = _select_blocks(_cmp_to_sel(p_cmp.sum(axis=2), n_slc), pos, n_slc)
        sub = PAGE_SIZE // SEL_BLOCK
        pb = jnp.minimum(blocks[:, :, 0], PAST_LEN // SEL_BLOCK - 1)
        page = jnp.take_along_axis(jnp.broadcast_to(page_table[:, None], (n, N_KV_C, page_table.shape[1])),
                                   pb // sub, axis=2)
        want = jnp.where(ok[:, :, 0], pb % sub, -1).astype(F32)
        o_slc = selected_attention_sample(
            q_rope[0].reshape(n, N_Q_C, HEAD_DIM), jnp.repeat(slc_kv[:, 0, 0], grp, axis=1),
            jnp.repeat(slc_kv[:, 0, 1], grp, axis=1), gates[0, :, N_Q_C:2 * N_Q_C].reshape(n, N_Q_C, 1),
            slc_pool, page.reshape(-1).astype(jnp.int32), jnp.repeat(want, grp, axis=1))
        o_slc = o_slc.reshape(1, n, C_Q_WIDTH)
        o_win, new_win = decode_window_attention(
            q_rope[0].reshape(n, N_Q_C, HEAD_DIM), win_kv[:, 0, 0], win_kv[:, 0, 1], win_buf,
            gate=gates[0, :, 2 * N_Q_C:].reshape(n, N_Q_C, 1))
        o_win = o_win.reshape(1, n, C_Q_WIDTH)
        o_cmp = (gview[:, :, 0] * o_cmp.astype(F32)).reshape(1, n, C_Q_WIDTH)
    return linear_out([o_cmp, o_slc, o_win], w_o_bf, x, gate, 'sum3', tm), (new_win, cmp_kv, slc_kv)


def _dispatch_plan(eidx, n_blk):
    n = eidx.shape[0]
    tm = MOE_TILE
    chunk = LANES_V7X
    onehot = (eidx[:, :, None] == jnp.arange(N_EXPERTS)).astype(jnp.int32)
    sel = onehot.sum(axis=1).astype(F32).reshape(n // chunk, chunk, N_EXPERTS)
    tril = jnp.tril(jnp.ones((chunk, chunk), F32))
    within = jnp.einsum('ij,cjk->cik', tril, sel)
    chunk_tot = within[:, -1, :]
    before = jnp.cumsum(chunk_tot, axis=0) - chunk_tot
    rank = (within - sel + before[:, None, :]).reshape(n, N_EXPERTS).astype(jnp.int32)
    counts = jnp.sum(chunk_tot, axis=0).astype(jnp.int32)
    padded = (counts + tm - 1) // tm * tm
    ends = jnp.cumsum(padded)
    dest = jnp.sum(onehot * (ends - padded + rank)[:, None, :], axis=-1)
    tok = jnp.broadcast_to(jnp.arange(n, dtype=jnp.int32)[:, None], dest.shape)
    row_tok = jnp.zeros((n_blk * tm,), jnp.int32).at[dest.reshape(-1)].set(tok.reshape(-1))
    blk_start = jnp.arange(n_blk, dtype=jnp.int32) * tm
    blk_e = jnp.minimum(jnp.sum(ends[None, :] <= blk_start[:, None], axis=1), N_EXPERTS - 1).astype(jnp.int32)
    n_used = (ends[-1] // tm).astype(jnp.int32).reshape(1)
    return dest, row_tok, blk_e, n_used


def _moe(xp, xs, mods_p, mods_s, g, router_w, router_b, w_gate, w_up, w_down, sg_bf, su_bf, sd_bf):
    hp, ep, wp, shp = moe_in(xp, g, mods_p[0], mods_p[1], router_w, router_b, sg_bf, su_bf, sd_bf, ROW_TILE)
    hs, es, ws, shs = moe_in(xs, g, mods_s[0], mods_s[1], router_w, router_b, sg_bf, su_bf, sd_bf, xs.shape[1])
    d = xp.shape[-1]
    n_p = xp.shape[0] * xp.shape[1]
    h_all = jnp.concatenate([hp.reshape(-1, d), hs.reshape(-1, d)], axis=0)
    eidx = jnp.concatenate([ep, es], axis=1).T
    ew = jnp.concatenate([wp, ws], axis=1).T
    n = h_all.shape[0]
    n_blk = (n * TOP_K + N_EXPERTS * (MOE_TILE - 1)) // MOE_TILE + 1
    dest, row_tok, blk_e, n_used = _dispatch_plan(eidx, n_blk)
    ys = grouped_experts(h_all[row_tok], blk_e, n_used, w_gate, w_up, w_down)
    picked = ys[dest.T.reshape(-1)].reshape(TOP_K, n, d)
    xp = moe_out(xp, mods_p[2], picked, ew, shp, MOE_OUT_TILE, 0)
    xs = moe_out(xs, mods_s[2], picked, ew, shs, xs.shape[1], n_p)
    return xp, xs


def kernel(x_prompt, x_sample, c_prompt, c_sample, cache_a_kv, cache_b_kv_w128, cache_b_kv_w512, cache_b_kv_w2048, cache_c_win_kv, cache_c_cmp_kv, cache_c_slc_kv, page_table, norm_g, final_g, ada_w, ada_b, a_w_qkv, a_w_o, a_sink, b_w_qkv, b_w_o, c_w_in, c_w_o, c_cmp_w1, c_cmp_b1, c_cmp_w2, c_cmp_pe, moe_router, moe_bias, moe_w_gate, moe_w_up, moe_w_down, shared_w_gate, shared_w_up, shared_w_down):
    bp, seq, d = x_prompt.shape
    ns = x_sample.shape[0]
    b_caches = (cache_b_kv_w128, cache_b_kv_w512, cache_b_kv_w2048)
    pos_p = jnp.arange(seq, dtype=jnp.int32)
    pos_s = PAST_LEN + jnp.arange(x_sample.shape[1], dtype=jnp.int32)
    tabs_p = _rope_tables(pos_p, seq)
    tabs_s = _rope_tables(pos_s, ns)

    mods = ada_mods(jnp.concatenate([c_prompt, c_sample], axis=0), ada_w, ada_b)
    mods_p = mods[:, :bp].reshape(DEPTH, bp, 6, 1, d)
    mods_s = mods[:, bp:].reshape(DEPTH, 1, ns, 6, d)

    xp = x_prompt
    xs = x_sample.reshape(1, ns, d)
    st_p = {0: [], 1: [], 2: []}
    st_s = {0: [], 1: [], 2: []}
    for l in range(DEPTH):
        kind, slot = LAYER_KIND[l], LAYER_SLOT[l]
        mp = [mods_p[l, :, i] for i in range(6)]
        ms = [mods_s[l, :, :, i] for i in range(6)]
        g_mix, g_moe = norm_g[l, 0], norm_g[l, 1]
        if kind == 0:
            w_in, w_o = a_w_qkv[slot].astype(BF16), a_w_o[slot].astype(BF16)
            xp, sp = _mixer_a(xp, mp[:3], g_mix, w_in, w_o, a_sink[slot], tabs_p, ROW_TILE, None)
            xs, ss = _mixer_a(xs, ms[:3], g_mix, w_in, w_o, a_sink[slot], tabs_s, ns, cache_a_kv[slot])
        elif kind == 1:
            w_in, w_o = b_w_qkv[slot].astype(BF16), b_w_o[slot].astype(BF16)
            xp, sp = _mixer_b(xp, mp[:3], g_mix, w_in, w_o, tabs_p, ROW_TILE, None)
            xs, ss = _mixer_b(xs, ms[:3], g_mix, w_in, w_o, tabs_s, ns, tuple(buf[slot] for buf in b_caches))
        else:
            w_in, w_o = c_w_in[slot].astype(BF16), c_w_o[slot].astype(BF16)
            cargs = (c_cmp_w1[slot], c_cmp_b1[slot], c_cmp_w2[slot], c_cmp_pe[slot])
            xp, sp = _mixer_c(xp, mp[:3], g_mix, w_in, w_o, *cargs, tabs_p, pos_p, ROW_TILE, None)
            xs, ss = _mixer_c(xs, ms[:3], g_mix, w_in, w_o, *cargs, tabs_s, pos_s, ns,
                              (cache_c_win_kv[slot], cache_c_cmp_kv[slot], cache_c_slc_kv[slot], page_table))
        st_p[kind].append(sp)
        st_s[kind].append(ss)
        xp, xs = _moe(xp, xs, mp[3:], ms[3:], g_moe, moe_router[l], moe_bias[l], moe_w_gate[l], moe_w_up[l],
                      moe_w_down[l], shared_w_gate[l].astype(BF16), shared_w_up[l].astype(BF16),
                      shared_w_down[l].astype(BF16))
    y_prompt = final_norm(xp, final_g, ROW_TILE)
    y_sample = final_norm(xs, final_g, ns).reshape(x_sample.shape)

    outs = [y_prompt, y_sample, jnp.stack(st_p[0]), jnp.stack(st_s[0])]
    for i in range(N_GROUPS_B):
        outs += [jnp.stack([s[i] for s in st_p[1]]), jnp.stack([s[i] for s in st_s[1]])]
    outs += [jnp.stack([s[0] for s in st_p[2]]), jnp.stack([s[0] for s in st_s[2]])]
    outs += [jnp.stack([s[1] for s in st_p[2]]), jnp.stack([s[1] for s in st_s[2]])]
    outs += [jnp.stack([s[2] for s in st_p[2]]), jnp.stack([s[2] for s in st_s[2]])]
    return tuple(outs)
```

```python
import functools

import jax
import jax.numpy as jnp
from jax import lax
from jax.experimental import pallas as pl
from jax.experimental.pallas import tpu as pltpu

D_MODEL = 1024
DEPTH = 4
PAST_LEN = 8192
PAGE_SIZE = 128
HEAD_DIM = 64
ROT_DIM = HEAD_DIM // 4
ROPE_THETA = 500000.0
ATTN_SCALE = HEAD_DIM ** -0.5
RMS_EPS = 1e-6

N_MIXERS = 3
LAYER_KIND = tuple(i % N_MIXERS for i in range(DEPTH))
LAYER_SLOT = tuple(LAYER_KIND[:i].count(LAYER_KIND[i]) for i in range(DEPTH))

N_Q_A = 16
N_KV_A = 4
WINDOW_A = 128
B_PATTERNS = ((128, 1), (512, 4), (2048, 16))
N_GROUPS_B = len(B_PATTERNS)
B_HEADS_PER_GROUP = 4
B_WIDTH = N_GROUPS_B * B_HEADS_PER_GROUP * HEAD_DIM
N_Q_C = 16
N_KV_C = 4
CMP_LEN = 32
CMP_STRIDE = 16
CMP_HID = 64
SEL_BLOCK = 64
N_SEL = 16
N_FORCED = 3
WINDOW_C = 512
C_KV_WIDTH = 2 * N_KV_C * HEAD_DIM
C_Q_WIDTH = N_Q_C * HEAD_DIM
C_MAIN_WIDTH = C_Q_WIDTH + 3 * C_KV_WIDTH
N_GATES_C = 3 * N_Q_C

N_EXPERTS = 64
TOP_K = 8
N_EXPERT_GROUPS = 8
TOPK_GROUPS = 4
D_EXPERT = 256
ROUTED_SCALE = 2.5

LANES_V7X = 128
VMEM_LIMIT_V7X = 56 * 1024 * 1024

ROW_TILE = 512
COL_GROUP = 512
MOE_TILE = 512
MOE_OUT_TILE = 256
CMP_PAGES_PER_STEP = 32
SEL_Q_TILE = 128
SEL_K_TILE = 256
NEG_BIG = -1e30

BF16 = jnp.bfloat16
F32 = jnp.float32


def _cparams(*sem):
    return pltpu.CompilerParams(dimension_semantics=sem, vmem_limit_bytes=VMEM_LIMIT_V7X)


def _rope_tables(pos, rows):
    half = ROT_DIM // 2
    inv_freq = ROPE_THETA ** (-jnp.arange(half, dtype=F32) / half)
    ang = pos.astype(F32)[:, None] * inv_freq
    cos, sin = jnp.cos(ang), jnp.sin(ang)
    t = pos.shape[0]
    z8 = jnp.zeros((t, half), F32)
    rest1 = jnp.ones((t, HEAD_DIM - ROT_DIM), F32)
    rest0 = jnp.zeros((t, HEAD_DIM - ROT_DIM), F32)
    reps = LANES_V7X // HEAD_DIM
    cos_t = jnp.tile(jnp.concatenate([cos, cos, rest1], axis=1), (1, reps))
    sin_a = jnp.tile(jnp.concatenate([-sin, z8, rest0], axis=1), (1, reps))
    sin_b = jnp.tile(jnp.concatenate([z8, sin, rest0], axis=1), (1, reps))
    return tuple(jnp.broadcast_to(a, (rows, LANES_V7X)) for a in (cos_t, sin_a, sin_b))


def _rope_chunk(blk, cos_t, sin_a, sin_b):
    return (blk * cos_t + pltpu.roll(blk, LANES_V7X - ROT_DIM // 2, 1) * sin_a
            + pltpu.roll(blk, ROT_DIM // 2, 1) * sin_b)


def _build_plan(n_cols, specs):
    plan = []
    for c0 in range(0, n_cols, COL_GROUP):
        width = min(COL_GROUP, n_cols - c0)
        segs = []
        for (s0, s1, mode, oi, d0) in specs:
            lo, hi = max(s0, c0), min(s1, c0 + width)
            if lo >= hi:
                continue
            step = LANES_V7X if mode == 'rope' else hi - lo
            for a in range(lo, hi, step):
                segs.append((a - c0, min(step, hi - a), mode, oi, d0 + a - s0))
        plan.append((c0, width, tuple(segs)))
    return tuple(plan)


def _modulated_norm(x, g, scale, shift):
    var = jnp.mean(x * x, axis=-1, keepdims=True)
    y = x * lax.rsqrt(var + RMS_EPS) * g
    return y * (1 + scale) + shift


def _norm_linear_kernel(x_ref, g_ref, sh_ref, sc_ref, w_ref, cos_ref, sa_ref, sb_ref, *refs, plan, layouts):
    outs, h_ref = refs[:len(layouts)], refs[len(layouts)]
    h_ref[...] = _modulated_norm(x_ref[0], g_ref[...], sc_ref[0], sh_ref[0]).astype(BF16)
    for (c0, width, segs) in plan:
        acc = jnp.dot(h_ref[...], w_ref[:, c0:c0 + width], preferred_element_type=F32)
        for (off, wd, mode, oi, dst) in segs:
            blk = acc[:, off:off + wd]
            if mode == 'rope':
                blk = _rope_chunk(blk, cos_ref[...], sa_ref[...], sb_ref[...])
            elif mode == 'sigmoid':
                blk = jax.nn.sigmoid(blk)
            if layouts[oi] == 'rows':
                outs[oi][0, :, dst:dst + wd] = blk
            else:
                outs[oi][0, dst:dst + wd, :] = blk.T


def norm_linear(x, g, shift, scale, w_bf, tabs, specs, out_descs, tm):
    b, t, k = x.shape
    n = w_bf.shape[1]
    per_row = shift.shape[1] != 1
    mod_spec = (pl.BlockSpec((1, tm, k), lambda bi, i: (bi, i, 0)) if per_row
                else pl.BlockSpec((1, 1, k), lambda bi, i: (bi, 0, 0)))
    tab_spec = pl.BlockSpec((tm, LANES_V7X), lambda bi, i: (i, 0))
    plan = _build_plan(n, specs)
    out_specs, out_shape = [], []
    for wd, layout in out_descs:
        if layout == 'rows':
            out_specs.append(pl.BlockSpec((1, tm, wd), lambda bi, i: (bi, i, 0)))
            out_shape.append(jax.ShapeDtypeStruct((b, t, wd), F32))
        elif layout == 'cols':
            out_specs.append(pl.BlockSpec((1, wd, tm), lambda bi, i: (bi, 0, i)))
            out_shape.append(jax.ShapeDtypeStruct((b, wd, t), F32))
        else:
            out_specs.append(pl.BlockSpec((1, wd, tm), lambda bi, i: (bi, 0, 0)))
            out_shape.append(jax.ShapeDtypeStruct((b, wd, tm), F32))
    outs = pl.pallas_call(
        functools.partial(_norm_linear_kernel, plan=plan, layouts=tuple(l for _, l in out_descs)),
        grid=(b, t // tm),
        in_specs=[pl.BlockSpec((1, tm, k), lambda bi, i: (bi, i, 0)),
                  pl.BlockSpec((1, k), lambda bi, i: (0, 0)),
                  mod_spec, mod_spec,
                  pl.BlockSpec((k, n), lambda bi, i: (0, 0)),
                  tab_spec, tab_spec, tab_spec],
        out_specs=out_specs,
        out_shape=out_shape,
        scratch_shapes=[pltpu.VMEM((tm, k), BF16)],
        compiler_params=_cparams("parallel", "arbitrary"),
        name="norm_linear",
    )(x, g.reshape(1, k), shift, scale, w_bf, *tabs)
    return outs


def _linear_out_kernel(*refs, mode, n_o):
    o_refs = refs[:n_o]
    w_ref, x_ref, gate_ref, out_ref = refs[n_o:]
    if mode == 'plain':
        o = o_refs[0][0]
    elif mode == 'sum3':
        o = (o_refs[0][0] + o_refs[1][0]) + o_refs[2][0]
    else:
        ng = n_o // 2
        lses = [r[0] for r in o_refs[ng:]]
        m = functools.reduce(jnp.maximum, lses)
        es = [jnp.exp(l - m) for l in lses]
        den = functools.reduce(lambda a, c: a + c, es)
        o = jnp.concatenate([(e / den) * r[0] for e, r in zip(es, o_refs[:ng])], axis=1)
    acc = jnp.dot(o.astype(BF16), w_ref[...], preferred_element_type=F32)
    out_ref[0] = x_ref[0] + gate_ref[0] * acc


def linear_out(o_list, w_bf, x, gate, mode, tm):
    b, t, d = x.shape
    per_row = gate.shape[1] != 1
    gate_spec = (pl.BlockSpec((1, tm, d), lambda bi, i: (bi, i, 0)) if per_row
                 else pl.BlockSpec((1, 1, d), lambda bi, i: (bi, 0, 0)))
    row_spec = lambda wd: pl.BlockSpec((1, tm, wd), lambda bi, i: (bi, i, 0))
    return pl.pallas_call(
        functools.partial(_linear_out_kernel, mode=mode, n_o=len(o_list)),
        grid=(b, t // tm),
        in_specs=[row_spec(o.shape[-1]) for o in o_list]
        + [pl.BlockSpec(w_bf.shape, lambda bi, i: (0, 0)), row_spec(d), gate_spec],
        out_specs=row_spec(d),
        out_shape=jax.ShapeDtypeStruct((b, t, d), F32),
        compiler_params=_cparams("parallel", "parallel"),
        name="linear_out",
    )(*o_list, w_bf, x, gate)


def _band_kernel(*refs, n_kv, group, tq, has_sink, has_gate, gate_col, want_lse):
    q_ref, kp_ref, kc_ref, vp_ref, vc_ref = refs[:5]
    pos = 5
    sink_ref = gate_ref = lse_ref = None
    if has_sink:
        sink_ref = refs[pos]
        pos += 1
    if has_gate:
        gate_ref = refs[pos]
        pos += 1
    o_ref = refs[pos]
    if want_lse:
        lse_ref = refs[pos + 1]
    no_prev = jnp.where(pl.program_id(2) == 0, 2 * tq, 0)
    rows = lax.broadcasted_iota(jnp.int32, (tq, 2 * tq), 0)
    cols = lax.broadcasted_iota(jnp.int32, (tq, 2 * tq), 1)
    mask = ((cols < tq) & (cols >= rows + no_prev)) | ((cols >= tq) & ((cols - tq) <= rows))
    for j in range(n_kv):
        ks = slice(j * HEAD_DIM, (j + 1) * HEAD_DIM)
        k = jnp.concatenate([kp_ref[0, :, ks], kc_ref[0, :, ks]], axis=0).astype(BF16)
        v = jnp.concatenate([vp_ref[0, :, ks], vc_ref[0, :, ks]], axis=0).astype(BF16)
        for g in range(group):
            h = j * group + g
            hs = slice(h * HEAD_DIM, (h + 1) * HEAD_DIM)
            q = (q_ref[0, :, hs] * ATTN_SCALE).astype(BF16)
            s = lax.dot_general(q, k, (((1,), (1,)), ((), ())), preferred_element_type=F32)
            s = jnp.where(mask, s, -jnp.inf)
            m = jnp.max(s, axis=-1, keepdims=True)
            if has_sink:
                m = jnp.maximum(m, sink_ref[h])
            e = jnp.exp(s - m)
            den = jnp.sum(e, axis=-1, keepdims=True)
            if has_sink:
                den = den + jnp.exp(sink_ref[h] - m)
            o = jnp.dot(e.astype(BF16), v, preferred_element_type=F32) / den
            if has_gate:
                o = gate_ref[0, :, gate_col + h:gate_col + h + 1] * o
            o_ref[0, :, hs] = o
            if want_lse:
                lse_ref[0, :, hs] = jnp.broadcast_to(m + jnp.log(den), (tq, HEAD_DIM))


def band_attention(qa, ka, va, *, dil, n_kv, group, tq, q_idx, k_idx, v_idx, sink=None, gate=None,
                   gate_col=0, want_lse=False):
    b, s = qa.shape[:2]
    l = s // dil
    qw, kw = n_kv * group * HEAD_DIM, n_kv * HEAD_DIM
    q_rs, k_rs, v_rs = qa.shape[2] // qw, ka.shape[2] // kw, va.shape[2] // kw
    q2, k2, v2 = (a.reshape(b, l, dil * a.shape[2]) for a in (qa, ka, va))
    prev = lambda i: jnp.maximum(i - 1, 0)
    in_specs = [pl.BlockSpec((1, tq, qw), lambda bi, r, i: (bi, i, r * q_rs + q_idx)),
                pl.BlockSpec((1, tq, kw), lambda bi, r, i: (bi, prev(i), r * k_rs + k_idx)),
                pl.BlockSpec((1, tq, kw), lambda bi, r, i: (bi, i, r * k_rs + k_idx)),
                pl.BlockSpec((1, tq, kw), lambda bi, r, i: (bi, prev(i), r * v_rs + v_idx)),
                pl.BlockSpec((1, tq, kw), lambda bi, r, i: (bi, i, r * v_rs + v_idx))]
    args = [q2, k2, k2, v2, v2]
    if sink is not None:
        in_specs.append(pl.BlockSpec(memory_space=pltpu.SMEM))
        args.append(sink.astype(F32))
    if gate is not None:
        in_specs.append(pl.BlockSpec((1, tq, gate.shape[2]), lambda bi, r, i: (bi, i, 0)))
        args.append(gate)
    o_spec = pl.BlockSpec((1, tq, qw), lambda bi, r, i: (bi, i, r))
    o_shape = jax.ShapeDtypeStruct((b, l, dil * qw), F32)
    res = pl.pallas_call(
        functools.partial(_band_kernel, n_kv=n_kv, group=group, tq=tq, has_sink=sink is not None,
                          has_gate=gate is not None, gate_col=gate_col, want_lse=want_lse),
        grid=(b, dil, l // tq),
        in_specs=in_specs,
        out_specs=[o_spec, o_spec] if want_lse else o_spec,
        out_shape=[o_shape, o_shape] if want_lse else o_shape,
        compiler_params=_cparams("parallel", "parallel", "arbitrary"),
        name="band_attention",
    )(*args)
    if want_lse:
        return res[0].reshape(b, s, qw), res[1].reshape(b, s, qw)
    return res.reshape(b, s, qw)


def _decode_kernel(*refs, n_kv, group, width, dil, has_sink, has_gate, want_lse):
    q_ref, knew_ref, vnew_ref, kcol_ref, vcol_ref, cache_ref = refs[:6]
    pos = 6
    sink_ref = gate_ref = lse_ref = None
    if has_sink:
        sink_ref = refs[pos]
        pos += 1
    if has_gate:
        gate_ref = refs[pos]
        pos += 1
    o_ref = refs[pos]
    pos += 1
    if want_lse:
        lse_ref = refs[pos]
        pos += 1
    newc_ref = refs[pos]
    lane_g = lax.broadcasted_iota(jnp.int32, (group, width), 1)
    lane_d = lax.broadcasted_iota(jnp.int32, (HEAD_DIM, width), 1)
    tap = ((width - lane_g) & (dil - 1)) == 0
    for j in range(n_kv):
        rows = slice(j * group, (j + 1) * group)
        k_t, v_t = cache_ref[0, 0, j], cache_ref[0, 1, j]
        q = q_ref[0, rows] * ATTN_SCALE
        sc = jnp.dot(q.astype(BF16), k_t.astype(BF16), preferred_element_type=F32)
        s_new = jnp.sum(q * knew_ref[0, j:j + 1], axis=-1, keepdims=True)
        sc = jnp.where(tap, sc, NEG_BIG)
        m = jnp.maximum(jnp.max(sc, axis=-1, keepdims=True), s_new)
        if has_sink:
            row_g = lax.broadcasted_iota(jnp.int32, (group, 1), 0)
            sink = jnp.zeros((group, 1), F32)
            for g in range(group):
                sink = jnp.where(row_g == g, sink_ref[j * group + g], sink)
            m = jnp.maximum(m, sink)
        p = jnp.where(tap, jnp.exp(sc - m), 0.0)
        p_new = jnp.exp(s_new - m)
        den = jnp.sum(p, axis=-1, keepdims=True) + p_new
        if has_sink:
            den = den + jnp.exp(sink - m)
        pv = lax.dot_general(p.astype(BF16), v_t.astype(BF16), (((1,), (1,)), ((), ())), preferred_element_type=F32)
        o = (pv + p_new * vnew_ref[0, j:j + 1]) / den
        if has_gate:
            o = gate_ref[0, rows] * o
        o_ref[0, rows] = o
        if want_lse:
            lse_ref[0, rows] = jnp.broadcast_to(m + jnp.log(den), (group, HEAD_DIM))
        newc_ref[0, 0, j] = jnp.where(lane_d == width - 1, kcol_ref[0, j], pltpu.roll(k_t, width - 1, 1))
        newc_ref[0, 1, j] = jnp.where(lane_d == width - 1, vcol_ref[0, j], pltpu.roll(v_t, width - 1, 1))


def decode_window_attention(q, k_new, v_new, cache, *, dil=1, sink=None, gate=None, want_lse=False):
    n, n_q, d = q.shape
    width, n_kv = cache.shape[1], cache.shape[3]
    group = n_q // n_kv
    cache_t = cache.transpose(0, 2, 3, 4, 1)
    row_spec = lambda a: pl.BlockSpec((1,) + a.shape[1:], lambda bi: (bi,) + (0,) * (a.ndim - 1))
    k_col, v_col = k_new[..., None], v_new[..., None]
    args = [q, k_new, v_new, k_col, v_col, cache_t]
    in_specs = [row_spec(a) for a in args]
    if sink is not None:
        in_specs.append(pl.BlockSpec(memory_space=pltpu.SMEM))
        args.append(sink.astype(F32))
    if gate is not None:
        in_specs.append(row_spec(gate))
        args.append(gate)
    o_shape = jax.ShapeDtypeStruct((n, n_q, d), F32)
    out_shape = [o_shape] + ([o_shape] if want_lse else []) + [jax.ShapeDtypeStruct(cache_t.shape, F32)]
    res = pl.pallas_call(
        functools.partial(_decode_kernel, n_kv=n_kv, group=group, width=width, dil=dil, has_sink=sink is not None,
                          has_gate=gate is not None, want_lse=want_lse),
        grid=(n,),
        in_specs=in_specs,
        out_specs=[row_spec(s) for s in out_shape],
        out_shape=out_shape,
        compiler_params=_cparams("parallel"),
        name="decode_window_attention",
    )(*args)
    return tuple(res[:-1]) + (res[-1].transpose(0, 4, 1, 2, 3),)


CMP_PAD = LANES_V7X


def _cmp_select_kernel(q_ref, ck_ref, cv_ref, fold_ref, gate_ref, o_ref, sel_ref, *, tq, n_cmp, n_kv, group):
    i = pl.program_id(1)
    wide = (CMP_PAD, group * tq)
    n_idx = lax.broadcasted_iota(jnp.int32, wide, 0)
    qpos_w = i * tq + (lax.broadcasted_iota(jnp.int32, wide, 1) & (tq - 1))
    valid = (n_idx * CMP_STRIDE + (CMP_LEN - 1) <= qpos_w) & (n_idx < n_cmp)
    n_slc_pad = fold_ref.shape[0]
    blk = lax.broadcasted_iota(jnp.int32, (n_slc_pad, tq), 0)
    cur = lax.shift_right_logical(i * tq + lax.broadcasted_iota(jnp.int32, (n_slc_pad, tq), 1),
                                  SEL_BLOCK.bit_length() - 1)
    cand = (blk >= 1) & (blk <= cur - 2)
    forced = (blk == cur) | ((blk == cur - 1) & (cur >= 1)) | ((blk == 0) & (cur >= 2))
    for j in range(n_kv):
        q = jnp.concatenate(
            [(q_ref[0, :, (j * group + g) * HEAD_DIM:(j * group + g + 1) * HEAD_DIM] * ATTN_SCALE).astype(BF16)
             for g in range(group)], axis=0)
        s_t = lax.dot_general(ck_ref[0, j].astype(BF16), q, (((1,), (1,)), ((), ())), preferred_element_type=F32)
        s_t = jnp.where(valid, s_t, -jnp.inf)
        m = jnp.max(s_t, axis=0, keepdims=True)
        m = jnp.where(m == -jnp.inf, 0.0, m)
        e = jnp.exp(s_t - m)
        p_t = e / jnp.maximum(jnp.sum(e, axis=0, keepdims=True), 1e-30)
        cv = cv_ref[0, j].astype(BF16)
        p_sum = jnp.zeros((CMP_PAD, tq), F32)
        for g in range(group):
            h = j * group + g
            p_g = p_t[:, g * tq:(g + 1) * tq]
            p_sum = p_sum + p_g
            o = jnp.dot(p_g.T.astype(BF16), cv, preferred_element_type=F32)
            o_ref[0, :, h * HEAD_DIM:(h + 1) * HEAD_DIM] = gate_ref[0, :, h:h + 1] * o
        imp = jnp.dot(fold_ref[...], p_sum, preferred_element_type=F32, precision=lax.Precision.HIGHEST)
        sc = jnp.where(cand, imp, -jnp.inf)
        chosen = forced
        for _ in range(N_SEL - N_FORCED):
            best, bi = _first_argmax(sc, blk, n_slc_pad)
            hit = blk == bi
            chosen = chosen | (hit & (best > -jnp.inf))
            sc = jnp.where(hit, -jnp.inf, sc)
        sel_t = jnp.concatenate([jnp.where(chosen, 1.0, 0.0), jnp.zeros((CMP_PAD - n_slc_pad, tq), F32)], axis=0)
        sel_ref[0, j] = sel_t.T.astype(BF16)


def cmp_select_prompt(y_main, comp, gates, tq=LANES_V7X):
    b, t = y_main.shape[:2]
    n_cmp = comp.shape[1]
    n_slc = -(-t // SEL_BLOCK)
    n_slc_pad = -(-n_slc // 8) * 8
    grp = N_Q_C // N_KV_C
    ckv = jnp.pad(comp, ((0, 0), (0, CMP_PAD - n_cmp), (0, 0), (0, 0), (0, 0))).transpose(2, 0, 3, 1, 4)
    ratio, left = SEL_BLOCK // CMP_STRIDE, CMP_LEN // CMP_STRIDE - 1
    mi, bi = jnp.arange(CMP_PAD)[None, :], jnp.arange(n_slc_pad)[:, None]
    fold = ((mi >= ratio * bi - left) & (mi <= ratio * bi + ratio - 1) & (mi < n_cmp) & (bi < n_slc)).astype(F32)
    return pl.pallas_call(
        functools.partial(_cmp_select_kernel, tq=tq, n_cmp=n_cmp, n_kv=N_KV_C, group=grp),
        grid=(b, t // tq),
        in_specs=[pl.BlockSpec((1, tq, C_Q_WIDTH), lambda bi_, i: (bi_, i, 0)),
                  pl.BlockSpec((1, N_KV_C, CMP_PAD, HEAD_DIM), lambda bi_, i: (bi_, 0, 0, 0)),
                  pl.BlockSpec((1, N_KV_C, CMP_PAD, HEAD_DIM), lambda bi_, i: (bi_, 0, 0, 0)),
                  pl.BlockSpec(fold.shape, lambda bi_, i: (0, 0)),
                  pl.BlockSpec((1, tq, gates.shape[2]), lambda bi_, i: (bi_, i, 0))],
        out_specs=[pl.BlockSpec((1, tq, C_Q_WIDTH), lambda bi_, i: (bi_, i, 0)),
                   pl.BlockSpec((1, N_KV_C, tq, CMP_PAD), lambda bi_, i: (bi_, 0, i, 0))],
        out_shape=[jax.ShapeDtypeStruct((b, t, C_Q_WIDTH), F32),
                   jax.ShapeDtypeStruct((b, N_KV_C, t, CMP_PAD), BF16)],
        compiler_params=_cparams("parallel", "parallel"),
        name="cmp_select",
    )(y_main, ckv[0], ckv[1], fold, gates)


def _sel_kernel(q_ref, k_ref, v_ref, sel_ref, gate_ref, o_ref, *, tq, tk, n_kv, group, n_blocks, gate_col):
    i = pl.program_id(1)
    n_chunks = (i * tq + tq + tk - 1) // tk
    qpos = i * tq + lax.broadcasted_iota(jnp.int32, (tq, tk), 0)
    qs = [jnp.concatenate(
        [(q_ref[0, :, (j * group + g) * HEAD_DIM:(j * group + g + 1) * HEAD_DIM] * ATTN_SCALE).astype(BF16)
         for g in range(group)], axis=0) for j in range(n_kv)]

    def body(c, carry):
        k0 = pl.multiple_of(c * tk, tk)
        causal = (k0 + lax.broadcasted_iota(jnp.int32, (tq, tk), 1)) <= qpos
        blk_of_key = lax.shift_right_logical(k0 + lax.broadcasted_iota(jnp.int32, (n_blocks, tk), 1),
                                             SEL_BLOCK.bit_length() - 1)
        expand = (blk_of_key == lax.broadcasted_iota(jnp.int32, (n_blocks, tk), 0)).astype(BF16)
        out = []
        for j in range(n_kv):
            m, l, acc = carry[j]
            ks = slice(j * HEAD_DIM, (j + 1) * HEAD_DIM)
            k = k_ref[0, pl.ds(k0, tk), ks].astype(BF16)
            v = v_ref[0, pl.ds(k0, tk), ks].astype(BF16)
            mask = (jnp.dot(sel_ref[0, j], expand, preferred_element_type=F32) > 0.5) & causal
            s = lax.dot_general(qs[j], k, (((1,), (1,)), ((), ())), preferred_element_type=F32)
            s = jnp.where(mask[None], s.reshape(group, tq, tk), -jnp.inf).reshape(group * tq, tk)
            m_new = jnp.maximum(m, jnp.max(s, axis=-1, keepdims=True))
            p = jnp.exp(s - m_new)
            alpha = jnp.exp(m - m_new)
            l = alpha * l + jnp.sum(p, axis=-1, keepdims=True)
            acc = alpha * acc + jnp.dot(p.astype(BF16), v, preferred_element_type=F32)
            out.append((m_new, l, acc))
        return tuple(out)

    init = tuple((jnp.full((group * tq, 1), NEG_BIG, F32), jnp.zeros((group * tq, 1), F32),
                  jnp.zeros((group * tq, HEAD_DIM), F32)) for _ in range(n_kv))
    final = lax.fori_loop(0, n_chunks, body, init)
    for j in range(n_kv):
        _, l, acc = final[j]
        o = acc / l
        for g in range(group):
            h = j * group + g
            o_ref[0, :, h * HEAD_DIM:(h + 1) * HEAD_DIM] = (
                gate_ref[0, :, gate_col + h:gate_col + h + 1] * o[g * tq:(g + 1) * tq])


def selected_attention_prompt(q_rope, y_main, selmask, gates, *, k_idx, v_idx, gate_col):
    b, t, qw = q_rope.shape
    n_blocks = selmask.shape[-1]
    kw = N_KV_C * HEAD_DIM
    tq, tk = SEL_Q_TILE, SEL_K_TILE
    return pl.pallas_call(
        functools.partial(_sel_kernel, tq=tq, tk=tk, n_kv=N_KV_C, group=N_Q_C // N_KV_C,
                          n_blocks=n_blocks, gate_col=gate_col),
        grid=(b, t // tq),
        in_specs=[pl.BlockSpec((1, tq, qw), lambda bi, i: (bi, i, 0)),
                  pl.BlockSpec((1, t, kw), lambda bi, i: (bi, 0, k_idx)),
                  pl.BlockSpec((1, t, kw), lambda bi, i: (bi, 0, v_idx)),
                  pl.BlockSpec((1, N_KV_C, tq, n_blocks), lambda bi, i: (bi, 0, i, 0)),
                  pl.BlockSpec((1, tq, gates.shape[2]), lambda bi, i: (bi, i, 0))],
        out_specs=pl.BlockSpec((1, tq, qw), lambda bi, i: (bi, i, 0)),
        out_shape=jax.ShapeDtypeStruct((b, t, qw), F32),
        compiler_params=_cparams("parallel", "arbitrary"),
        name="selected_attention",
    )(q_rope, y_main, y_main, selmask, gates)


def _sel_sample_kernel(page_ref, q_ref, qbd_ref, knew_ref, vnew_ref, gate_ref, want_ref, *refs, n_sel):
    del page_ref
    n_q, grp, n_past = N_Q_C, N_Q_C // N_KV_C, n_sel - 1
    pages, o_ref = refs[:n_past * N_KV_C], refs[n_past * N_KV_C]

    def stacked(c):
        return jnp.concatenate(
            [jnp.concatenate([pages[s * N_KV_C + j][0, c, 0] for j in range(N_KV_C)], axis=0) for s in range(n_past)],
            axis=1).astype(BF16)

    width = n_past * PAGE_SIZE
    page_shift = PAGE_SIZE.bit_length() - 1
    sc = jnp.dot((qbd_ref[0] * ATTN_SCALE).astype(BF16), stacked(0), preferred_element_type=F32)
    lane = lax.broadcasted_iota(jnp.int32, (n_q, width), 1)
    key_half = lax.shift_right_logical(lane, SEL_BLOCK.bit_length() - 1) & (PAGE_SIZE // SEL_BLOCK - 1)
    slot_of_lane = lax.shift_right_logical(lax.broadcasted_iota(jnp.int32, (n_sel, width), 1), page_shift) + 1
    expand = (slot_of_lane == lax.broadcasted_iota(jnp.int32, (n_sel, width), 0)).astype(BF16)
    want = jnp.dot(want_ref[0].astype(BF16), expand, preferred_element_type=F32)
    mask = key_half.astype(F32) == want
    s_new = jnp.sum(q_ref[0] * ATTN_SCALE * knew_ref[0], axis=-1, keepdims=True)
    sc = jnp.where(mask, sc, NEG_BIG)
    m = jnp.maximum(jnp.max(sc, axis=-1, keepdims=True), s_new)
    p = jnp.where(mask, jnp.exp(sc - m), 0.0)
    p_new = jnp.exp(s_new - m)
    den = jnp.sum(p, axis=-1, keepdims=True) + p_new
    pv = lax.dot_general(p.astype(BF16), stacked(1), (((1,), (1,)), ((), ())), preferred_element_type=F32)
    head_of_row = lax.shift_right_logical(lax.broadcasted_iota(jnp.int32, (n_q, HEAD_DIM), 0), grp.bit_length() - 1)
    o = jnp.zeros((n_q, HEAD_DIM), F32)
    for j in range(N_KV_C):
        o = jnp.where(head_of_row == j, pv[:, j * HEAD_DIM:(j + 1) * HEAD_DIM], o)
    o_ref[0] = gate_ref[0] * ((o + p_new * vnew_ref[0]) / den)


def selected_attention_sample(q, k_new, v_new, gate, pool, page, want):
    n, n_q, d = q.shape
    grp = n_q // N_KV_C
    n_sel = want.shape[2]
    pool_t = pool.transpose(0, 2, 3, 4, 1)
    eye = jnp.repeat(jnp.eye(N_KV_C, dtype=q.dtype), grp, axis=0)
    q_bd = (q[:, :, None, :] * eye[None, :, :, None]).reshape(n, n_q, N_KV_C * d)
    row_spec = lambda w: pl.BlockSpec((1, n_q, w), lambda bi, pg: (bi, 0, 0))

    def page_spec(s, j):
        return pl.BlockSpec((1, 2, 1, d, PAGE_SIZE), lambda bi, pg: (pg[(bi * N_KV_C + j) * n_sel + s], 0, j, 0, 0))

    page_specs = [page_spec(s, j) for s in range(1, n_sel) for j in range(N_KV_C)]
    grid_spec = pltpu.PrefetchScalarGridSpec(
        num_scalar_prefetch=1,
        grid=(n,),
        in_specs=[row_spec(d), row_spec(N_KV_C * d), row_spec(d), row_spec(d), row_spec(1), row_spec(n_sel)] + page_specs,
        out_specs=row_spec(d),
    )
    return pl.pallas_call(
        functools.partial(_sel_sample_kernel, n_sel=n_sel),
        grid_spec=grid_spec,
        out_shape=jax.ShapeDtypeStruct((n, n_q, d), F32),
        compiler_params=_cparams("parallel"),
        name="selected_attention_sample",
    )(page, q, q_bd, k_new, v_new, gate, want, *([pool_t] * len(page_specs)))


def _half_proj_kernel(*refs, n_rows):
    n_pairs = N_KV_C // 2
    x_refs, w_ref, o_ref = refs[:2 * n_pairs], refs[2 * n_pairs], refs[2 * n_pairs + 1]
    out_w = 2 * 2 * CMP_HID
    for c in range(2):
        for p in range(n_pairs):
            x_ref = x_refs[c * n_pairs + p]
            acc = jnp.zeros((n_rows, out_w), F32)
            for sp in range(CMP_STRIDE // 2):
                lo = x_ref[0, pl.ds(2 * sp, n_rows, stride=CMP_STRIDE), :]
                hi = x_ref[0, pl.ds(2 * sp + 1, n_rows, stride=CMP_STRIDE), :]
                lhs = jnp.concatenate([lo, hi], axis=1).astype(BF16)
                acc = acc + jnp.dot(lhs, w_ref[c, sp], preferred_element_type=F32)
            o_ref[0, :, (c * n_pairs + p) * out_w:(c * n_pairs + p + 1) * out_w] = acc


def half_block_proj(rows_arr, col_idx, w_bd, tile_rows):
    b, l = rows_arr.shape[:2]
    n_half_tile = tile_rows // CMP_STRIDE
    out_w = 2 * 2 * N_KV_C * CMP_HID
    n_chunks = C_KV_WIDTH // LANES_V7X

    def chunk_spec(cp):
        return pl.BlockSpec((1, tile_rows, LANES_V7X), lambda bi, i: (bi, i, col_idx * n_chunks + cp))

    return pl.pallas_call(
        functools.partial(_half_proj_kernel, n_rows=n_half_tile),
        grid=(b, l // tile_rows),
        in_specs=[chunk_spec(cp) for cp in range(n_chunks)] + [pl.BlockSpec(w_bd.shape, lambda bi, i: (0, 0, 0, 0))],
        out_specs=pl.BlockSpec((1, n_half_tile, out_w), lambda bi, i: (bi, i, 0)),
        out_shape=jax.ShapeDtypeStruct((b, l // CMP_STRIDE, out_w), F32),
        compiler_params=_cparams("parallel", "parallel"),
        name="half_block_proj",
    )(*([rows_arr] * n_chunks), w_bd)


def _paged_half_proj_kernel(pt_ref, *refs, n_pages):
    del pt_ref
    n_pairs = N_KV_C // 2
    pages, w_ref, o_ref = refs[:n_pages], refs[n_pages], refs[n_pages + 1]
    rows = refs[n_pages + 2:]
    for j, page in enumerate(pages):
        for c in range(2):
            for p in range(n_pairs):
                x_t = page[0, c, 2 * p:2 * p + 2].reshape(2 * HEAD_DIM, PAGE_SIZE)
                rows[c * n_pairs + p][j * PAGE_SIZE:(j + 1) * PAGE_SIZE, :] = x_t.T
    n_rows = n_pages * PAGE_SIZE // CMP_STRIDE
    out_w = 2 * 2 * CMP_HID
    for c in range(2):
        for p in range(n_pairs):
            x_ref = rows[c * n_pairs + p]
            acc = jnp.zeros((n_rows, out_w), F32)
            for sp in range(CMP_STRIDE // 2):
                lo = x_ref[pl.ds(2 * sp, n_rows, stride=CMP_STRIDE), :]
                hi = x_ref[pl.ds(2 * sp + 1, n_rows, stride=CMP_STRIDE), :]
                lhs = jnp.concatenate([lo, hi], axis=1).astype(BF16)
                acc = acc + jnp.dot(lhs, w_ref[c, sp], preferred_element_type=F32)
            o_ref[0, :, (c * n_pairs + p) * out_w:(c * n_pairs + p + 1) * out_w] = acc


def paged_half_block_proj(pool, page_table, w_bd, n_pages):
    n, pages_per_req = page_table.shape
    pool_t = pool.transpose(0, 2, 3, 4, 1)
    steps = pages_per_req // n_pages
    n_half = n_pages * PAGE_SIZE // CMP_STRIDE
    out_w = 2 * 2 * N_KV_C * CMP_HID

    def page_spec(j):
        return pl.BlockSpec((1, 2, N_KV_C, HEAD_DIM, PAGE_SIZE),
                            lambda bi, i, pt: (pt[bi * pages_per_req + i * n_pages + j], 0, 0, 0, 0))

    grid_spec = pltpu.PrefetchScalarGridSpec(
        num_scalar_prefetch=1,
        grid=(n, steps),
        in_specs=[page_spec(j) for j in range(n_pages)] + [pl.BlockSpec(w_bd.shape, lambda bi, i, pt: (0, 0, 0, 0))],
        out_specs=pl.BlockSpec((1, n_half, out_w), lambda bi, i, pt: (bi, i, 0)),
        scratch_shapes=[pltpu.VMEM((n_pages * PAGE_SIZE, LANES_V7X), F32) for _ in range(C_KV_WIDTH // LANES_V7X)],
    )
    return pl.pallas_call(
        functools.partial(_paged_half_proj_kernel, n_pages=n_pages),
        grid_spec=grid_spec,
        out_shape=jax.ShapeDtypeStruct((n, pages_per_req * PAGE_SIZE // CMP_STRIDE, out_w), F32),
        compiler_params=_cparams("parallel", "arbitrary"),
        name="paged_half_block_proj",
    )(page_table.reshape(-1), *([pool_t] * n_pages), w_bd)


def _half_proj_weight(w1h):
    r = CMP_LEN // CMP_STRIDE
    eye = jnp.eye(2, dtype=F32)
    w = w1h.reshape(2, r, CMP_STRIDE // 2, 2, HEAD_DIM, CMP_HID)
    w = jnp.einsum('cjpldh,kq->cplkdqjh', w, eye)
    return w.reshape(2, CMP_STRIDE // 2, 4 * HEAD_DIM, 2 * r * CMP_HID).astype(BF16)


def _first_argmax(v, iota, size):
    m = jnp.max(v, axis=0, keepdims=True)
    return m, jnp.min(jnp.where(v == m, iota, size), axis=0, keepdims=True)


def _route_tokens(scores, bias):
    tm = scores.shape[1]
    per = N_EXPERTS // N_EXPERT_GROUPS
    biased = scores + bias
    iota_per = lax.broadcasted_iota(jnp.int32, (per, tm), 0)
    iota_grp = lax.broadcasted_iota(jnp.int32, (N_EXPERT_GROUPS, tm), 0)
    grp = jnp.zeros((N_EXPERT_GROUPS, tm), F32)
    slabs = [biased[g * per:(g + 1) * per] for g in range(N_EXPERT_GROUPS)]
    for g, v in enumerate(slabs):
        m1, i1 = _first_argmax(v, iota_per, per)
        m2 = jnp.max(jnp.where(iota_per == i1, -jnp.inf, v), axis=0, keepdims=True)
        grp = jnp.where(iota_grp == g, m1 + m2, grp)
    keep = jnp.zeros((N_EXPERT_GROUPS, tm), jnp.int32)
    for _ in range(TOPK_GROUPS):
        _, gi = _first_argmax(grp, iota_grp, N_EXPERT_GROUPS)
        keep = jnp.where(iota_grp == gi, 1, keep)
        grp = jnp.where(iota_grp == gi, -jnp.inf, grp)
    v = jnp.concatenate([jnp.where(keep[g:g + 1] > 0, slabs[g], -jnp.inf) for g in range(N_EXPERT_GROUPS)], axis=0)
    iota_e = lax.broadcasted_iota(jnp.int32, (N_EXPERTS, tm), 0)
    iota_k = lax.broadcasted_iota(jnp.int32, (TOP_K, tm), 0)
    eidx = jnp.zeros((TOP_K, tm), jnp.int32)
    ew = jnp.zeros((TOP_K, tm), F32)
    for k in range(TOP_K):
        _, ei = _first_argmax(v, iota_e, N_EXPERTS)
        hit = iota_e == ei
        eidx = jnp.where(iota_k == k, ei, eidx)
        ew = jnp.where(iota_k == k, jnp.sum(jnp.where(hit, scores, 0.0), axis=0, keepdims=True), ew)
        v = jnp.where(hit, -jnp.inf, v)
    return eidx, ew / jnp.sum(ew, axis=0, keepdims=True) * ROUTED_SCALE


def _moe_in_kernel(x_ref, g_ref, sh_ref, sc_ref, rwt_ref, rb_ref, sg_ref, su_ref, sd_ref,
                   h_ref, eidx_ref, ew_ref, shared_ref):
    h = _modulated_norm(x_ref[0], g_ref[...], sc_ref[0], sh_ref[0])
    logits_t = lax.dot_general(rwt_ref[...], h, (((1,), (1,)), ((), ())), preferred_element_type=F32,
                               precision=lax.Precision.HIGHEST)
    eidx, ew = _route_tokens(jax.nn.sigmoid(logits_t), rb_ref[...])
    eidx_ref[...] = eidx
    ew_ref[...] = ew
    h_ref[0] = h
    hb = h.astype(BF16)
    gate = jnp.dot(hb, sg_ref[...], preferred_element_type=F32)
    up = jnp.dot(hb, su_ref[...], preferred_element_type=F32)
    mid = (jax.nn.silu(gate) * up).astype(BF16)
    shared_ref[0] = jnp.dot(mid, sd_ref[...], preferred_element_type=F32)


def moe_in(x, g, shift, scale, router_w, router_b, sg_bf, su_bf, sd_bf, tm):
    b, t, k = x.shape
    per_row = shift.shape[1] != 1
    mod_spec = (pl.BlockSpec((1, tm, k), lambda bi, i: (bi, i, 0)) if per_row
                else pl.BlockSpec((1, 1, k), lambda bi, i: (bi, 0, 0)))
    row_spec = lambda wd: pl.BlockSpec((1, tm, wd), lambda bi, i: (bi, i, 0))
    full = lambda a: pl.BlockSpec(a.shape, lambda bi, i: (0,) * a.ndim)
    tiles = t // tm
    tok_spec = pl.BlockSpec((TOP_K, tm), lambda bi, i: (0, bi * tiles + i))
    rwt = router_w.T
    rb = router_b.astype(F32).reshape(N_EXPERTS, 1)
    return pl.pallas_call(
        _moe_in_kernel,
        grid=(b, tiles),
        in_specs=[row_spec(k), pl.BlockSpec((1, k), lambda bi, i: (0, 0)), mod_spec, mod_spec,
                  full(rwt), full(rb), full(sg_bf), full(su_bf), full(sd_bf)],
        out_specs=[row_spec(k), tok_spec, tok_spec, row_spec(k)],
        out_shape=[jax.ShapeDtypeStruct((b, t, k), F32), jax.ShapeDtypeStruct((TOP_K, b * t), jnp.int32),
                   jax.ShapeDtypeStruct((TOP_K, b * t), F32), jax.ShapeDtypeStruct((b, t, k), F32)],
        compiler_params=_cparams("parallel", "parallel"),
        name="moe_in",
    )(x, g.reshape(1, k), shift, scale, rwt, rb, sg_bf, su_bf, sd_bf)


def _gmm_kernel(blk_e_ref, n_used_ref, x_ref, wg_ref, wu_ref, wd_ref, o_ref, wg_s, wu_s, wd_s):
    blk = pl.program_id(0)
    prev_e = blk_e_ref[jnp.maximum(blk - 1, 0)]
    new_expert = (blk == 0) | (blk_e_ref[blk] != prev_e)

    @pl.when(new_expert)
    def _():
        wg_s[...] = wg_ref[0].astype(BF16)
        wu_s[...] = wu_ref[0].astype(BF16)
        wd_s[...] = wd_ref[0].astype(BF16)

    @pl.when(blk < n_used_ref[0])
    def _():
        x = x_ref[...].astype(BF16)
        gate = jnp.dot(x, wg_s[...], preferred_element_type=F32)
        up = jnp.dot(x, wu_s[...], preferred_element_type=F32)
        mid = (jax.nn.silu(gate) * up).astype(BF16)
        o_ref[...] = jnp.dot(mid, wd_s[...], preferred_element_type=F32)

    @pl.when(blk >= n_used_ref[0])
    def _():
        o_ref[...] = jnp.zeros_like(o_ref)


def grouped_experts(xs, blk_e, n_used, w_gate, w_up, w_down, layer):
    cap, k = xs.shape
    tm = MOE_TILE
    n_blk = cap // tm
    de = w_gate.shape[-1]
    grid_spec = pltpu.PrefetchScalarGridSpec(
        num_scalar_prefetch=2,
        grid=(n_blk,),
        in_specs=[pl.BlockSpec((tm, k), lambda i, be, nu: (i, 0)),
                  pl.BlockSpec((None, 1, k, de), lambda i, be, nu: (layer, be[i], 0, 0)),
                  pl.BlockSpec((None, 1, k, de), lambda i, be, nu: (layer, be[i], 0, 0)),
                  pl.BlockSpec((None, 1, de, k), lambda i, be, nu: (layer, be[i], 0, 0))],
        out_specs=pl.BlockSpec((tm, k), lambda i, be, nu: (i, 0)),
        scratch_shapes=[pltpu.VMEM((k, de), BF16), pltpu.VMEM((k, de), BF16), pltpu.VMEM((de, k), BF16)],
    )
    return pl.pallas_call(
        _gmm_kernel,
        grid_spec=grid_spec,
        out_shape=jax.ShapeDtypeStruct((cap, k), F32),
        compiler_params=_cparams("arbitrary"),
        name="grouped_experts",
    )(blk_e, n_used, xs, w_gate, w_up, w_down)


def _moe_out_kernel(x_ref, gate_ref, picked_ref, ew_ref, shared_ref, o_ref):
    routed = picked_ref[0] * ew_ref[:, 0:1]
    for k in range(1, TOP_K):
        routed = routed + picked_ref[k] * ew_ref[:, k:k + 1]
    o_ref[0] = x_ref[0] + gate_ref[0] * (routed + shared_ref[0])


def moe_out(x, gate, picked, ew, shared, tm, row0):
    b, t, d = x.shape
    per_row = gate.shape[1] != 1
    gate_spec = (pl.BlockSpec((1, tm, d), lambda bi, i: (bi, i, 0)) if per_row
                 else pl.BlockSpec((1, 1, d), lambda bi, i: (bi, 0, 0)))
    row_spec = pl.BlockSpec((1, tm, d), lambda bi, i: (bi, i, 0))
    tiles, off = t // tm, row0 // tm
    return pl.pallas_call(
        _moe_out_kernel,
        grid=(b, tiles),
        in_specs=[row_spec, gate_spec,
                  pl.BlockSpec((TOP_K, tm, d), lambda bi, i: (0, off + bi * tiles + i, 0)),
                  pl.BlockSpec((tm, TOP_K), lambda bi, i: (off + bi * tiles + i, 0)),
                  row_spec],
        out_specs=row_spec,
        out_shape=jax.ShapeDtypeStruct((b, t, d), F32),
        compiler_params=_cparams("parallel", "parallel"),
        name="moe_out",
    )(x, gate, picked, ew, shared)


def _final_norm_kernel(x_ref, g_ref, o_ref):
    x = x_ref[0]
    var = jnp.mean(x * x, axis=-1, keepdims=True)
    o_ref[0] = x * lax.rsqrt(var + RMS_EPS) * g_ref[...]


def final_norm(x, g, tm):
    b, t, d = x.shape
    row_spec = pl.BlockSpec((1, tm, d), lambda bi, i: (bi, i, 0))
    return pl.pallas_call(
        _final_norm_kernel,
        grid=(b, t // tm),
        in_specs=[row_spec, pl.BlockSpec((1, d), lambda bi, i: (0, 0))],
        out_specs=row_spec,
        out_shape=jax.ShapeDtypeStruct((b, t, d), F32),
        compiler_params=_cparams("parallel", "parallel"),
        name="final_norm",
    )(x, g.reshape(1, d))


def _ada_kernel(c_ref, w_ref, b_ref, o_ref):
    a = jax.nn.silu(c_ref[...]).astype(BF16)
    o_ref[0] = jnp.dot(a, w_ref[0].astype(BF16), preferred_element_type=F32) + b_ref[0]


def ada_mods(c_all, ada_w, ada_b):
    n, d = c_all.shape
    depth, _, width = ada_w.shape
    tn = 1024
    return pl.pallas_call(
        _ada_kernel,
        grid=(depth, width // tn),
        in_specs=[pl.BlockSpec((n, d), lambda l, j: (0, 0)),
                  pl.BlockSpec((1, d, tn), lambda l, j: (l, 0, j)),
                  pl.BlockSpec((1, 1, tn), lambda l, j: (l, 0, j))],
        out_specs=pl.BlockSpec((1, n, tn), lambda l, j: (l, 0, j)),
        out_shape=jax.ShapeDtypeStruct((depth, n, width), F32),
        compiler_params=_cparams("parallel", "parallel"),
        name="ada_mods",
    )(c_all, ada_w, ada_b.reshape(depth, 1, width))


def _masked_softmax(s, mask, sink=None):
    s = jnp.where(mask, s, -jnp.inf)
    m = jnp.max(s, axis=-1, keepdims=True)
    if sink is not None:
        m = jnp.maximum(m, sink)
    m = jnp.where(jnp.isfinite(m), m, 0.0)
    e = jnp.exp(s - m)
    den = jnp.sum(e, axis=-1, keepdims=True)
    if sink is not None:
        den = den + jnp.exp(sink - m)
    p = e / jnp.maximum(den, 1e-30)
    return p, (m + jnp.log(den))[..., 0]


def _cmp_to_sel(p, n_slc):
    ratio = SEL_BLOCK // CMP_STRIDE
    left = CMP_LEN // CMP_STRIDE - 1
    pad = [(0, 0)] * (p.ndim - 1) + [(left, ratio * n_slc - p.shape[-1])]
    pp = jnp.pad(p, pad)
    out = pp[..., 0:ratio * n_slc:ratio]
    for o in range(1, ratio + left):
        out = out + pp[..., o:o + ratio * n_slc:ratio]
    return out


def _select_blocks(imp, pos, n_slc):
    k_top = N_SEL - N_FORCED
    cur = pos // SEL_BLOCK
    j = jnp.arange(n_slc)
    cand = (j[None, :] >= 1) & (j[None, :] <= cur[:, None] - 2)
    sc = jnp.where(cand, imp, -jnp.inf)
    if n_slc < k_top:
        sc = jnp.pad(sc, ((0, 0), (0, 0), (0, 0), (0, k_top - n_slc)), constant_values=-jnp.inf)
    lane = jnp.arange(sc.shape[-1])
    vals, idx = [], []
    for _ in range(k_top):
        best = jnp.argmax(sc, axis=-1)
        vals.append(jnp.max(sc, axis=-1))
        idx.append(best)
        sc = jnp.where(lane == best[..., None], -jnp.inf, sc)
    vals, idx = jnp.stack(vals, axis=-1), jnp.stack(idx, axis=-1)
    forced = jnp.stack([cur, cur - 1, jnp.zeros_like(cur)], axis=-1)
    forced_ok = jnp.stack([cur >= 0, cur >= 1, cur >= 2], axis=-1)
    lead = imp.shape[:2]
    blocks = jnp.concatenate([jnp.broadcast_to(forced, lead + forced.shape), idx.astype(cur.dtype)], axis=-1)
    ok = jnp.concatenate([jnp.broadcast_to(forced_ok, lead + forced_ok.shape), vals > -jnp.inf], axis=-1)
    return jnp.clip(blocks, 0, n_slc - 1), ok


def _compress(hp, n_cmp, w1h, b1, w2, pe):
    b = hp.shape[0]
    r = CMP_LEN // CMP_STRIDE
    hp = hp.reshape(b, hp.shape[1], 2, N_KV_C, r, CMP_HID)
    pre = (jnp.einsum('cjsd,cjsdh->ch', pe.reshape(2, r, CMP_STRIDE, HEAD_DIM), w1h) + b1)[:, None, :]
    for j in range(r):
        pre = pre + hp[:, j:j + n_cmp, :, :, j]
    return jnp.einsum('bnckh,chd->bnckd', jax.nn.gelu(pre), w2)


def _compressed_attention(q_cmp, comp, pos):
    blk_end = jnp.arange(comp.shape[1]) * CMP_STRIDE + CMP_LEN - 1
    s = jnp.einsum('bqhgd,bnhd->bhgqn', q_cmp, comp[:, :, 0], preferred_element_type=F32) * ATTN_SCALE
    p_cmp, _ = _masked_softmax(s, blk_end[None, :] <= pos[:, None])
    o_cmp = jnp.einsum('bhgqn,bnhd->bqhgd', p_cmp.astype(comp.dtype), comp[:, :, 1])
    return o_cmp, p_cmp


A_SPECS = ((0, (N_Q_A + N_KV_A) * HEAD_DIM, 'rope', 0, 0),
           ((N_Q_A + N_KV_A) * HEAD_DIM, (N_Q_A + 2 * N_KV_A) * HEAD_DIM, 'raw', 0, (N_Q_A + N_KV_A) * HEAD_DIM))
B_SPECS = ((0, 2 * B_WIDTH, 'rope', 0, 0), (2 * B_WIDTH, 3 * B_WIDTH, 'raw', 0, 2 * B_WIDTH))
_C0 = C_Q_WIDTH + C_KV_WIDTH
_C1 = _C0 + C_KV_WIDTH
C_SPECS = ((0, _C0, 'raw', 0, 0),
           (_C0, _C0 + C_KV_WIDTH // 2, 'rope', 0, _C0), (_C0 + C_KV_WIDTH // 2, _C1, 'raw', 0, _C0 + C_KV_WIDTH // 2),
           (_C1, _C1 + C_KV_WIDTH // 2, 'rope', 0, _C1), (_C1 + C_KV_WIDTH // 2, C_MAIN_WIDTH, 'raw', 0, _C1 + C_KV_WIDTH // 2),
           (0, C_Q_WIDTH, 'rope', 1, 0),
           (C_MAIN_WIDTH, C_MAIN_WIDTH + N_GATES_C, 'sigmoid', 2, 0))
_AK = N_Q_A * HEAD_DIM
_AW = N_KV_A * HEAD_DIM
A_SPECS_P = A_SPECS + ((_AK, _AK + _AW, 'rope', 1, 0), (_AK + _AW, _AK + 2 * _AW, 'raw', 1, _AW))
_BW = B_HEADS_PER_GROUP * HEAD_DIM
B_SPECS_P = B_SPECS + tuple(
    seg for gi in range(N_GROUPS_B) for seg in
    ((B_WIDTH + gi * _BW, B_WIDTH + (gi + 1) * _BW, 'rope', 1 + gi, 0),
     (2 * B_WIDTH + gi * _BW, 2 * B_WIDTH + (gi + 1) * _BW, 'raw', 1 + gi, _BW)))
_CH = C_KV_WIDTH // 2
C_SPECS_P = C_SPECS + ((C_Q_WIDTH, _C0, 'raw', 3, 0),
                       (_C0, _C0 + _CH, 'rope', 4, 0), (_C0 + _CH, _C1, 'raw', 4, _CH),
                       (_C1, _C1 + _CH, 'rope', 5, 0), (_C1 + _CH, C_MAIN_WIDTH, 'raw', 5, _CH))


def _kv_from_cols(cols, n_kv, rows):
    b = cols.shape[0]
    return cols[:, :, cols.shape[2] - rows:].reshape(b, 2, n_kv, HEAD_DIM, rows).transpose(0, 4, 1, 2, 3)


def _mixer_a(x, mods, g, w_qkv_bf, w_o_bf, sink, tabs, tm, past):
    shift, scale, gate = mods
    nq = N_Q_A * HEAD_DIM
    kvw = N_KV_A * HEAD_DIM
    if past is None:
        y, kv_cols = norm_linear(x, g, shift, scale, w_qkv_bf, tabs, A_SPECS_P,
                                 ((nq + 2 * kvw, 'rows'), (2 * kvw, 'cols_last')), tm)
        o = band_attention(y, y, y, dil=1, n_kv=N_KV_A, group=N_Q_A // N_KV_A, tq=WINDOW_A,
                           q_idx=0, k_idx=nq // kvw, v_idx=nq // kvw + 1, sink=sink)
        new = _kv_from_cols(kv_cols, N_KV_A, min(WINDOW_A, y.shape[1]))
    else:
        (y,) = norm_linear(x, g, shift, scale, w_qkv_bf, tabs, A_SPECS, ((nq + 2 * kvw, 'rows'),), tm)
        n = y.shape[1]
        o, new = decode_window_attention(y[0, :, :nq].reshape(n, N_Q_A, HEAD_DIM),
                                         y[0, :, nq:nq + kvw].reshape(n, N_KV_A, HEAD_DIM),
                                         y[0, :, nq + kvw:].reshape(n, N_KV_A, HEAD_DIM), past, sink=sink)
        o = o.reshape(1, n, nq)
    return linear_out([o], w_o_bf, x, gate, 'plain', tm), new


def _mixer_b(x, mods, g, w_qkv_bf, w_o_bf, tabs, tm, past):
    shift, scale, gate = mods
    hpg = B_HEADS_PER_GROUP
    gw = hpg * HEAD_DIM
    outs, lses, news = [], [], []
    if past is None:
        t = x.shape[1]
        descs = [(3 * B_WIDTH, 'rows')] + [(2 * gw, 'cols_last' if win <= tm else 'cols') for win, _ in B_PATTERNS]
        y, *kv_cols = norm_linear(x, g, shift, scale, w_qkv_bf, tabs, B_SPECS_P, descs, tm)
        for gi, (win, dil) in enumerate(B_PATTERNS):
            o, lse = band_attention(y, y, y, dil=dil, n_kv=hpg, group=1, tq=win // dil,
                                    q_idx=gi, k_idx=N_GROUPS_B + gi, v_idx=2 * N_GROUPS_B + gi, want_lse=True)
            outs.append(o)
            lses.append(lse)
            news.append(_kv_from_cols(kv_cols[gi], hpg, min(win, t)))
    else:
        (y,) = norm_linear(x, g, shift, scale, w_qkv_bf, tabs, B_SPECS, ((3 * B_WIDTH, 'rows'),), tm)
        n = y.shape[1]
        for gi, (win, dil) in enumerate(B_PATTERNS):
            part = lambda c: y[0, :, c * B_WIDTH + gi * gw:c * B_WIDTH + (gi + 1) * gw].reshape(n, hpg, HEAD_DIM)
            o, lse, new = decode_window_attention(part(0), part(1), part(2), past[gi], dil=dil, want_lse=True)
            outs.append(o.reshape(1, n, gw))
            lses.append(lse.reshape(1, n, gw))
            news.append(new)
    return linear_out(outs + lses, w_o_bf, x, gate, 'mix3', tm), tuple(news)


def _mixer_c(x, mods, g, w_in_bf, w_o_bf, cmp_w1, cmp_b1, cmp_w2, cmp_pe, tabs, pos, tm, past):
    shift, scale, gate = mods
    grp = N_Q_C // N_KV_C
    kvw = N_KV_C * HEAD_DIM
    descs = [(C_MAIN_WIDTH, 'rows'), (C_Q_WIDTH, 'rows'), (N_GATES_C, 'rows')]
    w1h = cmp_w1.reshape(2, CMP_LEN // CMP_STRIDE, CMP_STRIDE, HEAD_DIM, CMP_HID)
    w_bd = _half_proj_weight(w1h)
    cmp_idx, slc_idx, win_idx = C_Q_WIDTH // C_KV_WIDTH, _C0 // C_KV_WIDTH, _C1 // C_KV_WIDTH
    if past is None:
        descs += [(C_KV_WIDTH, 'cols'), (C_KV_WIDTH, 'cols'), (C_KV_WIDTH, 'cols_last')]
        y, q_rope, gates, cmp_cols, slc_cols, win_cols = norm_linear(x, g, shift, scale, w_in_bf, tabs, C_SPECS_P,
                                                                     descs, tm)
        b, t = y.shape[:2]
        hp = half_block_proj(y, cmp_idx, w_bd, t)
        n_cmp = (t - CMP_LEN) // CMP_STRIDE + 1
        comp = _compress(hp, n_cmp, w1h, cmp_b1, cmp_w2, cmp_pe)
        o_cmp, selmask = cmp_select_prompt(y, comp, gates)
        o_slc = selected_attention_prompt(q_rope, y, selmask, gates, k_idx=2 * slc_idx, v_idx=2 * slc_idx + 1,
                                          gate_col=N_Q_C)
        o_win = band_attention(q_rope, y, y, dil=1, n_kv=N_KV_C, group=grp, tq=WINDOW_C, q_idx=0,
                               k_idx=2 * win_idx, v_idx=2 * win_idx + 1, gate=gates, gate_col=2 * N_Q_C)
        new_win = _kv_from_cols(win_cols, N_KV_C, min(WINDOW_C, t))
        cmp_kv = _kv_from_cols(cmp_cols, N_KV_C, t)
        slc_kv = _kv_from_cols(slc_cols, N_KV_C, t)
    else:
        y, q_rope, gates = norm_linear(x, g, shift, scale, w_in_bf, tabs, C_SPECS, descs, tm)
        win_buf, cmp_pool, slc_pool, page_table = past
        n = y.shape[1]
        t = 1
        cmp_kv = y[0, :, C_Q_WIDTH:_C0].reshape(n, t, 2, N_KV_C, HEAD_DIM)
        slc_kv = y[0, :, _C0:_C1].reshape(n, t, 2, N_KV_C, HEAD_DIM)
        win_kv = y[0, :, _C1:].reshape(n, t, 2, N_KV_C, HEAD_DIM)
        q_cmp = y[0, :, :C_Q_WIDTH].reshape(n, t, N_KV_C, grp, HEAD_DIM)
        gview = gates[0].reshape(n, t, 3, N_KV_C, grp, 1)
        hp = paged_half_block_proj(cmp_pool, page_table, w_bd, CMP_PAGES_PER_STEP)
        length = PAST_LEN + t
        n_cmp = (length - CMP_LEN) // CMP_STRIDE + 1
        comp = _compress(hp, n_cmp, w1h, cmp_b1, cmp_w2, cmp_pe)
        o_cmp, p_cmp = _compressed_attention(q_cmp, comp, pos)
        n_slc = -(-length // SEL_BLOCK)
        blocks, ok = _select_blocks(_cmp_to_sel(p_cmp.sum(axis=2), n_slc), pos, n_slc)
        sub = PAGE_SIZE // SEL_BLOCK
        pb = jnp.minimum(blocks[:, :, 0], PAST_LEN // SEL_BLOCK - 1)
        page = jnp.take_along_axis(jnp.broadcast_to(page_table[:, None], (n, N_KV_C, page_table.shape[1])),
                                   pb // sub, axis=2)
        want = jnp.where(ok[:, :, 0], pb % sub, -1).astype(F32)
        o_slc = selected_attention_sample(
            q_rope[0].reshape(n, N_Q_C, HEAD_DIM), jnp.repeat(slc_kv[:, 0, 0], grp, axis=1),
            jnp.repeat(slc_kv[:, 0, 1], grp, axis=1), gates[0, :, N_Q_C:2 * N_Q_C].reshape(n, N_Q_C, 1),
            slc_pool, page.reshape(-1).astype(jnp.int32), jnp.repeat(want, grp, axis=1))
        o_slc = o_slc.reshape(1, n, C_Q_WIDTH)
        o_win, new_win = decode_window_attention(
            q_rope[0].reshape(n, N_Q_C, HEAD_DIM), win_kv[:, 0, 0], win_kv[:, 0, 1], win_buf,
            gate=gates[0, :, 2 * N_Q_C:].reshape(n, N_Q_C, 1))
        o_win = o_win.reshape(1, n, C_Q_WIDTH)
        o_cmp = (gview[:, :, 0] * o_cmp.astype(F32)).reshape(1, n, C_Q_WIDTH)
    return linear_out([o_cmp, o_slc, o_win], w_o_bf, x, gate, 'sum3', tm), (new_win, cmp_kv, slc_kv)


def _dispatch_plan(eidx, n_blk):
    n = eidx.shape[0]
    tm = MOE_TILE
    chunk = LANES_V7X
    onehot = (eidx[:, :, None] == jnp.arange(N_EXPERTS)).astype(jnp.int32)
    sel = onehot.sum(axis=1).astype(F32).reshape(n // chunk, chunk, N_EXPERTS)
    tril = jnp.tril(jnp.ones((chunk, chunk), F32))
    within = jnp.einsum('ij,cjk->cik', tril, sel)
    chunk_tot = within[:, -1, :]
    before = jnp.cumsum(chunk_tot, axis=0) - chunk_tot
    rank = (within - sel + before[:, None, :]).reshape(n, N_EXPERTS).astype(jnp.int32)
    counts = jnp.sum(chunk_tot, axis=0).astype(jnp.int32)
    padded = (counts + tm - 1) // tm * tm
    ends = jnp.cumsum(padded)
    dest = jnp.sum(onehot * (ends - padded + rank)[:, None, :], axis=-1)
    blk_start = jnp.arange(n_blk, dtype=jnp.int32) * tm
    blk_e = jnp.minimum(jnp.sum(ends[None, :] <= blk_start[:, None], axis=1), N_EXPERTS - 1).astype(jnp.int32)
    n_used = (ends[-1] // tm).astype(jnp.int32).reshape(1)
    sorted_tok = (jnp.argsort(eidx.reshape(-1), stable=True) // TOP_K).astype(jnp.int32)
    shift = (ends - padded) - (jnp.cumsum(counts) - counts)
    dense = jnp.arange(n_blk * tm, dtype=jnp.int32) - jnp.repeat(shift[blk_e], tm)
    row_tok = sorted_tok[jnp.clip(dense, 0, n * TOP_K - 1)]
    return dest, row_tok, blk_e, n_used


def _moe(xp, xs, mods_p, mods_s, g, router_w, router_b, w_gate, w_up, w_down, layer, sg_bf, su_bf, sd_bf):
    hp, ep, wp, shp = moe_in(xp, g, mods_p[0], mods_p[1], router_w, router_b, sg_bf, su_bf, sd_bf, ROW_TILE)
    hs, es, ws, shs = moe_in(xs, g, mods_s[0], mods_s[1], router_w, router_b, sg_bf, su_bf, sd_bf, xs.shape[1])
    d = xp.shape[-1]
    n_p = xp.shape[0] * xp.shape[1]
    h_all = jnp.concatenate([hp.reshape(-1, d), hs.reshape(-1, d)], axis=0)
    eidx = jnp.concatenate([ep, es], axis=1).T
    ew = jnp.concatenate([wp, ws], axis=1).T
    n = h_all.shape[0]
    n_blk = (n * TOP_K + N_EXPERTS * (MOE_TILE - 1)) // MOE_TILE + 1
    dest, row_tok, blk_e, n_used = _dispatch_plan(eidx, n_blk)
    ys = grouped_experts(h_all[row_tok], blk_e, n_used, w_gate, w_up, w_down, layer)
    picked = ys[dest.T.reshape(-1)].reshape(TOP_K, n, d)
    xp = moe_out(xp, mods_p[2], picked, ew, shp, MOE_OUT_TILE, 0)
    xs = moe_out(xs, mods_s[2], picked, ew, shs, xs.shape[1], n_p)
    return xp, xs


def kernel(x_prompt, x_sample, c_prompt, c_sample, cache_a_kv, cache_b_kv_w128, cache_b_kv_w512, cache_b_kv_w2048, cache_c_win_kv, cache_c_cmp_kv, cache_c_slc_kv, page_table, norm_g, final_g, ada_w, ada_b, a_w_qkv, a_w_o, a_sink, b_w_qkv, b_w_o, c_w_in, c_w_o, c_cmp_w1, c_cmp_b1, c_cmp_w2, c_cmp_pe, moe_router, moe_bias, moe_w_gate, moe_w_up, moe_w_down, shared_w_gate, shared_w_up, shared_w_down):
    bp, seq, d = x_prompt.shape
    ns = x_sample.shape[0]
    b_caches = (cache_b_kv_w128, cache_b_kv_w512, cache_b_kv_w2048)
    pos_p = jnp.arange(seq, dtype=jnp.int32)
    pos_s = PAST_LEN + jnp.arange(x_sample.shape[1], dtype=jnp.int32)
    tabs_p = _rope_tables(pos_p, seq)
    tabs_s = _rope_tables(pos_s, ns)

    mods = ada_mods(jnp.concatenate([c_prompt, c_sample], axis=0), ada_w, ada_b)
    mods_p = mods[:, :bp].reshape(DEPTH, bp, 6, 1, d)
    mods_s = mods[:, bp:].reshape(DEPTH, 1, ns, 6, d)

    xp = x_prompt
    xs = x_sample.reshape(1, ns, d)
    st_p = {0: [], 1: [], 2: []}
    st_s = {0: [], 1: [], 2: []}
    for l in range(DEPTH):
        kind, slot = LAYER_KIND[l], LAYER_SLOT[l]
        mp = [mods_p[l, :, i] for i in range(6)]
        ms = [mods_s[l, :, :, i] for i in range(6)]
        g_mix, g_moe = norm_g[l, 0], norm_g[l, 1]
        if kind == 0:
            w_in, w_o = a_w_qkv[slot].astype(BF16), a_w_o[slot].astype(BF16)
            xp, sp = _mixer_a(xp, mp[:3], g_mix, w_in, w_o, a_sink[slot], tabs_p, ROW_TILE, None)
            xs, ss = _mixer_a(xs, ms[:3], g_mix, w_in, w_o, a_sink[slot], tabs_s, ns, cache_a_kv[slot])
        elif kind == 1:
            w_in, w_o = b_w_qkv[slot].astype(BF16), b_w_o[slot].astype(BF16)
            xp, sp = _mixer_b(xp, mp[:3], g_mix, w_in, w_o, tabs_p, ROW_TILE, None)
            xs, ss = _mixer_b(xs, ms[:3], g_mix, w_in, w_o, tabs_s, ns, tuple(buf[slot] for buf in b_caches))
        else:
            w_in, w_o = c_w_in[slot].astype(BF16), c_w_o[slot].astype(BF16)
            cargs = (c_cmp_w1[slot], c_cmp_b1[slot], c_cmp_w2[slot], c_cmp_pe[slot])
            xp, sp = _mixer_c(xp, mp[:3], g_mix, w_in, w_o, *cargs, tabs_p, pos_p, ROW_TILE, None)
            xs, ss = _mixer_c(xs, ms[:3], g_mix, w_in, w_o, *cargs, tabs_s, pos_s, ns,
                              (cache_c_win_kv[slot], cache_c_cmp_kv[slot], cache_c_slc_kv[slot], page_table))
        st_p[kind].append(sp)
        st_s[kind].append(ss)
        xp, xs = _moe(xp, xs, mp[3:], ms[3:], g_moe, moe_router[l], moe_bias[l], moe_w_gate, moe_w_up, moe_w_down, l,
                      shared_w_gate[l].astype(BF16), shared_w_up[l].astype(BF16), shared_w_down[l].astype(BF16))
    y_prompt = final_norm(xp, final_g, ROW_TILE)
    y_sample = final_norm(xs, final_g, ns).reshape(x_sample.shape)

    outs = [y_prompt, y_sample, jnp.stack(st_p[0]), jnp.stack(st_s[0])]
    for i in range(N_GROUPS_B):
        outs += [jnp.stack([s[i] for s in st_p[1]]), jnp.stack([s[i] for s in st_s[1]])]
    outs += [jnp.stack([s[0] for s in st_p[2]]), jnp.stack([s[0] for s in st_s[2]])]
    outs += [jnp.stack([s[1] for s in st_p[2]]), jnp.stack([s[1] for s in st_s[2]])]
    outs += [jnp.stack([s[2] for s in st_p[2]]), jnp.stack([s[2] for s in st_s[2]])]
    return tuple(outs)
```

```python
import functools

import jax
import jax.numpy as jnp
from jax import lax
from jax.experimental import pallas as pl
from jax.experimental.pallas import tpu as pltpu

D_MODEL = 1024
DEPTH = 4
PAST_LEN = 8192
PAGE_SIZE = 128
HEAD_DIM = 64
ROT_DIM = HEAD_DIM // 4
ROPE_THETA = 500000.0
ATTN_SCALE = HEAD_DIM ** -0.5
RMS_EPS = 1e-6

N_MIXERS = 3
LAYER_KIND = tuple(i % N_MIXERS for i in range(DEPTH))
LAYER_SLOT = tuple(LAYER_KIND[:i].count(LAYER_KIND[i]) for i in range(DEPTH))

N_Q_A = 16
N_KV_A = 4
WINDOW_A = 128
B_PATTERNS = ((128, 1), (512, 4), (2048, 16))
N_GROUPS_B = len(B_PATTERNS)
B_HEADS_PER_GROUP = 4
B_WIDTH = N_GROUPS_B * B_HEADS_PER_GROUP * HEAD_DIM
N_Q_C = 16
N_KV_C = 4
CMP_LEN = 32
CMP_STRIDE = 16
CMP_HID = 64
SEL_BLOCK = 64
N_SEL = 16
N_FORCED = 3
WINDOW_C = 512
C_KV_WIDTH = 2 * N_KV_C * HEAD_DIM
C_Q_WIDTH = N_Q_C * HEAD_DIM
C_MAIN_WIDTH = C_Q_WIDTH + 3 * C_KV_WIDTH
N_GATES_C = 3 * N_Q_C

N_EXPERTS = 64
TOP_K = 8
N_EXPERT_GROUPS = 8
TOPK_GROUPS = 4
D_EXPERT = 256
ROUTED_SCALE = 2.5

LANES_V7X = 128
VMEM_LIMIT_V7X = 56 * 1024 * 1024

ROW_TILE = 512
COL_GROUP = 512
MOE_TILE = 512
MOE_OUT_TILE = 256
CMP_PAGES_PER_STEP = 32
PROMPT_CHAINS = 2
SEL_Q_TILE = 128
SEL_K_TILE = 256
NEG_BIG = -1e30

BF16 = jnp.bfloat16
F32 = jnp.float32


def _cparams(*sem):
    return pltpu.CompilerParams(dimension_semantics=sem, vmem_limit_bytes=VMEM_LIMIT_V7X)


def _rope_tables(pos, rows):
    half = ROT_DIM // 2
    inv_freq = ROPE_THETA ** (-jnp.arange(half, dtype=F32) / half)
    ang = pos.astype(F32)[:, None] * inv_freq
    cos, sin = jnp.cos(ang), jnp.sin(ang)
    t = pos.shape[0]
    z8 = jnp.zeros((t, half), F32)
    rest1 = jnp.ones((t, HEAD_DIM - ROT_DIM), F32)
    rest0 = jnp.zeros((t, HEAD_DIM - ROT_DIM), F32)
    reps = LANES_V7X // HEAD_DIM
    cos_t = jnp.tile(jnp.concatenate([cos, cos, rest1], axis=1), (1, reps))
    sin_a = jnp.tile(jnp.concatenate([-sin, z8, rest0], axis=1), (1, reps))
    sin_b = jnp.tile(jnp.concatenate([z8, sin, rest0], axis=1), (1, reps))
    return tuple(jnp.broadcast_to(a, (rows, LANES_V7X)) for a in (cos_t, sin_a, sin_b))


def _rope_chunk(blk, cos_t, sin_a, sin_b):
    return (blk * cos_t + pltpu.roll(blk, LANES_V7X - ROT_DIM // 2, 1) * sin_a
            + pltpu.roll(blk, ROT_DIM // 2, 1) * sin_b)


def _build_plan(n_cols, specs):
    plan = []
    for c0 in range(0, n_cols, COL_GROUP):
        width = min(COL_GROUP, n_cols - c0)
        segs = []
        for (s0, s1, mode, oi, d0) in specs:
            lo, hi = max(s0, c0), min(s1, c0 + width)
            if lo >= hi:
                continue
            step = LANES_V7X if mode == 'rope' else hi - lo
            for a in range(lo, hi, step):
                segs.append((a - c0, min(step, hi - a), mode, oi, d0 + a - s0))
        plan.append((c0, width, tuple(segs)))
    return tuple(plan)


def _modulated_norm(x, g, scale, shift):
    var = jnp.mean(x * x, axis=-1, keepdims=True)
    y = x * lax.rsqrt(var + RMS_EPS) * g
    return y * (1 + scale) + shift


def _norm_linear_kernel(x_ref, g_ref, sh_ref, sc_ref, w_ref, cos_ref, sa_ref, sb_ref, *refs, plan, layouts):
    outs, h_ref = refs[:len(layouts)], refs[len(layouts)]
    h_ref[...] = _modulated_norm(x_ref[0], g_ref[...], sc_ref[0], sh_ref[0]).astype(BF16)
    for (c0, width, segs) in plan:
        acc = jnp.dot(h_ref[...], w_ref[:, c0:c0 + width], preferred_element_type=F32)
        for (off, wd, mode, oi, dst) in segs:
            blk = acc[:, off:off + wd]
            if mode == 'rope':
                blk = _rope_chunk(blk, cos_ref[...], sa_ref[...], sb_ref[...])
            elif mode == 'sigmoid':
                blk = jax.nn.sigmoid(blk)
            if layouts[oi] == 'rows':
                outs[oi][0, :, dst:dst + wd] = blk
            else:
                outs[oi][0, dst:dst + wd, :] = blk.T


def norm_linear(x, g, shift, scale, w_bf, tabs, specs, out_descs, tm):
    b, t, k = x.shape
    n = w_bf.shape[1]
    per_row = shift.shape[1] != 1
    mod_spec = (pl.BlockSpec((1, tm, k), lambda bi, i: (bi, i, 0)) if per_row
                else pl.BlockSpec((1, 1, k), lambda bi, i: (bi, 0, 0)))
    tab_spec = pl.BlockSpec((tm, LANES_V7X), lambda bi, i: (i, 0))
    plan = _build_plan(n, specs)
    out_specs, out_shape = [], []
    for wd, layout in out_descs:
        if layout == 'rows':
            out_specs.append(pl.BlockSpec((1, tm, wd), lambda bi, i: (bi, i, 0)))
            out_shape.append(jax.ShapeDtypeStruct((b, t, wd), F32))
        elif layout == 'cols':
            out_specs.append(pl.BlockSpec((1, wd, tm), lambda bi, i: (bi, 0, i)))
            out_shape.append(jax.ShapeDtypeStruct((b, wd, t), F32))
        else:
            out_specs.append(pl.BlockSpec((1, wd, tm), lambda bi, i: (bi, 0, 0)))
            out_shape.append(jax.ShapeDtypeStruct((b, wd, tm), F32))
    outs = pl.pallas_call(
        functools.partial(_norm_linear_kernel, plan=plan, layouts=tuple(l for _, l in out_descs)),
        grid=(b, t // tm),
        in_specs=[pl.BlockSpec((1, tm, k), lambda bi, i: (bi, i, 0)),
                  pl.BlockSpec((1, k), lambda bi, i: (0, 0)),
                  mod_spec, mod_spec,
                  pl.BlockSpec((k, n), lambda bi, i: (0, 0)),
                  tab_spec, tab_spec, tab_spec],
        out_specs=out_specs,
        out_shape=out_shape,
        scratch_shapes=[pltpu.VMEM((tm, k), BF16)],
        compiler_params=_cparams("parallel", "arbitrary"),
        name="norm_linear",
    )(x, g.reshape(1, k), shift, scale, w_bf, *tabs)
    return outs


def _linear_out_kernel(*refs, mode, n_o):
    o_refs = refs[:n_o]
    w_ref, x_ref, gate_ref, out_ref = refs[n_o:]
    if mode == 'plain':
        o = o_refs[0][0]
    elif mode == 'sum3':
        o = (o_refs[0][0] + o_refs[1][0]) + o_refs[2][0]
    else:
        ng = n_o // 2
        lses = [r[0] for r in o_refs[ng:]]
        m = functools.reduce(jnp.maximum, lses)
        es = [jnp.exp(l - m) for l in lses]
        den = functools.reduce(lambda a, c: a + c, es)
        o = jnp.concatenate([(e / den) * r[0] for e, r in zip(es, o_refs[:ng])], axis=1)
    acc = jnp.dot(o.astype(BF16), w_ref[...], preferred_element_type=F32)
    out_ref[0] = x_ref[0] + gate_ref[0] * acc


def linear_out(o_list, w_bf, x, gate, mode, tm):
    b, t, d = x.shape
    per_row = gate.shape[1] != 1
    gate_spec = (pl.BlockSpec((1, tm, d), lambda bi, i: (bi, i, 0)) if per_row
                 else pl.BlockSpec((1, 1, d), lambda bi, i: (bi, 0, 0)))
    row_spec = lambda wd: pl.BlockSpec((1, tm, wd), lambda bi, i: (bi, i, 0))
    return pl.pallas_call(
        functools.partial(_linear_out_kernel, mode=mode, n_o=len(o_list)),
        grid=(b, t // tm),
        in_specs=[row_spec(o.shape[-1]) for o in o_list]
        + [pl.BlockSpec(w_bf.shape, lambda bi, i: (0, 0)), row_spec(d), gate_spec],
        out_specs=row_spec(d),
        out_shape=jax.ShapeDtypeStruct((b, t, d), F32),
        compiler_params=_cparams("parallel", "parallel"),
        name="linear_out",
    )(*o_list, w_bf, x, gate)


def _band_kernel(*refs, n_kv, group, tq, has_sink, has_gate, gate_col, want_lse):
    q_ref, kp_ref, kc_ref, vp_ref, vc_ref = refs[:5]
    pos = 5
    sink_ref = gate_ref = lse_ref = None
    if has_sink:
        sink_ref = refs[pos]
        pos += 1
    if has_gate:
        gate_ref = refs[pos]
        pos += 1
    o_ref = refs[pos]
    if want_lse:
        lse_ref = refs[pos + 1]
    no_prev = jnp.where(pl.program_id(2) == 0, 2 * tq, 0)
    rows = lax.broadcasted_iota(jnp.int32, (tq, 2 * tq), 0)
    cols = lax.broadcasted_iota(jnp.int32, (tq, 2 * tq), 1)
    mask = ((cols < tq) & (cols >= rows + no_prev)) | ((cols >= tq) & ((cols - tq) <= rows))
    for j in range(n_kv):
        ks = slice(j * HEAD_DIM, (j + 1) * HEAD_DIM)
        k = jnp.concatenate([kp_ref[0, :, ks], kc_ref[0, :, ks]], axis=0).astype(BF16)
        v = jnp.concatenate([vp_ref[0, :, ks], vc_ref[0, :, ks]], axis=0).astype(BF16)
        for g in range(group):
            h = j * group + g
            hs = slice(h * HEAD_DIM, (h + 1) * HEAD_DIM)
            q = (q_ref[0, :, hs] * ATTN_SCALE).astype(BF16)
            s = lax.dot_general(q, k, (((1,), (1,)), ((), ())), preferred_element_type=F32)
            s = jnp.where(mask, s, -jnp.inf)
            m = jnp.max(s, axis=-1, keepdims=True)
            if has_sink:
                m = jnp.maximum(m, sink_ref[h])
            e = jnp.exp(s - m)
            den = jnp.sum(e, axis=-1, keepdims=True)
            if has_sink:
                den = den + jnp.exp(sink_ref[h] - m)
            o = jnp.dot(e.astype(BF16), v, preferred_element_type=F32) / den
            if has_gate:
                o = gate_ref[0, :, gate_col + h:gate_col + h + 1] * o
            o_ref[0, :, hs] = o
            if want_lse:
                lse_ref[0, :, hs] = jnp.broadcast_to(m + jnp.log(den), (tq, HEAD_DIM))


def band_attention(qa, ka, va, *, dil, n_kv, group, tq, q_idx, k_idx, v_idx, sink=None, gate=None,
                   gate_col=0, want_lse=False):
    b, s = qa.shape[:2]
    l = s // dil
    qw, kw = n_kv * group * HEAD_DIM, n_kv * HEAD_DIM
    q_rs, k_rs, v_rs = qa.shape[2] // qw, ka.shape[2] // kw, va.shape[2] // kw
    q2, k2, v2 = (a.reshape(b, l, dil * a.shape[2]) for a in (qa, ka, va))
    prev = lambda i: jnp.maximum(i - 1, 0)
    in_specs = [pl.BlockSpec((1, tq, qw), lambda bi, r, i: (bi, i, r * q_rs + q_idx)),
                pl.BlockSpec((1, tq, kw), lambda bi, r, i: (bi, prev(i), r * k_rs + k_idx)),
                pl.BlockSpec((1, tq, kw), lambda bi, r, i: (bi, i, r * k_rs + k_idx)),
                pl.BlockSpec((1, tq, kw), lambda bi, r, i: (bi, prev(i), r * v_rs + v_idx)),
                pl.BlockSpec((1, tq, kw), lambda bi, r, i: (bi, i, r * v_rs + v_idx))]
    args = [q2, k2, k2, v2, v2]
    if sink is not None:
        in_specs.append(pl.BlockSpec(memory_space=pltpu.SMEM))
        args.append(sink.astype(F32))
    if gate is not None:
        in_specs.append(pl.BlockSpec((1, tq, gate.shape[2]), lambda bi, r, i: (bi, i, 0)))
        args.append(gate)
    o_spec = pl.BlockSpec((1, tq, qw), lambda bi, r, i: (bi, i, r))
    o_shape = jax.ShapeDtypeStruct((b, l, dil * qw), F32)
    res = pl.pallas_call(
        functools.partial(_band_kernel, n_kv=n_kv, group=group, tq=tq, has_sink=sink is not None,
                          has_gate=gate is not None, gate_col=gate_col, want_lse=want_lse),
        grid=(b, dil, l // tq),
        in_specs=in_specs,
        out_specs=[o_spec, o_spec] if want_lse else o_spec,
        out_shape=[o_shape, o_shape] if want_lse else o_shape,
        compiler_params=_cparams("parallel", "parallel", "arbitrary"),
        name="band_attention",
    )(*args)
    if want_lse:
        return res[0].reshape(b, s, qw), res[1].reshape(b, s, qw)
    return res.reshape(b, s, qw)


def _decode_kernel(*refs, n_kv, group, width, dil, has_sink, has_gate, want_lse):
    q_ref, knew_ref, vnew_ref, kcol_ref, vcol_ref, cache_ref = refs[:6]
    pos = 6
    sink_ref = gate_ref = lse_ref = None
    if has_sink:
        sink_ref = refs[pos]
        pos += 1
    if has_gate:
        gate_ref = refs[pos]
        pos += 1
    o_ref = refs[pos]
    pos += 1
    if want_lse:
        lse_ref = refs[pos]
        pos += 1
    newc_ref = refs[pos]
    lane_g = lax.broadcasted_iota(jnp.int32, (group, width), 1)
    lane_d = lax.broadcasted_iota(jnp.int32, (HEAD_DIM, width), 1)
    tap = ((width - lane_g) & (dil - 1)) == 0
    for j in range(n_kv):
        rows = slice(j * group, (j + 1) * group)
        k_t, v_t = cache_ref[0, 0, j], cache_ref[0, 1, j]
        q = q_ref[0, rows] * ATTN_SCALE
        sc = jnp.dot(q.astype(BF16), k_t.astype(BF16), preferred_element_type=F32)
        s_new = jnp.sum(q * knew_ref[0, j:j + 1], axis=-1, keepdims=True)
        sc = jnp.where(tap, sc, NEG_BIG)
        m = jnp.maximum(jnp.max(sc, axis=-1, keepdims=True), s_new)
        if has_sink:
            row_g = lax.broadcasted_iota(jnp.int32, (group, 1), 0)
            sink = jnp.zeros((group, 1), F32)
            for g in range(group):
                sink = jnp.where(row_g == g, sink_ref[j * group + g], sink)
            m = jnp.maximum(m, sink)
        p = jnp.where(tap, jnp.exp(sc - m), 0.0)
        p_new = jnp.exp(s_new - m)
        den = jnp.sum(p, axis=-1, keepdims=True) + p_new
        if has_sink:
            den = den + jnp.exp(sink - m)
        pv = lax.dot_general(p.astype(BF16), v_t.astype(BF16), (((1,), (1,)), ((), ())), preferred_element_type=F32)
        o = (pv + p_new * vnew_ref[0, j:j + 1]) / den
        if has_gate:
            o = gate_ref[0, rows] * o
        o_ref[0, rows] = o
        if want_lse:
            lse_ref[0, rows] = jnp.broadcast_to(m + jnp.log(den), (group, HEAD_DIM))
        newc_ref[0, 0, j] = jnp.where(lane_d == width - 1, kcol_ref[0, j], pltpu.roll(k_t, width - 1, 1))
        newc_ref[0, 1, j] = jnp.where(lane_d == width - 1, vcol_ref[0, j], pltpu.roll(v_t, width - 1, 1))


def decode_window_attention(q, k_new, v_new, cache, *, dil=1, sink=None, gate=None, want_lse=False):
    n, n_q, d = q.shape
    width, n_kv = cache.shape[1], cache.shape[3]
    group = n_q // n_kv
    cache_t = cache.transpose(0, 2, 3, 4, 1)
    row_spec = lambda a: pl.BlockSpec((1,) + a.shape[1:], lambda bi: (bi,) + (0,) * (a.ndim - 1))
    k_col, v_col = k_new[..., None], v_new[..., None]
    args = [q, k_new, v_new, k_col, v_col, cache_t]
    in_specs = [row_spec(a) for a in args]
    if sink is not None:
        in_specs.append(pl.BlockSpec(memory_space=pltpu.SMEM))
        args.append(sink.astype(F32))
    if gate is not None:
        in_specs.append(row_spec(gate))
        args.append(gate)
    o_shape = jax.ShapeDtypeStruct((n, n_q, d), F32)
    out_shape = [o_shape] + ([o_shape] if want_lse else []) + [jax.ShapeDtypeStruct(cache_t.shape, F32)]
    res = pl.pallas_call(
        functools.partial(_decode_kernel, n_kv=n_kv, group=group, width=width, dil=dil, has_sink=sink is not None,
                          has_gate=gate is not None, want_lse=want_lse),
        grid=(n,),
        in_specs=in_specs,
        out_specs=[row_spec(s) for s in out_shape],
        out_shape=out_shape,
        compiler_params=_cparams("parallel"),
        name="decode_window_attention",
    )(*args)
    return tuple(res[:-1]) + (res[-1].transpose(0, 4, 1, 2, 3),)


CMP_PAD = LANES_V7X


def _cmp_select_kernel(q_ref, ck_ref, cv_ref, fold_ref, gate_ref, o_ref, sel_ref, *, tq, n_cmp, n_kv, group):
    i = pl.program_id(1)
    wide = (CMP_PAD, group * tq)
    n_idx = lax.broadcasted_iota(jnp.int32, wide, 0)
    qpos_w = i * tq + (lax.broadcasted_iota(jnp.int32, wide, 1) & (tq - 1))
    valid = (n_idx * CMP_STRIDE + (CMP_LEN - 1) <= qpos_w) & (n_idx < n_cmp)
    n_slc_pad = fold_ref.shape[0]
    blk = lax.broadcasted_iota(jnp.int32, (n_slc_pad, tq), 0)
    cur = lax.shift_right_logical(i * tq + lax.broadcasted_iota(jnp.int32, (n_slc_pad, tq), 1),
                                  SEL_BLOCK.bit_length() - 1)
    cand = (blk >= 1) & (blk <= cur - 2)
    forced = (blk == cur) | ((blk == cur - 1) & (cur >= 1)) | ((blk == 0) & (cur >= 2))
    for j in range(n_kv):
        q = jnp.concatenate(
            [(q_ref[0, :, (j * group + g) * HEAD_DIM:(j * group + g + 1) * HEAD_DIM] * ATTN_SCALE).astype(BF16)
             for g in range(group)], axis=0)
        s_t = lax.dot_general(ck_ref[0, j].astype(BF16), q, (((1,), (1,)), ((), ())), preferred_element_type=F32)
        s_t = jnp.where(valid, s_t, -jnp.inf)
        m = jnp.max(s_t, axis=0, keepdims=True)
        m = jnp.where(m == -jnp.inf, 0.0, m)
        e = jnp.exp(s_t - m)
        p_t = e / jnp.maximum(jnp.sum(e, axis=0, keepdims=True), 1e-30)
        cv = cv_ref[0, j].astype(BF16)
        p_sum = jnp.zeros((CMP_PAD, tq), F32)
        for g in range(group):
            h = j * group + g
            p_g = p_t[:, g * tq:(g + 1) * tq]
            p_sum = p_sum + p_g
            o = jnp.dot(p_g.T.astype(BF16), cv, preferred_element_type=F32)
            o_ref[0, :, h * HEAD_DIM:(h + 1) * HEAD_DIM] = gate_ref[0, :, h:h + 1] * o
        imp = jnp.dot(fold_ref[...], p_sum, preferred_element_type=F32, precision=lax.Precision.HIGHEST)
        sc = jnp.where(cand, imp, -jnp.inf)
        chosen = forced
        for _ in range(N_SEL - N_FORCED):
            best, bi = _first_argmax(sc, blk, n_slc_pad)
            hit = blk == bi
            chosen = chosen | (hit & (best > -jnp.inf))
            sc = jnp.where(hit, -jnp.inf, sc)
        sel_t = jnp.concatenate([jnp.where(chosen, 1.0, 0.0), jnp.zeros((CMP_PAD - n_slc_pad, tq), F32)], axis=0)
        sel_ref[0, j] = sel_t.T.astype(BF16)


def cmp_select_prompt(y_main, comp, gates, tq=LANES_V7X):
    b, t = y_main.shape[:2]
    n_cmp = comp.shape[1]
    n_slc = -(-t // SEL_BLOCK)
    n_slc_pad = -(-n_slc // 8) * 8
    grp = N_Q_C // N_KV_C
    ckv = jnp.pad(comp, ((0, 0), (0, CMP_PAD - n_cmp), (0, 0), (0, 0), (0, 0))).transpose(2, 0, 3, 1, 4)
    ratio, left = SEL_BLOCK // CMP_STRIDE, CMP_LEN // CMP_STRIDE - 1
    mi, bi = jnp.arange(CMP_PAD)[None, :], jnp.arange(n_slc_pad)[:, None]
    fold = ((mi >= ratio * bi - left) & (mi <= ratio * bi + ratio - 1) & (mi < n_cmp) & (bi < n_slc)).astype(F32)
    return pl.pallas_call(
        functools.partial(_cmp_select_kernel, tq=tq, n_cmp=n_cmp, n_kv=N_KV_C, group=grp),
        grid=(b, t // tq),
        in_specs=[pl.BlockSpec((1, tq, C_Q_WIDTH), lambda bi_, i: (bi_, i, 0)),
                  pl.BlockSpec((1, N_KV_C, CMP_PAD, HEAD_DIM), lambda bi_, i: (bi_, 0, 0, 0)),
                  pl.BlockSpec((1, N_KV_C, CMP_PAD, HEAD_DIM), lambda bi_, i: (bi_, 0, 0, 0)),
                  pl.BlockSpec(fold.shape, lambda bi_, i: (0, 0)),
                  pl.BlockSpec((1, tq, gates.shape[2]), lambda bi_, i: (bi_, i, 0))],
        out_specs=[pl.BlockSpec((1, tq, C_Q_WIDTH), lambda bi_, i: (bi_, i, 0)),
                   pl.BlockSpec((1, N_KV_C, tq, CMP_PAD), lambda bi_, i: (bi_, 0, i, 0))],
        out_shape=[jax.ShapeDtypeStruct((b, t, C_Q_WIDTH), F32),
                   jax.ShapeDtypeStruct((b, N_KV_C, t, CMP_PAD), BF16)],
        compiler_params=_cparams("parallel", "parallel"),
        name="cmp_select",
    )(y_main, ckv[0], ckv[1], fold, gates)


def _sel_kernel(q_ref, k_ref, v_ref, sel_ref, gate_ref, o_ref, *, tq, tk, n_kv, group, n_blocks, gate_col):
    i = pl.program_id(1)
    n_chunks = (i * tq + tq + tk - 1) // tk
    qpos = i * tq + lax.broadcasted_iota(jnp.int32, (tq, tk), 0)
    qs = [jnp.concatenate(
        [(q_ref[0, :, (j * group + g) * HEAD_DIM:(j * group + g + 1) * HEAD_DIM] * ATTN_SCALE).astype(BF16)
         for g in range(group)], axis=0) for j in range(n_kv)]

    def body(c, carry):
        k0 = pl.multiple_of(c * tk, tk)
        causal = (k0 + lax.broadcasted_iota(jnp.int32, (tq, tk), 1)) <= qpos
        blk_of_key = lax.shift_right_logical(k0 + lax.broadcasted_iota(jnp.int32, (n_blocks, tk), 1),
                                             SEL_BLOCK.bit_length() - 1)
        expand = (blk_of_key == lax.broadcasted_iota(jnp.int32, (n_blocks, tk), 0)).astype(BF16)
        out = []
        for j in range(n_kv):
            m, l, acc = carry[j]
            ks = slice(j * HEAD_DIM, (j + 1) * HEAD_DIM)
            k = k_ref[0, pl.ds(k0, tk), ks].astype(BF16)
            v = v_ref[0, pl.ds(k0, tk), ks].astype(BF16)
            mask = (jnp.dot(sel_ref[0, j], expand, preferred_element_type=F32) > 0.5) & causal
            s = lax.dot_general(qs[j], k, (((1,), (1,)), ((), ())), preferred_element_type=F32)
            s = jnp.where(mask[None], s.reshape(group, tq, tk), -jnp.inf).reshape(group * tq, tk)
            m_new = jnp.maximum(m, jnp.max(s, axis=-1, keepdims=True))
            p = jnp.exp(s - m_new)
            alpha = jnp.exp(m - m_new)
            l = alpha * l + jnp.sum(p, axis=-1, keepdims=True)
            acc = alpha * acc + jnp.dot(p.astype(BF16), v, preferred_element_type=F32)
            out.append((m_new, l, acc))
        return tuple(out)

    init = tuple((jnp.full((group * tq, 1), NEG_BIG, F32), jnp.zeros((group * tq, 1), F32),
                  jnp.zeros((group * tq, HEAD_DIM), F32)) for _ in range(n_kv))
    final = lax.fori_loop(0, n_chunks, body, init)
    for j in range(n_kv):
        _, l, acc = final[j]
        o = acc / l
        for g in range(group):
            h = j * group + g
            o_ref[0, :, h * HEAD_DIM:(h + 1) * HEAD_DIM] = (
                gate_ref[0, :, gate_col + h:gate_col + h + 1] * o[g * tq:(g + 1) * tq])


def selected_attention_prompt(q_rope, y_main, selmask, gates, *, k_idx, v_idx, gate_col):
    b, t, qw = q_rope.shape
    n_blocks = selmask.shape[-1]
    kw = N_KV_C * HEAD_DIM
    tq, tk = SEL_Q_TILE, SEL_K_TILE
    return pl.pallas_call(
        functools.partial(_sel_kernel, tq=tq, tk=tk, n_kv=N_KV_C, group=N_Q_C // N_KV_C,
                          n_blocks=n_blocks, gate_col=gate_col),
        grid=(b, t // tq),
        in_specs=[pl.BlockSpec((1, tq, qw), lambda bi, i: (bi, i, 0)),
                  pl.BlockSpec((1, t, kw), lambda bi, i: (bi, 0, k_idx)),
                  pl.BlockSpec((1, t, kw), lambda bi, i: (bi, 0, v_idx)),
                  pl.BlockSpec((1, N_KV_C, tq, n_blocks), lambda bi, i: (bi, 0, i, 0)),
                  pl.BlockSpec((1, tq, gates.shape[2]), lambda bi, i: (bi, i, 0))],
        out_specs=pl.BlockSpec((1, tq, qw), lambda bi, i: (bi, i, 0)),
        out_shape=jax.ShapeDtypeStruct((b, t, qw), F32),
        compiler_params=_cparams("parallel", "arbitrary"),
        name="selected_attention",
    )(q_rope, y_main, y_main, selmask, gates)


def _sel_sample_kernel(page_ref, q_ref, qbd_ref, knew_ref, vnew_ref, gate_ref, want_ref, *refs, n_sel):
    del page_ref
    n_q, grp, n_past = N_Q_C, N_Q_C // N_KV_C, n_sel - 1
    pages, o_ref = refs[:n_past * N_KV_C], refs[n_past * N_KV_C]

    def stacked(c):
        return jnp.concatenate(
            [jnp.concatenate([pages[s * N_KV_C + j][0, c, 0] for j in range(N_KV_C)], axis=0) for s in range(n_past)],
            axis=1).astype(BF16)

    width = n_past * PAGE_SIZE
    page_shift = PAGE_SIZE.bit_length() - 1
    sc = jnp.dot((qbd_ref[0] * ATTN_SCALE).astype(BF16), stacked(0), preferred_element_type=F32)
    lane = lax.broadcasted_iota(jnp.int32, (n_q, width), 1)
    key_half = lax.shift_right_logical(lane, SEL_BLOCK.bit_length() - 1) & (PAGE_SIZE // SEL_BLOCK - 1)
    slot_of_lane = lax.shift_right_logical(lax.broadcasted_iota(jnp.int32, (n_sel, width), 1), page_shift) + 1
    expand = (slot_of_lane == lax.broadcasted_iota(jnp.int32, (n_sel, width), 0)).astype(BF16)
    want = jnp.dot(want_ref[0].astype(BF16), expand, preferred_element_type=F32)
    mask = key_half.astype(F32) == want
    s_new = jnp.sum(q_ref[0] * ATTN_SCALE * knew_ref[0], axis=-1, keepdims=True)
    sc = jnp.where(mask, sc, NEG_BIG)
    m = jnp.maximum(jnp.max(sc, axis=-1, keepdims=True), s_new)
    p = jnp.where(mask, jnp.exp(sc - m), 0.0)
    p_new = jnp.exp(s_new - m)
    den = jnp.sum(p, axis=-1, keepdims=True) + p_new
    pv = lax.dot_general(p.astype(BF16), stacked(1), (((1,), (1,)), ((), ())), preferred_element_type=F32)
    head_of_row = lax.shift_right_logical(lax.broadcasted_iota(jnp.int32, (n_q, HEAD_DIM), 0), grp.bit_length() - 1)
    o = jnp.zeros((n_q, HEAD_DIM), F32)
    for j in range(N_KV_C):
        o = jnp.where(head_of_row == j, pv[:, j * HEAD_DIM:(j + 1) * HEAD_DIM], o)
    o_ref[0] = gate_ref[0] * ((o + p_new * vnew_ref[0]) / den)


def selected_attention_sample(q, k_new, v_new, gate, pool, page, want):
    n, n_q, d = q.shape
    grp = n_q // N_KV_C
    n_sel = want.shape[2]
    pool_t = pool.transpose(0, 2, 3, 4, 1)
    eye = jnp.repeat(jnp.eye(N_KV_C, dtype=q.dtype), grp, axis=0)
    q_bd = (q[:, :, None, :] * eye[None, :, :, None]).reshape(n, n_q, N_KV_C * d)
    row_spec = lambda w: pl.BlockSpec((1, n_q, w), lambda bi, pg: (bi, 0, 0))

    def page_spec(s, j):
        return pl.BlockSpec((1, 2, 1, d, PAGE_SIZE), lambda bi, pg: (pg[(bi * N_KV_C + j) * n_sel + s], 0, j, 0, 0))

    page_specs = [page_spec(s, j) for s in range(1, n_sel) for j in range(N_KV_C)]
    grid_spec = pltpu.PrefetchScalarGridSpec(
        num_scalar_prefetch=1,
        grid=(n,),
        in_specs=[row_spec(d), row_spec(N_KV_C * d), row_spec(d), row_spec(d), row_spec(1), row_spec(n_sel)] + page_specs,
        out_specs=row_spec(d),
    )
    return pl.pallas_call(
        functools.partial(_sel_sample_kernel, n_sel=n_sel),
        grid_spec=grid_spec,
        out_shape=jax.ShapeDtypeStruct((n, n_q, d), F32),
        compiler_params=_cparams("parallel"),
        name="selected_attention_sample",
    )(page, q, q_bd, k_new, v_new, gate, want, *([pool_t] * len(page_specs)))


def _half_proj_kernel(*refs, n_rows):
    n_pairs = N_KV_C // 2
    x_refs, w_ref, o_ref = refs[:2 * n_pairs], refs[2 * n_pairs], refs[2 * n_pairs + 1]
    out_w = 2 * 2 * CMP_HID
    for c in range(2):
        for p in range(n_pairs):
            x_ref = x_refs[c * n_pairs + p]
            acc = jnp.zeros((n_rows, out_w), F32)
            for sp in range(CMP_STRIDE // 2):
                lo = x_ref[0, pl.ds(2 * sp, n_rows, stride=CMP_STRIDE), :]
                hi = x_ref[0, pl.ds(2 * sp + 1, n_rows, stride=CMP_STRIDE), :]
                lhs = jnp.concatenate([lo, hi], axis=1).astype(BF16)
                acc = acc + jnp.dot(lhs, w_ref[c, sp], preferred_element_type=F32)
            o_ref[0, :, (c * n_pairs + p) * out_w:(c * n_pairs + p + 1) * out_w] = acc


def half_block_proj(rows_arr, col_idx, w_bd, tile_rows):
    b, l = rows_arr.shape[:2]
    n_half_tile = tile_rows // CMP_STRIDE
    out_w = 2 * 2 * N_KV_C * CMP_HID
    n_chunks = C_KV_WIDTH // LANES_V7X

    def chunk_spec(cp):
        return pl.BlockSpec((1, tile_rows, LANES_V7X), lambda bi, i: (bi, i, col_idx * n_chunks + cp))

    return pl.pallas_call(
        functools.partial(_half_proj_kernel, n_rows=n_half_tile),
        grid=(b, l // tile_rows),
        in_specs=[chunk_spec(cp) for cp in range(n_chunks)] + [pl.BlockSpec(w_bd.shape, lambda bi, i: (0, 0, 0, 0))],
        out_specs=pl.BlockSpec((1, n_half_tile, out_w), lambda bi, i: (bi, i, 0)),
        out_shape=jax.ShapeDtypeStruct((b, l // CMP_STRIDE, out_w), F32),
        compiler_params=_cparams("parallel", "parallel"),
        name="half_block_proj",
    )(*([rows_arr] * n_chunks), w_bd)


def _paged_half_proj_kernel(pt_ref, *refs, n_pages):
    del pt_ref
    n_pairs = N_KV_C // 2
    pages, w_ref, o_ref = refs[:n_pages], refs[n_pages], refs[n_pages + 1]
    rows = refs[n_pages + 2:]
    for j, page in enumerate(pages):
        for c in range(2):
            for p in range(n_pairs):
                x_t = page[0, c, 2 * p:2 * p + 2].reshape(2 * HEAD_DIM, PAGE_SIZE)
                rows[c * n_pairs + p][j * PAGE_SIZE:(j + 1) * PAGE_SIZE, :] = x_t.T
    n_rows = n_pages * PAGE_SIZE // CMP_STRIDE
    out_w = 2 * 2 * CMP_HID
    for c in range(2):
        for p in range(n_pairs):
            x_ref = rows[c * n_pairs + p]
            acc = jnp.zeros((n_rows, out_w), F32)
            for sp in range(CMP_STRIDE // 2):
                lo = x_ref[pl.ds(2 * sp, n_rows, stride=CMP_STRIDE), :]
                hi = x_ref[pl.ds(2 * sp + 1, n_rows, stride=CMP_STRIDE), :]
                lhs = jnp.concatenate([lo, hi], axis=1).astype(BF16)
                acc = acc + jnp.dot(lhs, w_ref[c, sp], preferred_element_type=F32)
            o_ref[0, :, (c * n_pairs + p) * out_w:(c * n_pairs + p + 1) * out_w] = acc


def paged_half_block_proj(pool, page_table, w_bd, n_pages):
    n, pages_per_req = page_table.shape
    pool_t = pool.transpose(0, 2, 3, 4, 1)
    steps = pages_per_req // n_pages
    n_half = n_pages * PAGE_SIZE // CMP_STRIDE
    out_w = 2 * 2 * N_KV_C * CMP_HID

    def page_spec(j):
        return pl.BlockSpec((1, 2, N_KV_C, HEAD_DIM, PAGE_SIZE),
                            lambda bi, i, pt: (pt[bi * pages_per_req + i * n_pages + j], 0, 0, 0, 0))

    grid_spec = pltpu.PrefetchScalarGridSpec(
        num_scalar_prefetch=1,
        grid=(n, steps),
        in_specs=[page_spec(j) for j in range(n_pages)] + [pl.BlockSpec(w_bd.shape, lambda bi, i, pt: (0, 0, 0, 0))],
        out_specs=pl.BlockSpec((1, n_half, out_w), lambda bi, i, pt: (bi, i, 0)),
        scratch_shapes=[pltpu.VMEM((n_pages * PAGE_SIZE, LANES_V7X), F32) for _ in range(C_KV_WIDTH // LANES_V7X)],
    )
    return pl.pallas_call(
        functools.partial(_paged_half_proj_kernel, n_pages=n_pages),
        grid_spec=grid_spec,
        out_shape=jax.ShapeDtypeStruct((n, pages_per_req * PAGE_SIZE // CMP_STRIDE, out_w), F32),
        compiler_params=_cparams("parallel", "arbitrary"),
        name="paged_half_block_proj",
    )(page_table.reshape(-1), *([pool_t] * n_pages), w_bd)


def _half_proj_weight(w1h):
    r = CMP_LEN // CMP_STRIDE
    eye = jnp.eye(2, dtype=F32)
    w = w1h.reshape(2, r, CMP_STRIDE // 2, 2, HEAD_DIM, CMP_HID)
    w = jnp.einsum('cjpldh,kq->cplkdqjh', w, eye)
    return w.reshape(2, CMP_STRIDE // 2, 4 * HEAD_DIM, 2 * r * CMP_HID).astype(BF16)


def _first_argmax(v, iota, size):
    m = jnp.max(v, axis=0, keepdims=True)
    return m, jnp.min(jnp.where(v == m, iota, size), axis=0, keepdims=True)


def _route_tokens(scores, bias):
    tm = scores.shape[1]
    per = N_EXPERTS // N_EXPERT_GROUPS
    biased = scores + bias
    iota_per = lax.broadcasted_iota(jnp.int32, (per, tm), 0)
    iota_grp = lax.broadcasted_iota(jnp.int32, (N_EXPERT_GROUPS, tm), 0)
    grp = jnp.zeros((N_EXPERT_GROUPS, tm), F32)
    slabs = [biased[g * per:(g + 1) * per] for g in range(N_EXPERT_GROUPS)]
    for g, v in enumerate(slabs):
        m1, i1 = _first_argmax(v, iota_per, per)
        m2 = jnp.max(jnp.where(iota_per == i1, -jnp.inf, v), axis=0, keepdims=True)
        grp = jnp.where(iota_grp == g, m1 + m2, grp)
    keep = jnp.zeros((N_EXPERT_GROUPS, tm), jnp.int32)
    for _ in range(TOPK_GROUPS):
        _, gi = _first_argmax(grp, iota_grp, N_EXPERT_GROUPS)
        keep = jnp.where(iota_grp == gi, 1, keep)
        grp = jnp.where(iota_grp == gi, -jnp.inf, grp)
    v = jnp.concatenate([jnp.where(keep[g:g + 1] > 0, slabs[g], -jnp.inf) for g in range(N_EXPERT_GROUPS)], axis=0)
    iota_e = lax.broadcasted_iota(jnp.int32, (N_EXPERTS, tm), 0)
    iota_k = lax.broadcasted_iota(jnp.int32, (TOP_K, tm), 0)
    eidx = jnp.zeros((TOP_K, tm), jnp.int32)
    ew = jnp.zeros((TOP_K, tm), F32)
    for k in range(TOP_K):
        _, ei = _first_argmax(v, iota_e, N_EXPERTS)
        hit = iota_e == ei
        eidx = jnp.where(iota_k == k, ei, eidx)
        ew = jnp.where(iota_k == k, jnp.sum(jnp.where(hit, scores, 0.0), axis=0, keepdims=True), ew)
        v = jnp.where(hit, -jnp.inf, v)
    return eidx, ew / jnp.sum(ew, axis=0, keepdims=True) * ROUTED_SCALE


def _moe_in_kernel(x_ref, g_ref, sh_ref, sc_ref, rwt_ref, rb_ref, sg_ref, su_ref, sd_ref,
                   h_ref, eidx_ref, ew_ref, shared_ref):
    h = _modulated_norm(x_ref[0], g_ref[...], sc_ref[0], sh_ref[0])
    logits_t = lax.dot_general(rwt_ref[...], h, (((1,), (1,)), ((), ())), preferred_element_type=F32,
                               precision=lax.Precision.HIGHEST)
    eidx, ew = _route_tokens(jax.nn.sigmoid(logits_t), rb_ref[...])
    eidx_ref[...] = eidx
    ew_ref[...] = ew
    h_ref[0] = h
    hb = h.astype(BF16)
    gate = jnp.dot(hb, sg_ref[...], preferred_element_type=F32)
    up = jnp.dot(hb, su_ref[...], preferred_element_type=F32)
    mid = (jax.nn.silu(gate) * up).astype(BF16)
    shared_ref[0] = jnp.dot(mid, sd_ref[...], preferred_element_type=F32)


def moe_in(x, g, shift, scale, router_w, router_b, sg_bf, su_bf, sd_bf, tm):
    b, t, k = x.shape
    per_row = shift.shape[1] != 1
    mod_spec = (pl.BlockSpec((1, tm, k), lambda bi, i: (bi, i, 0)) if per_row
                else pl.BlockSpec((1, 1, k), lambda bi, i: (bi, 0, 0)))
    row_spec = lambda wd: pl.BlockSpec((1, tm, wd), lambda bi, i: (bi, i, 0))
    full = lambda a: pl.BlockSpec(a.shape, lambda bi, i: (0,) * a.ndim)
    tiles = t // tm
    tok_spec = pl.BlockSpec((TOP_K, tm), lambda bi, i: (0, bi * tiles + i))
    rwt = router_w.T
    rb = router_b.astype(F32).reshape(N_EXPERTS, 1)
    return pl.pallas_call(
        _moe_in_kernel,
        grid=(b, tiles),
        in_specs=[row_spec(k), pl.BlockSpec((1, k), lambda bi, i: (0, 0)), mod_spec, mod_spec,
                  full(rwt), full(rb), full(sg_bf), full(su_bf), full(sd_bf)],
        out_specs=[row_spec(k), tok_spec, tok_spec, row_spec(k)],
        out_shape=[jax.ShapeDtypeStruct((b, t, k), F32), jax.ShapeDtypeStruct((TOP_K, b * t), jnp.int32),
                   jax.ShapeDtypeStruct((TOP_K, b * t), F32), jax.ShapeDtypeStruct((b, t, k), F32)],
        compiler_params=_cparams("parallel", "parallel"),
        name="moe_in",
    )(x, g.reshape(1, k), shift, scale, rwt, rb, sg_bf, su_bf, sd_bf)


def _gmm_kernel(blk_e_ref, n_used_ref, x_ref, wg_ref, wu_ref, wd_ref, o_ref, wg_s, wu_s, wd_s):
    blk = pl.program_id(0)
    prev_e = blk_e_ref[jnp.maximum(blk - 1, 0)]
    new_expert = (blk == 0) | (blk_e_ref[blk] != prev_e)

    @pl.when(new_expert)
    def _():
        wg_s[...] = wg_ref[0].astype(BF16)
        wu_s[...] = wu_ref[0].astype(BF16)
        wd_s[...] = wd_ref[0].astype(BF16)

    @pl.when(blk < n_used_ref[0])
    def _():
        x = x_ref[...].astype(BF16)
        gate = jnp.dot(x, wg_s[...], preferred_element_type=F32)
        up = jnp.dot(x, wu_s[...], preferred_element_type=F32)
        mid = (jax.nn.silu(gate) * up).astype(BF16)
        o_ref[...] = jnp.dot(mid, wd_s[...], preferred_element_type=F32)

    @pl.when(blk >= n_used_ref[0])
    def _():
        o_ref[...] = jnp.zeros_like(o_ref)


def grouped_experts(xs, blk_e, n_used, w_gate, w_up, w_down, layer):
    cap, k = xs.shape
    tm = MOE_TILE
    n_blk = cap // tm
    de = w_gate.shape[-1]
    grid_spec = pltpu.PrefetchScalarGridSpec(
        num_scalar_prefetch=2,
        grid=(n_blk,),
        in_specs=[pl.BlockSpec((tm, k), lambda i, be, nu: (i, 0)),
                  pl.BlockSpec((None, 1, k, de), lambda i, be, nu: (layer, be[i], 0, 0)),
                  pl.BlockSpec((None, 1, k, de), lambda i, be, nu: (layer, be[i], 0, 0)),
                  pl.BlockSpec((None, 1, de, k), lambda i, be, nu: (layer, be[i], 0, 0))],
        out_specs=pl.BlockSpec((tm, k), lambda i, be, nu: (i, 0)),
        scratch_shapes=[pltpu.VMEM((k, de), BF16), pltpu.VMEM((k, de), BF16), pltpu.VMEM((de, k), BF16)],
    )
    return pl.pallas_call(
        _gmm_kernel,
        grid_spec=grid_spec,
        out_shape=jax.ShapeDtypeStruct((cap, k), F32),
        compiler_params=_cparams("arbitrary"),
        name="grouped_experts",
    )(blk_e, n_used, xs, w_gate, w_up, w_down)


def _moe_out_kernel(x_ref, gate_ref, picked_ref, ew_ref, shared_ref, o_ref):
    routed = picked_ref[0] * ew_ref[:, 0:1]
    for k in range(1, TOP_K):
        routed = routed + picked_ref[k] * ew_ref[:, k:k + 1]
    o_ref[0] = x_ref[0] + gate_ref[0] * (routed + shared_ref[0])


def moe_out(x, gate, picked, ew, shared, tm, row0):
    b, t, d = x.shape
    per_row = gate.shape[1] != 1
    gate_spec = (pl.BlockSpec((1, tm, d), lambda bi, i: (bi, i, 0)) if per_row
                 else pl.BlockSpec((1, 1, d), lambda bi, i: (bi, 0, 0)))
    row_spec = pl.BlockSpec((1, tm, d), lambda bi, i: (bi, i, 0))
    tiles, off = t // tm, row0 // tm
    return pl.pallas_call(
        _moe_out_kernel,
        grid=(b, tiles),
        in_specs=[row_spec, gate_spec,
                  pl.BlockSpec((TOP_K, tm, d), lambda bi, i: (0, off + bi * tiles + i, 0)),
                  pl.BlockSpec((tm, TOP_K), lambda bi, i: (off + bi * tiles + i, 0)),
                  row_spec],
        out_specs=row_spec,
        out_shape=jax.ShapeDtypeStruct((b, t, d), F32),
        compiler_params=_cparams("parallel", "parallel"),
        name="moe_out",
    )(x, gate, picked, ew, shared)


def _final_norm_kernel(x_ref, g_ref, o_ref):
    x = x_ref[0]
    var = jnp.mean(x * x, axis=-1, keepdims=True)
    o_ref[0] = x * lax.rsqrt(var + RMS_EPS) * g_ref[...]


def final_norm(x, g, tm):
    b, t, d = x.shape
    row_spec = pl.BlockSpec((1, tm, d), lambda bi, i: (bi, i, 0))
    return pl.pallas_call(
        _final_norm_kernel,
        grid=(b, t // tm),
        in_specs=[row_spec, pl.BlockSpec((1, d), lambda bi, i: (0, 0))],
        out_specs=row_spec,
        out_shape=jax.ShapeDtypeStruct((b, t, d), F32),
        compiler_params=_cparams("parallel", "parallel"),
        name="final_norm",
    )(x, g.reshape(1, d))


def _ada_kernel(c_ref, w_ref, b_ref, o_ref):
    a = jax.nn.silu(c_ref[...]).astype(BF16)
    o_ref[0] = jnp.dot(a, w_ref[0].astype(BF16), preferred_element_type=F32) + b_ref[0]


def ada_mods(c_all, ada_w, ada_b):
    n, d = c_all.shape
    depth, _, width = ada_w.shape
    tn = 1024
    return pl.pallas_call(
        _ada_kernel,
        grid=(depth, width // tn),
        in_specs=[pl.BlockSpec((n, d), lambda l, j: (0, 0)),
                  pl.BlockSpec((1, d, tn), lambda l, j: (l, 0, j)),
                  pl.BlockSpec((1, 1, tn), lambda l, j: (l, 0, j))],
        out_specs=pl.BlockSpec((1, n, tn), lambda l, j: (l, 0, j)),
        out_shape=jax.ShapeDtypeStruct((depth, n, width), F32),
        compiler_params=_cparams("parallel", "parallel"),
        name="ada_mods",
    )(c_all, ada_w, ada_b.reshape(depth, 1, width))


def _masked_softmax(s, mask, sink=None):
    s = jnp.where(mask, s, -jnp.inf)
    m = jnp.max(s, axis=-1, keepdims=True)
    if sink is not None:
        m = jnp.maximum(m, sink)
    m = jnp.where(jnp.isfinite(m), m, 0.0)
    e = jnp.exp(s - m)
    den = jnp.sum(e, axis=-1, keepdims=True)
    if sink is not None:
        den = den + jnp.exp(sink - m)
    p = e / jnp.maximum(den, 1e-30)
    return p, (m + jnp.log(den))[..., 0]


def _cmp_to_sel(p, n_slc):
    ratio = SEL_BLOCK // CMP_STRIDE
    left = CMP_LEN // CMP_STRIDE - 1
    pad = [(0, 0)] * (p.ndim - 1) + [(left, ratio * n_slc - p.shape[-1])]
    pp = jnp.pad(p, pad)
    out = pp[..., 0:ratio * n_slc:ratio]
    for o in range(1, ratio + left):
        out = out + pp[..., o:o + ratio * n_slc:ratio]
    return out


def _select_blocks(imp, pos, n_slc):
    k_top = N_SEL - N_FORCED
    cur = pos // SEL_BLOCK
    j = jnp.arange(n_slc)
    cand = (j[None, :] >= 1) & (j[None, :] <= cur[:, None] - 2)
    sc = jnp.where(cand, imp, -jnp.inf)
    if n_slc < k_top:
        sc = jnp.pad(sc, ((0, 0), (0, 0), (0, 0), (0, k_top - n_slc)), constant_values=-jnp.inf)
    lane = jnp.arange(sc.shape[-1])
    vals, idx = [], []
    for _ in range(k_top):
        best = jnp.argmax(sc, axis=-1)
        vals.append(jnp.max(sc, axis=-1))
        idx.append(best)
        sc = jnp.where(lane == best[..., None], -jnp.inf, sc)
    vals, idx = jnp.stack(vals, axis=-1), jnp.stack(idx, axis=-1)
    forced = jnp.stack([cur, cur - 1, jnp.zeros_like(cur)], axis=-1)
    forced_ok = jnp.stack([cur >= 0, cur >= 1, cur >= 2], axis=-1)
    lead = imp.shape[:2]
    blocks = jnp.concatenate([jnp.broadcast_to(forced, lead + forced.shape), idx.astype(cur.dtype)], axis=-1)
    ok = jnp.concatenate([jnp.broadcast_to(forced_ok, lead + forced_ok.shape), vals > -jnp.inf], axis=-1)
    return jnp.clip(blocks, 0, n_slc - 1), ok


def _compress(hp, n_cmp, w1h, b1, w2, pe):
    b = hp.shape[0]
    r = CMP_LEN // CMP_STRIDE
    hp = hp.reshape(b, hp.shape[1], 2, N_KV_C, r, CMP_HID)
    pre = (jnp.einsum('cjsd,cjsdh->ch', pe.reshape(2, r, CMP_STRIDE, HEAD_DIM), w1h) + b1)[:, None, :]
    for j in range(r):
        pre = pre + hp[:, j:j + n_cmp, :, :, j]
    return jnp.einsum('bnckh,chd->bnckd', jax.nn.gelu(pre), w2)


def _compressed_attention(q_cmp, comp, pos):
    blk_end = jnp.arange(comp.shape[1]) * CMP_STRIDE + CMP_LEN - 1
    s = jnp.einsum('bqhgd,bnhd->bhgqn', q_cmp, comp[:, :, 0], preferred_element_type=F32) * ATTN_SCALE
    p_cmp, _ = _masked_softmax(s, blk_end[None, :] <= pos[:, None])
    o_cmp = jnp.einsum('bhgqn,bnhd->bqhgd', p_cmp.astype(comp.dtype), comp[:, :, 1])
    return o_cmp, p_cmp


A_SPECS = ((0, (N_Q_A + N_KV_A) * HEAD_DIM, 'rope', 0, 0),
           ((N_Q_A + N_KV_A) * HEAD_DIM, (N_Q_A + 2 * N_KV_A) * HEAD_DIM, 'raw', 0, (N_Q_A + N_KV_A) * HEAD_DIM))
B_SPECS = ((0, 2 * B_WIDTH, 'rope', 0, 0), (2 * B_WIDTH, 3 * B_WIDTH, 'raw', 0, 2 * B_WIDTH))
_C0 = C_Q_WIDTH + C_KV_WIDTH
_C1 = _C0 + C_KV_WIDTH
C_SPECS = ((0, _C0, 'raw', 0, 0),
           (_C0, _C0 + C_KV_WIDTH // 2, 'rope', 0, _C0), (_C0 + C_KV_WIDTH // 2, _C1, 'raw', 0, _C0 + C_KV_WIDTH // 2),
           (_C1, _C1 + C_KV_WIDTH // 2, 'rope', 0, _C1), (_C1 + C_KV_WIDTH // 2, C_MAIN_WIDTH, 'raw', 0, _C1 + C_KV_WIDTH // 2),
           (0, C_Q_WIDTH, 'rope', 1, 0),
           (C_MAIN_WIDTH, C_MAIN_WIDTH + N_GATES_C, 'sigmoid', 2, 0))
_AK = N_Q_A * HEAD_DIM
_AW = N_KV_A * HEAD_DIM
A_SPECS_P = A_SPECS + ((_AK, _AK + _AW, 'rope', 1, 0), (_AK + _AW, _AK + 2 * _AW, 'raw', 1, _AW))
_BW = B_HEADS_PER_GROUP * HEAD_DIM
B_SPECS_P = B_SPECS + tuple(
    seg for gi in range(N_GROUPS_B) for seg in
    ((B_WIDTH + gi * _BW, B_WIDTH + (gi + 1) * _BW, 'rope', 1 + gi, 0),
     (2 * B_WIDTH + gi * _BW, 2 * B_WIDTH + (gi + 1) * _BW, 'raw', 1 + gi, _BW)))
_CH = C_KV_WIDTH // 2
C_SPECS_P = C_SPECS + ((C_Q_WIDTH, _C0, 'raw', 3, 0),
                       (_C0, _C0 + _CH, 'rope', 4, 0), (_C0 + _CH, _C1, 'raw', 4, _CH),
                       (_C1, _C1 + _CH, 'rope', 5, 0), (_C1 + _CH, C_MAIN_WIDTH, 'raw', 5, _CH))


def _kv_from_cols(cols, n_kv, rows):
    b = cols.shape[0]
    return cols[:, :, cols.shape[2] - rows:].reshape(b, 2, n_kv, HEAD_DIM, rows).transpose(0, 4, 1, 2, 3)


def _mixer_a(x, mods, g, w_qkv_bf, w_o_bf, sink, tabs, tm, past):
    shift, scale, gate = mods
    nq = N_Q_A * HEAD_DIM
    kvw = N_KV_A * HEAD_DIM
    if past is None:
        y, kv_cols = norm_linear(x, g, shift, scale, w_qkv_bf, tabs, A_SPECS_P,
                                 ((nq + 2 * kvw, 'rows'), (2 * kvw, 'cols_last')), tm)
        o = band_attention(y, y, y, dil=1, n_kv=N_KV_A, group=N_Q_A // N_KV_A, tq=WINDOW_A,
                           q_idx=0, k_idx=nq // kvw, v_idx=nq // kvw + 1, sink=sink)
        new = _kv_from_cols(kv_cols, N_KV_A, min(WINDOW_A, y.shape[1]))
    else:
        (y,) = norm_linear(x, g, shift, scale, w_qkv_bf, tabs, A_SPECS, ((nq + 2 * kvw, 'rows'),), tm)
        n = y.shape[1]
        o, new = decode_window_attention(y[0, :, :nq].reshape(n, N_Q_A, HEAD_DIM),
                                         y[0, :, nq:nq + kvw].reshape(n, N_KV_A, HEAD_DIM),
                                         y[0, :, nq + kvw:].reshape(n, N_KV_A, HEAD_DIM), past, sink=sink)
        o = o.reshape(1, n, nq)
    return linear_out([o], w_o_bf, x, gate, 'plain', tm), new


def _mixer_b(x, mods, g, w_qkv_bf, w_o_bf, tabs, tm, past):
    shift, scale, gate = mods
    hpg = B_HEADS_PER_GROUP
    gw = hpg * HEAD_DIM
    outs, lses, news = [], [], []
    if past is None:
        t = x.shape[1]
        descs = [(3 * B_WIDTH, 'rows')] + [(2 * gw, 'cols_last' if win <= tm else 'cols') for win, _ in B_PATTERNS]
        y, *kv_cols = norm_linear(x, g, shift, scale, w_qkv_bf, tabs, B_SPECS_P, descs, tm)
        for gi, (win, dil) in enumerate(B_PATTERNS):
            o, lse = band_attention(y, y, y, dil=dil, n_kv=hpg, group=1, tq=win // dil,
                                    q_idx=gi, k_idx=N_GROUPS_B + gi, v_idx=2 * N_GROUPS_B + gi, want_lse=True)
            outs.append(o)
            lses.append(lse)
            news.append(_kv_from_cols(kv_cols[gi], hpg, min(win, t)))
    else:
        (y,) = norm_linear(x, g, shift, scale, w_qkv_bf, tabs, B_SPECS, ((3 * B_WIDTH, 'rows'),), tm)
        n = y.shape[1]
        for gi, (win, dil) in enumerate(B_PATTERNS):
            part = lambda c: y[0, :, c * B_WIDTH + gi * gw:c * B_WIDTH + (gi + 1) * gw].reshape(n, hpg, HEAD_DIM)
            o, lse, new = decode_window_attention(part(0), part(1), part(2), past[gi], dil=dil, want_lse=True)
            outs.append(o.reshape(1, n, gw))
            lses.append(lse.reshape(1, n, gw))
            news.append(new)
    return linear_out(outs + lses, w_o_bf, x, gate, 'mix3', tm), tuple(news)


def _mixer_c(x, mods, g, w_in_bf, w_o_bf, cmp_w1, cmp_b1, cmp_w2, cmp_pe, tabs, pos, tm, past):
    shift, scale, gate = mods
    grp = N_Q_C // N_KV_C
    kvw = N_KV_C * HEAD_DIM
    descs = [(C_MAIN_WIDTH, 'rows'), (C_Q_WIDTH, 'rows'), (N_GATES_C, 'rows')]
    w1h = cmp_w1.reshape(2, CMP_LEN // CMP_STRIDE, CMP_STRIDE, HEAD_DIM, CMP_HID)
    w_bd = _half_proj_weight(w1h)
    cmp_idx, slc_idx, win_idx = C_Q_WIDTH // C_KV_WIDTH, _C0 // C_KV_WIDTH, _C1 // C_KV_WIDTH
    if past is None:
        descs += [(C_KV_WIDTH, 'cols'), (C_KV_WIDTH, 'cols'), (C_KV_WIDTH, 'cols_last')]
        y, q_rope, gates, cmp_cols, slc_cols, win_cols = norm_linear(x, g, shift, scale, w_in_bf, tabs, C_SPECS_P,
                                                                     descs, tm)
        b, t = y.shape[:2]
        hp = half_block_proj(y, cmp_idx, w_bd, t)
        n_cmp = (t - CMP_LEN) // CMP_STRIDE + 1
        comp = _compress(hp, n_cmp, w1h, cmp_b1, cmp_w2, cmp_pe)
        o_cmp, selmask = cmp_select_prompt(y, comp, gates)
        o_slc = selected_attention_prompt(q_rope, y, selmask, gates, k_idx=2 * slc_idx, v_idx=2 * slc_idx + 1,
                                          gate_col=N_Q_C)
        o_win = band_attention(q_rope, y, y, dil=1, n_kv=N_KV_C, group=grp, tq=WINDOW_C, q_idx=0,
                               k_idx=2 * win_idx, v_idx=2 * win_idx + 1, gate=gates, gate_col=2 * N_Q_C)
        new_win = _kv_from_cols(win_cols, N_KV_C, min(WINDOW_C, t))
        cmp_kv = _kv_from_cols(cmp_cols, N_KV_C, t)
        slc_kv = _kv_from_cols(slc_cols, N_KV_C, t)
    else:
        y, q_rope, gates = norm_linear(x, g, shift, scale, w_in_bf, tabs, C_SPECS, descs, tm)
        win_buf, cmp_pool, slc_pool, page_table = past
        n = y.shape[1]
        t = 1
        cmp_kv = y[0, :, C_Q_WIDTH:_C0].reshape(n, t, 2, N_KV_C, HEAD_DIM)
        slc_kv = y[0, :, _C0:_C1].reshape(n, t, 2, N_KV_C, HEAD_DIM)
        win_kv = y[0, :, _C1:].reshape(n, t, 2, N_KV_C, HEAD_DIM)
        q_cmp = y[0, :, :C_Q_WIDTH].reshape(n, t, N_KV_C, grp, HEAD_DIM)
        gview = gates[0].reshape(n, t, 3, N_KV_C, grp, 1)
        hp = paged_half_block_proj(cmp_pool, page_table, w_bd, CMP_PAGES_PER_STEP)
        length = PAST_LEN + t
        n_cmp = (length - CMP_LEN) // CMP_STRIDE + 1
        comp = _compress(hp, n_cmp, w1h, cmp_b1, cmp_w2, cmp_pe)
        o_cmp, p_cmp = _compressed_attention(q_cmp, comp, pos)
        n_slc = -(-length // SEL_BLOCK)
        blocks, ok = _select_blocks(_cmp_to_sel(p_cmp.sum(axis=2), n_slc), pos, n_slc)
        sub = PAGE_SIZE // SEL_BLOCK
        pb = jnp.minimum(blocks[:, :, 0], PAST_LEN // SEL_BLOCK - 1)
        page = jnp.take_along_axis(jnp.broadcast_to(page_table[:, None], (n, N_KV_C, page_table.shape[1])),
                                   pb // sub, axis=2)
        want = jnp.where(ok[:, :, 0], pb % sub, -1).astype(F32)
        o_slc = selected_attention_sample(
            q_rope[0].reshape(n, N_Q_C, HEAD_DIM), jnp.repeat(slc_kv[:, 0, 0], grp, axis=1),
            jnp.repeat(slc_kv[:, 0, 1], grp, axis=1), gates[0, :, N_Q_C:2 * N_Q_C].reshape(n, N_Q_C, 1),
            slc_pool, page.reshape(-1).astype(jnp.int32), jnp.repeat(want, grp, axis=1))
        o_slc = o_slc.reshape(1, n, C_Q_WIDTH)
        o_win, new_win = decode_window_attention(
            q_rope[0].reshape(n, N_Q_C, HEAD_DIM), win_kv[:, 0, 0], win_kv[:, 0, 1], win_buf,
            gate=gates[0, :, 2 * N_Q_C:].reshape(n, N_Q_C, 1))
        o_win = o_win.reshape(1, n, C_Q_WIDTH)
        o_cmp = (gview[:, :, 0] * o_cmp.astype(F32)).reshape(1, n, C_Q_WIDTH)
    return linear_out([o_cmp, o_slc, o_win], w_o_bf, x, gate, 'sum3', tm), (new_win, cmp_kv, slc_kv)


def _dispatch_plan(eidx, n_blk):
    n = eidx.shape[0]
    tm = MOE_TILE
    chunk = LANES_V7X
    onehot = (eidx[:, :, None] == jnp.arange(N_EXPERTS)).astype(jnp.int32)
    sel = onehot.sum(axis=1).astype(F32).reshape(n // chunk, chunk, N_EXPERTS)
    tril = jnp.tril(jnp.ones((chunk, chunk), F32))
    within = jnp.einsum('ij,cjk->cik', tril, sel)
    chunk_tot = within[:, -1, :]
    before = jnp.cumsum(chunk_tot, axis=0) - chunk_tot
    rank = (within - sel + before[:, None, :]).reshape(n, N_EXPERTS).astype(jnp.int32)
    counts = jnp.sum(chunk_tot, axis=0).astype(jnp.int32)
    padded = (counts + tm - 1) // tm * tm
    ends = jnp.cumsum(padded)
    dest = jnp.sum(onehot * (ends - padded + rank)[:, None, :], axis=-1)
    blk_start = jnp.arange(n_blk, dtype=jnp.int32) * tm
    blk_e = jnp.minimum(jnp.sum(ends[None, :] <= blk_start[:, None], axis=1), N_EXPERTS - 1).astype(jnp.int32)
    n_used = (ends[-1] // tm).astype(jnp.int32).reshape(1)
    sorted_tok = (jnp.argsort(eidx.reshape(-1), stable=True) // TOP_K).astype(jnp.int32)
    shift = (ends - padded) - (jnp.cumsum(counts) - counts)
    dense = jnp.arange(n_blk * tm, dtype=jnp.int32) - jnp.repeat(shift[blk_e], tm)
    row_tok = sorted_tok[jnp.clip(dense, 0, n * TOP_K - 1)]
    return dest, row_tok, blk_e, n_used


def _moe(groups, g, router_w, router_b, w_gate, w_up, w_down, layer, sg_bf, su_bf, sd_bf):
    d = groups[0][0].shape[-1]
    fronts = [moe_in(x, g, mods[0], mods[1], router_w, router_b, sg_bf, su_bf, sd_bf, tm) for x, mods, tm in groups]
    join = lambda parts, axis: parts[0] if len(parts) == 1 else jnp.concatenate(parts, axis=axis)
    h_all = join([f[0].reshape(-1, d) for f in fronts], 0)
    eidx = join([f[1] for f in fronts], 1).T
    ew = join([f[2] for f in fronts], 1).T
    n = h_all.shape[0]
    n_blk = (n * TOP_K + N_EXPERTS * (MOE_TILE - 1)) // MOE_TILE + 1
    dest, row_tok, blk_e, n_used = _dispatch_plan(eidx, n_blk)
    ys = grouped_experts(h_all[row_tok], blk_e, n_used, w_gate, w_up, w_down, layer)
    picked = ys[dest.T.reshape(-1)].reshape(TOP_K, n, d)
    outs, row0 = [], 0
    for (x, mods, tm), front in zip(groups, fronts):
        outs.append(moe_out(x, mods[2], picked, ew, front[3], min(tm, MOE_OUT_TILE), row0))
        row0 += x.shape[0] * x.shape[1]
    return outs


def kernel(x_prompt, x_sample, c_prompt, c_sample, cache_a_kv, cache_b_kv_w128, cache_b_kv_w512, cache_b_kv_w2048, cache_c_win_kv, cache_c_cmp_kv, cache_c_slc_kv, page_table, norm_g, final_g, ada_w, ada_b, a_w_qkv, a_w_o, a_sink, b_w_qkv, b_w_o, c_w_in, c_w_o, c_cmp_w1, c_cmp_b1, c_cmp_w2, c_cmp_pe, moe_router, moe_bias, moe_w_gate, moe_w_up, moe_w_down, shared_w_gate, shared_w_up, shared_w_down):
    bp, seq, d = x_prompt.shape
    ns = x_sample.shape[0]
    b_caches = (cache_b_kv_w128, cache_b_kv_w512, cache_b_kv_w2048)
    pos_p = jnp.arange(seq, dtype=jnp.int32)
    pos_s = PAST_LEN + jnp.arange(x_sample.shape[1], dtype=jnp.int32)
    tabs_p = _rope_tables(pos_p, seq)
    tabs_s = _rope_tables(pos_s, ns)

    mods = ada_mods(jnp.concatenate([c_prompt, c_sample], axis=0), ada_w, ada_b)
    mods_p = mods[:, :bp].reshape(DEPTH, bp, 6, 1, d)
    mods_s = mods[:, bp:].reshape(DEPTH, 1, ns, 6, d)

    per = bp // PROMPT_CHAINS
    chains = [slice(c * per, (c + 1) * per) for c in range(PROMPT_CHAINS)]
    xps = [x_prompt[c] for c in chains]
    xs = x_sample.reshape(1, ns, d)
    st_p = {0: [], 1: [], 2: []}
    st_s = {0: [], 1: [], 2: []}
    for l in range(DEPTH):
        kind, slot = LAYER_KIND[l], LAYER_SLOT[l]
        mps = [[mods_p[l, c, i] for i in range(6)] for c in chains]
        ms = [mods_s[l, :, :, i] for i in range(6)]
        g_mix, g_moe = norm_g[l, 0], norm_g[l, 1]
        if kind == 0:
            w_in, w_o = a_w_qkv[slot].astype(BF16), a_w_o[slot].astype(BF16)
            res = [_mixer_a(xp, mp[:3], g_mix, w_in, w_o, a_sink[slot], tabs_p, ROW_TILE, None) for xp, mp in zip(xps, mps)]
            xs, ss = _mixer_a(xs, ms[:3], g_mix, w_in, w_o, a_sink[slot], tabs_s, ns, cache_a_kv[slot])
        elif kind == 1:
            w_in, w_o = b_w_qkv[slot].astype(BF16), b_w_o[slot].astype(BF16)
            res = [_mixer_b(xp, mp[:3], g_mix, w_in, w_o, tabs_p, ROW_TILE, None) for xp, mp in zip(xps, mps)]
            xs, ss = _mixer_b(xs, ms[:3], g_mix, w_in, w_o, tabs_s, ns, tuple(buf[slot] for buf in b_caches))
        else:
            w_in, w_o = c_w_in[slot].astype(BF16), c_w_o[slot].astype(BF16)
            cargs = (c_cmp_w1[slot], c_cmp_b1[slot], c_cmp_w2[slot], c_cmp_pe[slot])
            res = [_mixer_c(xp, mp[:3], g_mix, w_in, w_o, *cargs, tabs_p, pos_p, ROW_TILE, None) for xp, mp in zip(xps, mps)]
            xs, ss = _mixer_c(xs, ms[:3], g_mix, w_in, w_o, *cargs, tabs_s, pos_s, ns,
                              (cache_c_win_kv[slot], cache_c_cmp_kv[slot], cache_c_slc_kv[slot], page_table))
        xps = [r[0] for r in res]
        states = [r[1] for r in res]
        cat = lambda parts: jnp.concatenate(parts, axis=0)
        st_p[kind].append(tuple(cat([s[i] for s in states]) for i in range(len(states[0])))
                          if isinstance(states[0], tuple) else cat(states))
        st_s[kind].append(ss)
        moe_args = (g_moe, moe_router[l], moe_bias[l], moe_w_gate, moe_w_up, moe_w_down, l,
                    shared_w_gate[l].astype(BF16), shared_w_up[l].astype(BF16), shared_w_down[l].astype(BF16))
        for c in range(PROMPT_CHAINS - 1):
            (xps[c],) = _moe([(xps[c], mps[c][3:], ROW_TILE)], *moe_args)
        xps[-1], xs = _moe([(xps[-1], mps[-1][3:], ROW_TILE), (xs, ms[3:], ns)], *moe_args)
    y_prompt = jnp.concatenate([final_norm(xp, final_g, ROW_TILE) for xp in xps], axis=0)
    y_sample = final_norm(xs, final_g, ns).reshape(x_sample.shape)

    outs = [y_prompt, y_sample, jnp.stack(st_p[0]), jnp.stack(st_s[0])]
    for i in range(N_GROUPS_B):
        outs += [jnp.stack([s[i] for s in st_p[1]]), jnp.stack([s[i] for s in st_s[1]])]
    outs += [jnp.stack([s[0] for s in st_p[2]]), jnp.stack([s[0] for s in st_s[2]])]
    outs += [jnp.stack([s[1] for s in st_p[2]]), jnp.stack([s[1] for s in st_s[2]])]
    outs += [jnp.stack([s[2] for s in st_p[2]]), jnp.stack([s[2] for s in st_s[2]])]
    return tuple(outs)
```

```python
import functools

import jax
import jax.numpy as jnp
from jax import lax
from jax.experimental import pallas as pl
from jax.experimental.pallas import tpu as pltpu

D_MODEL = 1024
DEPTH = 4
PAST_LEN = 8192
PAGE_SIZE = 128
HEAD_DIM = 64
ROT_DIM = HEAD_DIM // 4
ROPE_THETA = 500000.0
ATTN_SCALE = HEAD_DIM ** -0.5
RMS_EPS = 1e-6

N_MIXERS = 3
LAYER_KIND = tuple(i % N_MIXERS for i in range(DEPTH))
LAYER_SLOT = tuple(LAYER_KIND[:i].count(LAYER_KIND[i]) for i in range(DEPTH))

N_Q_A = 16
N_KV_A = 4
WINDOW_A = 128
B_PATTERNS = ((128, 1), (512, 4), (2048, 16))
N_GROUPS_B = len(B_PATTERNS)
B_HEADS_PER_GROUP = 4
B_WIDTH = N_GROUPS_B * B_HEADS_PER_GROUP * HEAD_DIM
N_Q_C = 16
N_KV_C = 4
CMP_LEN = 32
CMP_STRIDE = 16
CMP_HID = 64
SEL_BLOCK = 64
N_SEL = 16
N_FORCED = 3
WINDOW_C = 512
C_KV_WIDTH = 2 * N_KV_C * HEAD_DIM
C_Q_WIDTH = N_Q_C * HEAD_DIM
C_MAIN_WIDTH = C_Q_WIDTH + 3 * C_KV_WIDTH
N_GATES_C = 3 * N_Q_C

N_EXPERTS = 64
TOP_K = 8
N_EXPERT_GROUPS = 8
TOPK_GROUPS = 4
D_EXPERT = 256
ROUTED_SCALE = 2.5

LANES_V7X = 128
VMEM_LIMIT_V7X = 56 * 1024 * 1024

ROW_TILE = 512
COL_GROUP = 512
MOE_TILE = 512
MOE_OUT_TILE = 256
CMP_PAGES_PER_STEP = 32
PROMPT_CHAINS = 1
DECODE_REQS_PER_STEP = 4
DECODE_ROWS_PER_STEP = 2048
SEL_Q_TILE = 128
SEL_K_TILE = 256
NEG_BIG = -1e30

BF16 = jnp.bfloat16
F32 = jnp.float32


def _cparams(*sem):
    return pltpu.CompilerParams(dimension_semantics=sem, vmem_limit_bytes=VMEM_LIMIT_V7X)


def _rope_tables(pos, rows):
    half = ROT_DIM // 2
    inv_freq = ROPE_THETA ** (-jnp.arange(half, dtype=F32) / half)
    ang = pos.astype(F32)[:, None] * inv_freq
    cos, sin = jnp.cos(ang), jnp.sin(ang)
    t = pos.shape[0]
    z8 = jnp.zeros((t, half), F32)
    rest1 = jnp.ones((t, HEAD_DIM - ROT_DIM), F32)
    rest0 = jnp.zeros((t, HEAD_DIM - ROT_DIM), F32)
    reps = LANES_V7X // HEAD_DIM
    cos_t = jnp.tile(jnp.concatenate([cos, cos, rest1], axis=1), (1, reps))
    sin_a = jnp.tile(jnp.concatenate([-sin, z8, rest0], axis=1), (1, reps))
    sin_b = jnp.tile(jnp.concatenate([z8, sin, rest0], axis=1), (1, reps))
    return tuple(jnp.broadcast_to(a, (rows, LANES_V7X)) for a in (cos_t, sin_a, sin_b))


def _rope_chunk(blk, cos_t, sin_a, sin_b):
    return (blk * cos_t + pltpu.roll(blk, LANES_V7X - ROT_DIM // 2, 1) * sin_a
            + pltpu.roll(blk, ROT_DIM // 2, 1) * sin_b)


def _build_plan(n_cols, specs):
    plan = []
    for c0 in range(0, n_cols, COL_GROUP):
        width = min(COL_GROUP, n_cols - c0)
        segs = []
        for (s0, s1, mode, oi, d0) in specs:
            lo, hi = max(s0, c0), min(s1, c0 + width)
            if lo >= hi:
                continue
            step = LANES_V7X if mode == 'rope' else hi - lo
            for a in range(lo, hi, step):
                segs.append((a - c0, min(step, hi - a), mode, oi, d0 + a - s0))
        plan.append((c0, width, tuple(segs)))
    return tuple(plan)


def _modulated_norm(x, g, scale, shift):
    var = jnp.mean(x * x, axis=-1, keepdims=True)
    y = x * lax.rsqrt(var + RMS_EPS) * g
    return y * (1 + scale) + shift


def _norm_linear_kernel(x_ref, g_ref, sh_ref, sc_ref, w_ref, cos_ref, sa_ref, sb_ref, *refs, plan, layouts):
    outs, h_ref = refs[:len(layouts)], refs[len(layouts)]
    h_ref[...] = _modulated_norm(x_ref[0], g_ref[...], sc_ref[0], sh_ref[0]).astype(BF16)
    for (c0, width, segs) in plan:
        acc = jnp.dot(h_ref[...], w_ref[:, c0:c0 + width], preferred_element_type=F32)
        for (off, wd, mode, oi, dst) in segs:
            blk = acc[:, off:off + wd]
            if mode == 'rope':
                blk = _rope_chunk(blk, cos_ref[...], sa_ref[...], sb_ref[...])
            elif mode == 'sigmoid':
                blk = jax.nn.sigmoid(blk)
            if layouts[oi] == 'rows':
                outs[oi][0, :, dst:dst + wd] = blk
            else:
                outs[oi][0, dst:dst + wd, :] = blk.T


def norm_linear(x, g, shift, scale, w_bf, tabs, specs, out_descs, tm):
    b, t, k = x.shape
    n = w_bf.shape[1]
    per_row = shift.shape[1] != 1
    mod_spec = (pl.BlockSpec((1, tm, k), lambda bi, i: (bi, i, 0)) if per_row
                else pl.BlockSpec((1, 1, k), lambda bi, i: (bi, 0, 0)))
    tab_spec = pl.BlockSpec((tm, LANES_V7X), lambda bi, i: (i, 0))
    plan = _build_plan(n, specs)
    out_specs, out_shape = [], []
    for wd, layout in out_descs:
        if layout == 'rows':
            out_specs.append(pl.BlockSpec((1, tm, wd), lambda bi, i: (bi, i, 0)))
            out_shape.append(jax.ShapeDtypeStruct((b, t, wd), F32))
        elif layout == 'cols':
            out_specs.append(pl.BlockSpec((1, wd, tm), lambda bi, i: (bi, 0, i)))
            out_shape.append(jax.ShapeDtypeStruct((b, wd, t), F32))
        else:
            out_specs.append(pl.BlockSpec((1, wd, tm), lambda bi, i: (bi, 0, 0)))
            out_shape.append(jax.ShapeDtypeStruct((b, wd, tm), F32))
    outs = pl.pallas_call(
        functools.partial(_norm_linear_kernel, plan=plan, layouts=tuple(l for _, l in out_descs)),
        grid=(b, t // tm),
        in_specs=[pl.BlockSpec((1, tm, k), lambda bi, i: (bi, i, 0)),
                  pl.BlockSpec((1, k), lambda bi, i: (0, 0)),
                  mod_spec, mod_spec,
                  pl.BlockSpec((k, n), lambda bi, i: (0, 0)),
                  tab_spec, tab_spec, tab_spec],
        out_specs=out_specs,
        out_shape=out_shape,
        scratch_shapes=[pltpu.VMEM((tm, k), BF16)],
        compiler_params=_cparams("parallel", "arbitrary"),
        name="norm_linear",
    )(x, g.reshape(1, k), shift, scale, w_bf, *tabs)
    return outs


def _linear_out_kernel(*refs, mode, n_o):
    o_refs = refs[:n_o]
    w_ref, x_ref, gate_ref, out_ref = refs[n_o:]
    if mode == 'plain':
        o = o_refs[0][0]
    elif mode == 'sum3':
        o = (o_refs[0][0] + o_refs[1][0]) + o_refs[2][0]
    else:
        ng = n_o // 2
        lses = [r[0] for r in o_refs[ng:]]
        m = functools.reduce(jnp.maximum, lses)
        es = [jnp.exp(l - m) for l in lses]
        den = functools.reduce(lambda a, c: a + c, es)
        o = jnp.concatenate([(e / den) * r[0] for e, r in zip(es, o_refs[:ng])], axis=1)
    acc = jnp.dot(o.astype(BF16), w_ref[...], preferred_element_type=F32)
    out_ref[0] = x_ref[0] + gate_ref[0] * acc


def linear_out(o_list, w_bf, x, gate, mode, tm):
    b, t, d = x.shape
    per_row = gate.shape[1] != 1
    gate_spec = (pl.BlockSpec((1, tm, d), lambda bi, i: (bi, i, 0)) if per_row
                 else pl.BlockSpec((1, 1, d), lambda bi, i: (bi, 0, 0)))
    row_spec = lambda wd: pl.BlockSpec((1, tm, wd), lambda bi, i: (bi, i, 0))
    return pl.pallas_call(
        functools.partial(_linear_out_kernel, mode=mode, n_o=len(o_list)),
        grid=(b, t // tm),
        in_specs=[row_spec(o.shape[-1]) for o in o_list]
        + [pl.BlockSpec(w_bf.shape, lambda bi, i: (0, 0)), row_spec(d), gate_spec],
        out_specs=row_spec(d),
        out_shape=jax.ShapeDtypeStruct((b, t, d), F32),
        compiler_params=_cparams("parallel", "parallel"),
        name="linear_out",
    )(*o_list, w_bf, x, gate)


def _band_kernel(*refs, n_kv, group, tq, has_sink, has_gate, gate_col, want_lse):
    q_ref, kp_ref, kc_ref, vp_ref, vc_ref = refs[:5]
    pos = 5
    sink_ref = gate_ref = lse_ref = None
    if has_sink:
        sink_ref = refs[pos]
        pos += 1
    if has_gate:
        gate_ref = refs[pos]
        pos += 1
    o_ref = refs[pos]
    if want_lse:
        lse_ref = refs[pos + 1]
    no_prev = jnp.where(pl.program_id(2) == 0, 2 * tq, 0)
    rows = lax.broadcasted_iota(jnp.int32, (tq, 2 * tq), 0)
    cols = lax.broadcasted_iota(jnp.int32, (tq, 2 * tq), 1)
    mask = ((cols < tq) & (cols >= rows + no_prev)) | ((cols >= tq) & ((cols - tq) <= rows))
    for j in range(n_kv):
        ks = slice(j * HEAD_DIM, (j + 1) * HEAD_DIM)
        k = jnp.concatenate([kp_ref[0, :, ks], kc_ref[0, :, ks]], axis=0).astype(BF16)
        v = jnp.concatenate([vp_ref[0, :, ks], vc_ref[0, :, ks]], axis=0).astype(BF16)
        for g in range(group):
            h = j * group + g
            hs = slice(h * HEAD_DIM, (h + 1) * HEAD_DIM)
            q = (q_ref[0, :, hs] * ATTN_SCALE).astype(BF16)
            s = lax.dot_general(q, k, (((1,), (1,)), ((), ())), preferred_element_type=F32)
            s = jnp.where(mask, s, -jnp.inf)
            m = jnp.max(s, axis=-1, keepdims=True)
            if has_sink:
                m = jnp.maximum(m, sink_ref[h])
            e = jnp.exp(s - m)
            den = jnp.sum(e, axis=-1, keepdims=True)
            if has_sink:
                den = den + jnp.exp(sink_ref[h] - m)
            o = jnp.dot(e.astype(BF16), v, preferred_element_type=F32) / den
            if has_gate:
                o = gate_ref[0, :, gate_col + h:gate_col + h + 1] * o
            o_ref[0, :, hs] = o
            if want_lse:
                lse_ref[0, :, hs] = jnp.broadcast_to(m + jnp.log(den), (tq, HEAD_DIM))


def band_attention(qa, ka, va, *, dil, n_kv, group, tq, q_idx, k_idx, v_idx, sink=None, gate=None,
                   gate_col=0, want_lse=False):
    b, s = qa.shape[:2]
    l = s // dil
    qw, kw = n_kv * group * HEAD_DIM, n_kv * HEAD_DIM
    q_rs, k_rs, v_rs = qa.shape[2] // qw, ka.shape[2] // kw, va.shape[2] // kw
    q2, k2, v2 = (a.reshape(b, l, dil * a.shape[2]) for a in (qa, ka, va))
    prev = lambda i: jnp.maximum(i - 1, 0)
    in_specs = [pl.BlockSpec((1, tq, qw), lambda bi, r, i: (bi, i, r * q_rs + q_idx)),
                pl.BlockSpec((1, tq, kw), lambda bi, r, i: (bi, prev(i), r * k_rs + k_idx)),
                pl.BlockSpec((1, tq, kw), lambda bi, r, i: (bi, i, r * k_rs + k_idx)),
                pl.BlockSpec((1, tq, kw), lambda bi, r, i: (bi, prev(i), r * v_rs + v_idx)),
                pl.BlockSpec((1, tq, kw), lambda bi, r, i: (bi, i, r * v_rs + v_idx))]
    args = [q2, k2, k2, v2, v2]
    if sink is not None:
        in_specs.append(pl.BlockSpec(memory_space=pltpu.SMEM))
        args.append(sink.astype(F32))
    if gate is not None:
        in_specs.append(pl.BlockSpec((1, tq, gate.shape[2]), lambda bi, r, i: (bi, i, 0)))
        args.append(gate)
    o_spec = pl.BlockSpec((1, tq, qw), lambda bi, r, i: (bi, i, r))
    o_shape = jax.ShapeDtypeStruct((b, l, dil * qw), F32)
    res = pl.pallas_call(
        functools.partial(_band_kernel, n_kv=n_kv, group=group, tq=tq, has_sink=sink is not None,
                          has_gate=gate is not None, gate_col=gate_col, want_lse=want_lse),
        grid=(b, dil, l // tq),
        in_specs=in_specs,
        out_specs=[o_spec, o_spec] if want_lse else o_spec,
        out_shape=[o_shape, o_shape] if want_lse else o_shape,
        compiler_params=_cparams("parallel", "parallel", "arbitrary"),
        name="band_attention",
    )(*args)
    if want_lse:
        return res[0].reshape(b, s, qw), res[1].reshape(b, s, qw)
    return res.reshape(b, s, qw)


def _decode_kernel(*refs, reqs, n_kv, group, width, dil, has_sink, has_gate, want_lse):
    q_ref, knew_ref, vnew_ref, kcol_ref, vcol_ref, cache_ref = refs[:6]
    pos = 6
    sink_ref = gate_ref = lse_ref = None
    if has_sink:
        sink_ref = refs[pos]
        pos += 1
    if has_gate:
        gate_ref = refs[pos]
        pos += 1
    o_ref = refs[pos]
    pos += 1
    if want_lse:
        lse_ref = refs[pos]
        pos += 1
    newc_ref = refs[pos]
    lane_g = lax.broadcasted_iota(jnp.int32, (group, width), 1)
    lane_d = lax.broadcasted_iota(jnp.int32, (HEAD_DIM, width), 1)
    tap = ((width - lane_g) & (dil - 1)) == 0
    for r, j in [(r, j) for r in range(reqs) for j in range(n_kv)]:
        rows = slice(j * group, (j + 1) * group)
        k_t, v_t = cache_ref[r, 0, j], cache_ref[r, 1, j]
        q = q_ref[r, rows] * ATTN_SCALE
        sc = jnp.dot(q.astype(BF16), k_t.astype(BF16), preferred_element_type=F32)
        s_new = jnp.sum(q * knew_ref[r, j:j + 1], axis=-1, keepdims=True)
        sc = jnp.where(tap, sc, NEG_BIG)
        m = jnp.maximum(jnp.max(sc, axis=-1, keepdims=True), s_new)
        if has_sink:
            row_g = lax.broadcasted_iota(jnp.int32, (group, 1), 0)
            sink = jnp.zeros((group, 1), F32)
            for g in range(group):
                sink = jnp.where(row_g == g, sink_ref[j * group + g], sink)
            m = jnp.maximum(m, sink)
        p = jnp.where(tap, jnp.exp(sc - m), 0.0)
        p_new = jnp.exp(s_new - m)
        den = jnp.sum(p, axis=-1, keepdims=True) + p_new
        if has_sink:
            den = den + jnp.exp(sink - m)
        pv = lax.dot_general(p.astype(BF16), v_t.astype(BF16), (((1,), (1,)), ((), ())), preferred_element_type=F32)
        o = (pv + p_new * vnew_ref[r, j:j + 1]) / den
        if has_gate:
            o = gate_ref[r, rows] * o
        o_ref[r, rows] = o
        if want_lse:
            lse_ref[r, rows] = jnp.broadcast_to(m + jnp.log(den), (group, HEAD_DIM))
        newc_ref[r, 0, j] = jnp.where(lane_d == width - 1, kcol_ref[r, j], pltpu.roll(k_t, width - 1, 1))
        newc_ref[r, 1, j] = jnp.where(lane_d == width - 1, vcol_ref[r, j], pltpu.roll(v_t, width - 1, 1))


def decode_window_attention(q, k_new, v_new, cache, *, dil=1, sink=None, gate=None, want_lse=False):
    n, n_q, d = q.shape
    width, n_kv = cache.shape[1], cache.shape[3]
    group = n_q // n_kv
    cache_t = cache.transpose(0, 2, 3, 4, 1)
    reqs = max(1, min(DECODE_REQS_PER_STEP, DECODE_ROWS_PER_STEP // width))
    row_spec = lambda a: pl.BlockSpec((reqs,) + a.shape[1:], lambda bi: (bi,) + (0,) * (a.ndim - 1))
    k_col, v_col = k_new[..., None], v_new[..., None]
    args = [q, k_new, v_new, k_col, v_col, cache_t]
    in_specs = [row_spec(a) for a in args]
    if sink is not None:
        in_specs.append(pl.BlockSpec(memory_space=pltpu.SMEM))
        args.append(sink.astype(F32))
    if gate is not None:
        in_specs.append(row_spec(gate))
        args.append(gate)
    o_shape = jax.ShapeDtypeStruct((n, n_q, d), F32)
    out_shape = [o_shape] + ([o_shape] if want_lse else []) + [jax.ShapeDtypeStruct(cache_t.shape, F32)]
    res = pl.pallas_call(
        functools.partial(_decode_kernel, reqs=reqs, n_kv=n_kv, group=group, width=width, dil=dil,
                          has_sink=sink is not None, has_gate=gate is not None, want_lse=want_lse),
        grid=(n // reqs,),
        in_specs=in_specs,
        out_specs=[row_spec(s) for s in out_shape],
        out_shape=out_shape,
        compiler_params=_cparams("parallel"),
        name="decode_window_attention",
    )(*args)
    return tuple(res[:-1]) + (res[-1].transpose(0, 4, 1, 2, 3),)


CMP_PAD = LANES_V7X


def _cmp_select_kernel(q_ref, ck_ref, cv_ref, fold_ref, gate_ref, o_ref, sel_ref, *, tq, n_cmp, n_kv, group):
    i = pl.program_id(1)
    wide = (CMP_PAD, group * tq)
    n_idx = lax.broadcasted_iota(jnp.int32, wide, 0)
    qpos_w = i * tq + (lax.broadcasted_iota(jnp.int32, wide, 1) & (tq - 1))
    valid = (n_idx * CMP_STRIDE + (CMP_LEN - 1) <= qpos_w) & (n_idx < n_cmp)
    n_slc_pad = fold_ref.shape[0]
    blk = lax.broadcasted_iota(jnp.int32, (n_slc_pad, tq), 0)
    cur = lax.shift_right_logical(i * tq + lax.broadcasted_iota(jnp.int32, (n_slc_pad, tq), 1),
                                  SEL_BLOCK.bit_length() - 1)
    cand = (blk >= 1) & (blk <= cur - 2)
    forced = (blk == cur) | ((blk == cur - 1) & (cur >= 1)) | ((blk == 0) & (cur >= 2))
    for j in range(n_kv):
        q = jnp.concatenate(
            [(q_ref[0, :, (j * group + g) * HEAD_DIM:(j * group + g + 1) * HEAD_DIM] * ATTN_SCALE).astype(BF16)
             for g in range(group)], axis=0)
        s_t = lax.dot_general(ck_ref[0, j].astype(BF16), q, (((1,), (1,)), ((), ())), preferred_element_type=F32)
        s_t = jnp.where(valid, s_t, -jnp.inf)
        m = jnp.max(s_t, axis=0, keepdims=True)
        m = jnp.where(m == -jnp.inf, 0.0, m)
        e = jnp.exp(s_t - m)
        p_t = e / jnp.maximum(jnp.sum(e, axis=0, keepdims=True), 1e-30)
        cv = cv_ref[0, j].astype(BF16)
        p_sum = jnp.zeros((CMP_PAD, tq), F32)
        for g in range(group):
            h = j * group + g
            p_g = p_t[:, g * tq:(g + 1) * tq]
            p_sum = p_sum + p_g
            o = jnp.dot(p_g.T.astype(BF16), cv, preferred_element_type=F32)
            o_ref[0, :, h * HEAD_DIM:(h + 1) * HEAD_DIM] = gate_ref[0, :, h:h + 1] * o
        imp = jnp.dot(fold_ref[...], p_sum, preferred_element_type=F32, precision=lax.Precision.HIGHEST)
        sc = jnp.where(cand, imp, -jnp.inf)
        chosen = forced
        for _ in range(N_SEL - N_FORCED):
            best, bi = _first_argmax(sc, blk, n_slc_pad)
            hit = blk == bi
            chosen = chosen | (hit & (best > -jnp.inf))
            sc = jnp.where(hit, -jnp.inf, sc)
        sel_t = jnp.concatenate([jnp.where(chosen, 1.0, 0.0), jnp.zeros((CMP_PAD - n_slc_pad, tq), F32)], axis=0)
        sel_ref[0, j] = sel_t.T.astype(BF16)


def cmp_select_prompt(y_main, comp, gates, tq=LANES_V7X):
    b, t = y_main.shape[:2]
    n_cmp = comp.shape[1]
    n_slc = -(-t // SEL_BLOCK)
    n_slc_pad = -(-n_slc // 8) * 8
    grp = N_Q_C // N_KV_C
    ckv = jnp.pad(comp, ((0, 0), (0, CMP_PAD - n_cmp), (0, 0), (0, 0), (0, 0))).transpose(2, 0, 3, 1, 4)
    ratio, left = SEL_BLOCK // CMP_STRIDE, CMP_LEN // CMP_STRIDE - 1
    mi, bi = jnp.arange(CMP_PAD)[None, :], jnp.arange(n_slc_pad)[:, None]
    fold = ((mi >= ratio * bi - left) & (mi <= ratio * bi + ratio - 1) & (mi < n_cmp) & (bi < n_slc)).astype(F32)
    return pl.pallas_call(
        functools.partial(_cmp_select_kernel, tq=tq, n_cmp=n_cmp, n_kv=N_KV_C, group=grp),
        grid=(b, t // tq),
        in_specs=[pl.BlockSpec((1, tq, C_Q_WIDTH), lambda bi_, i: (bi_, i, 0)),
                  pl.BlockSpec((1, N_KV_C, CMP_PAD, HEAD_DIM), lambda bi_, i: (bi_, 0, 0, 0)),
                  pl.BlockSpec((1, N_KV_C, CMP_PAD, HEAD_DIM), lambda bi_, i: (bi_, 0, 0, 0)),
                  pl.BlockSpec(fold.shape, lambda bi_, i: (0, 0)),
                  pl.BlockSpec((1, tq, gates.shape[2]), lambda bi_, i: (bi_, i, 0))],
        out_specs=[pl.BlockSpec((1, tq, C_Q_WIDTH), lambda bi_, i: (bi_, i, 0)),
                   pl.BlockSpec((1, N_KV_C, tq, CMP_PAD), lambda bi_, i: (bi_, 0, i, 0))],
        out_shape=[jax.ShapeDtypeStruct((b, t, C_Q_WIDTH), F32),
                   jax.ShapeDtypeStruct((b, N_KV_C, t, CMP_PAD), BF16)],
        compiler_params=_cparams("parallel", "parallel"),
        name="cmp_select",
    )(y_main, ckv[0], ckv[1], fold, gates)


def _sel_kernel(q_ref, k_ref, v_ref, sel_ref, gate_ref, o_ref, *, tq, tk, n_kv, group, n_blocks, gate_col):
    i = pl.program_id(1)
    n_chunks = (i * tq + tq + tk - 1) // tk
    qpos = i * tq + lax.broadcasted_iota(jnp.int32, (tq, tk), 0)
    qs = [jnp.concatenate(
        [(q_ref[0, :, (j * group + g) * HEAD_DIM:(j * group + g + 1) * HEAD_DIM] * ATTN_SCALE).astype(BF16)
         for g in range(group)], axis=0) for j in range(n_kv)]

    def body(c, carry):
        k0 = pl.multiple_of(c * tk, tk)
        causal = (k0 + lax.broadcasted_iota(jnp.int32, (tq, tk), 1)) <= qpos
        blk_of_key = lax.shift_right_logical(k0 + lax.broadcasted_iota(jnp.int32, (n_blocks, tk), 1),
                                             SEL_BLOCK.bit_length() - 1)
        expand = (blk_of_key == lax.broadcasted_iota(jnp.int32, (n_blocks, tk), 0)).astype(BF16)
        out = []
        for j in range(n_kv):
            m, l, acc = carry[j]
            ks = slice(j * HEAD_DIM, (j + 1) * HEAD_DIM)
            k = k_ref[0, pl.ds(k0, tk), ks].astype(BF16)
            v = v_ref[0, pl.ds(k0, tk), ks].astype(BF16)
            mask = (jnp.dot(sel_ref[0, j], expand, preferred_element_type=F32) > 0.5) & causal
            s = lax.dot_general(qs[j], k, (((1,), (1,)), ((), ())), preferred_element_type=F32)
            s = jnp.where(mask[None], s.reshape(group, tq, tk), -jnp.inf).reshape(group * tq, tk)
            m_new = jnp.maximum(m, jnp.max(s, axis=-1, keepdims=True))
            p = jnp.exp(s - m_new)
            alpha = jnp.exp(m - m_new)
            l = alpha * l + jnp.sum(p, axis=-1, keepdims=True)
            acc = alpha * acc + jnp.dot(p.astype(BF16), v, preferred_element_type=F32)
            out.append((m_new, l, acc))
        return tuple(out)

    init = tuple((jnp.full((group * tq, 1), NEG_BIG, F32), jnp.zeros((group * tq, 1), F32),
                  jnp.zeros((group * tq, HEAD_DIM), F32)) for _ in range(n_kv))
    final = lax.fori_loop(0, n_chunks, body, init)
    for j in range(n_kv):
        _, l, acc = final[j]
        o = acc / l
        for g in range(group):
            h = j * group + g
            o_ref[0, :, h * HEAD_DIM:(h + 1) * HEAD_DIM] = (
                gate_ref[0, :, gate_col + h:gate_col + h + 1] * o[g * tq:(g + 1) * tq])


def selected_attention_prompt(q_rope, y_main, selmask, gates, *, k_idx, v_idx, gate_col):
    b, t, qw = q_rope.shape
    n_blocks = selmask.shape[-1]
    kw = N_KV_C * HEAD_DIM
    tq, tk = SEL_Q_TILE, SEL_K_TILE
    return pl.pallas_call(
        functools.partial(_sel_kernel, tq=tq, tk=tk, n_kv=N_KV_C, group=N_Q_C // N_KV_C,
                          n_blocks=n_blocks, gate_col=gate_col),
        grid=(b, t // tq),
        in_specs=[pl.BlockSpec((1, tq, qw), lambda bi, i: (bi, i, 0)),
                  pl.BlockSpec((1, t, kw), lambda bi, i: (bi, 0, k_idx)),
                  pl.BlockSpec((1, t, kw), lambda bi, i: (bi, 0, v_idx)),
                  pl.BlockSpec((1, N_KV_C, tq, n_blocks), lambda bi, i: (bi, 0, i, 0)),
                  pl.BlockSpec((1, tq, gates.shape[2]), lambda bi, i: (bi, i, 0))],
        out_specs=pl.BlockSpec((1, tq, qw), lambda bi, i: (bi, i, 0)),
        out_shape=jax.ShapeDtypeStruct((b, t, qw), F32),
        compiler_params=_cparams("parallel", "arbitrary"),
        name="selected_attention",
    )(q_rope, y_main, y_main, selmask, gates)


def _sel_sample_kernel(page_ref, q_ref, qbd_ref, knew_ref, vnew_ref, gate_ref, want_ref, *refs, n_sel):
    del page_ref
    n_q, grp, n_past = N_Q_C, N_Q_C // N_KV_C, n_sel - 1
    pages, o_ref = refs[:n_past * N_KV_C], refs[n_past * N_KV_C]

    def stacked(c):
        return jnp.concatenate(
            [jnp.concatenate([pages[s * N_KV_C + j][0, c, 0] for j in range(N_KV_C)], axis=0) for s in range(n_past)],
            axis=1).astype(BF16)

    width = n_past * PAGE_SIZE
    page_shift = PAGE_SIZE.bit_length() - 1
    sc = jnp.dot((qbd_ref[0] * ATTN_SCALE).astype(BF16), stacked(0), preferred_element_type=F32)
    lane = lax.broadcasted_iota(jnp.int32, (n_q, width), 1)
    key_half = lax.shift_right_logical(lane, SEL_BLOCK.bit_length() - 1) & (PAGE_SIZE // SEL_BLOCK - 1)
    slot_of_lane = lax.shift_right_logical(lax.broadcasted_iota(jnp.int32, (n_sel, width), 1), page_shift) + 1
    expand = (slot_of_lane == lax.broadcasted_iota(jnp.int32, (n_sel, width), 0)).astype(BF16)
    want = jnp.dot(want_ref[0].astype(BF16), expand, preferred_element_type=F32)
    mask = key_half.astype(F32) == want
    s_new = jnp.sum(q_ref[0] * ATTN_SCALE * knew_ref[0], axis=-1, keepdims=True)
    sc = jnp.where(mask, sc, NEG_BIG)
    m = jnp.maximum(jnp.max(sc, axis=-1, keepdims=True), s_new)
    p = jnp.where(mask, jnp.exp(sc - m), 0.0)
    p_new = jnp.exp(s_new - m)
    den = jnp.sum(p, axis=-1, keepdims=True) + p_new
    pv = lax.dot_general(p.astype(BF16), stacked(1), (((1,), (1,)), ((), ())), preferred_element_type=F32)
    head_of_row = lax.shift_right_logical(lax.broadcasted_iota(jnp.int32, (n_q, HEAD_DIM), 0), grp.bit_length() - 1)
    o = jnp.zeros((n_q, HEAD_DIM), F32)
    for j in range(N_KV_C):
        o = jnp.where(head_of_row == j, pv[:, j * HEAD_DIM:(j + 1) * HEAD_DIM], o)
    o_ref[0] = gate_ref[0] * ((o + p_new * vnew_ref[0]) / den)


def selected_attention_sample(q, k_new, v_new, gate, pool, page, want):
    n, n_q, d = q.shape
    grp = n_q // N_KV_C
    n_sel = want.shape[2]
    pool_t = pool.transpose(0, 2, 3, 4, 1)
    eye = jnp.repeat(jnp.eye(N_KV_C, dtype=q.dtype), grp, axis=0)
    q_bd = (q[:, :, None, :] * eye[None, :, :, None]).reshape(n, n_q, N_KV_C * d)
    row_spec = lambda w: pl.BlockSpec((1, n_q, w), lambda bi, pg: (bi, 0, 0))

    def page_spec(s, j):
        return pl.BlockSpec((1, 2, 1, d, PAGE_SIZE), lambda bi, pg: (pg[(bi * N_KV_C + j) * n_sel + s], 0, j, 0, 0))

    page_specs = [page_spec(s, j) for s in range(1, n_sel) for j in range(N_KV_C)]
    grid_spec = pltpu.PrefetchScalarGridSpec(
        num_scalar_prefetch=1,
        grid=(n,),
        in_specs=[row_spec(d), row_spec(N_KV_C * d), row_spec(d), row_spec(d), row_spec(1), row_spec(n_sel)] + page_specs,
        out_specs=row_spec(d),
    )
    return pl.pallas_call(
        functools.partial(_sel_sample_kernel, n_sel=n_sel),
        grid_spec=grid_spec,
        out_shape=jax.ShapeDtypeStruct((n, n_q, d), F32),
        compiler_params=_cparams("parallel"),
        name="selected_attention_sample",
    )(page, q, q_bd, k_new, v_new, gate, want, *([pool_t] * len(page_specs)))


def _half_proj_kernel(*refs, n_rows):
    n_pairs = N_KV_C // 2
    x_refs, w_ref, o_ref = refs[:2 * n_pairs], refs[2 * n_pairs], refs[2 * n_pairs + 1]
    out_w = 2 * 2 * CMP_HID
    for c in range(2):
        for p in range(n_pairs):
            x_ref = x_refs[c * n_pairs + p]
            acc = jnp.zeros((n_rows, out_w), F32)
            for sp in range(CMP_STRIDE // 2):
                lo = x_ref[0, pl.ds(2 * sp, n_rows, stride=CMP_STRIDE), :]
                hi = x_ref[0, pl.ds(2 * sp + 1, n_rows, stride=CMP_STRIDE), :]
                lhs = jnp.concatenate([lo, hi], axis=1).astype(BF16)
                acc = acc + jnp.dot(lhs, w_ref[c, sp], preferred_element_type=F32)
            o_ref[0, :, (c * n_pairs + p) * out_w:(c * n_pairs + p + 1) * out_w] = acc


def half_block_proj(rows_arr, col_idx, w_bd, tile_rows):
    b, l = rows_arr.shape[:2]
    n_half_tile = tile_rows // CMP_STRIDE
    out_w = 2 * 2 * N_KV_C * CMP_HID
    n_chunks = C_KV_WIDTH // LANES_V7X

    def chunk_spec(cp):
        return pl.BlockSpec((1, tile_rows, LANES_V7X), lambda bi, i: (bi, i, col_idx * n_chunks + cp))

    return pl.pallas_call(
        functools.partial(_half_proj_kernel, n_rows=n_half_tile),
        grid=(b, l // tile_rows),
        in_specs=[chunk_spec(cp) for cp in range(n_chunks)] + [pl.BlockSpec(w_bd.shape, lambda bi, i: (0, 0, 0, 0))],
        out_specs=pl.BlockSpec((1, n_half_tile, out_w), lambda bi, i: (bi, i, 0)),
        out_shape=jax.ShapeDtypeStruct((b, l // CMP_STRIDE, out_w), F32),
        compiler_params=_cparams("parallel", "parallel"),
        name="half_block_proj",
    )(*([rows_arr] * n_chunks), w_bd)


def _paged_half_proj_kernel(pt_ref, *refs, n_pages):
    del pt_ref
    n_pairs = N_KV_C // 2
    pages, w_ref, o_ref = refs[:n_pages], refs[n_pages], refs[n_pages + 1]
    rows = refs[n_pages + 2:]
    for j, page in enumerate(pages):
        for c in range(2):
            for p in range(n_pairs):
                x_t = page[0, c, 2 * p:2 * p + 2].reshape(2 * HEAD_DIM, PAGE_SIZE)
                rows[c * n_pairs + p][j * PAGE_SIZE:(j + 1) * PAGE_SIZE, :] = x_t.T
    n_rows = n_pages * PAGE_SIZE // CMP_STRIDE
    out_w = 2 * 2 * CMP_HID
    for c in range(2):
        for p in range(n_pairs):
            x_ref = rows[c * n_pairs + p]
            acc = jnp.zeros((n_rows, out_w), F32)
            for sp in range(CMP_STRIDE // 2):
                lo = x_ref[pl.ds(2 * sp, n_rows, stride=CMP_STRIDE), :]
                hi = x_ref[pl.ds(2 * sp + 1, n_rows, stride=CMP_STRIDE), :]
                lhs = jnp.concatenate([lo, hi], axis=1).astype(BF16)
                acc = acc + jnp.dot(lhs, w_ref[c, sp], preferred_element_type=F32)
            o_ref[0, :, (c * n_pairs + p) * out_w:(c * n_pairs + p + 1) * out_w] = acc


def paged_half_block_proj(pool, page_table, w_bd, n_pages):
    n, pages_per_req = page_table.shape
    pool_t = pool.transpose(0, 2, 3, 4, 1)
    steps = pages_per_req // n_pages
    n_half = n_pages * PAGE_SIZE // CMP_STRIDE
    out_w = 2 * 2 * N_KV_C * CMP_HID

    def page_spec(j):
        return pl.BlockSpec((1, 2, N_KV_C, HEAD_DIM, PAGE_SIZE),
                            lambda bi, i, pt: (pt[bi * pages_per_req + i * n_pages + j], 0, 0, 0, 0))

    grid_spec = pltpu.PrefetchScalarGridSpec(
        num_scalar_prefetch=1,
        grid=(n, steps),
        in_specs=[page_spec(j) for j in range(n_pages)] + [pl.BlockSpec(w_bd.shape, lambda bi, i, pt: (0, 0, 0, 0))],
        out_specs=pl.BlockSpec((1, n_half, out_w), lambda bi, i, pt: (bi, i, 0)),
        scratch_shapes=[pltpu.VMEM((n_pages * PAGE_SIZE, LANES_V7X), F32) for _ in range(C_KV_WIDTH // LANES_V7X)],
    )
    return pl.pallas_call(
        functools.partial(_paged_half_proj_kernel, n_pages=n_pages),
        grid_spec=grid_spec,
        out_shape=jax.ShapeDtypeStruct((n, pages_per_req * PAGE_SIZE // CMP_STRIDE, out_w), F32),
        compiler_params=_cparams("parallel", "arbitrary"),
        name="paged_half_block_proj",
    )(page_table.reshape(-1), *([pool_t] * n_pages), w_bd)


def _half_proj_weight(w1h):
    r = CMP_LEN // CMP_STRIDE
    eye = jnp.eye(2, dtype=F32)
    w = w1h.reshape(2, r, CMP_STRIDE // 2, 2, HEAD_DIM, CMP_HID)
    w = jnp.einsum('cjpldh,kq->cplkdqjh', w, eye)
    return w.reshape(2, CMP_STRIDE // 2, 4 * HEAD_DIM, 2 * r * CMP_HID).astype(BF16)


def _first_argmax(v, iota, size):
    m = jnp.max(v, axis=0, keepdims=True)
    return m, jnp.min(jnp.where(v == m, iota, size), axis=0, keepdims=True)


def _route_tokens(scores, bias):
    tm = scores.shape[1]
    per = N_EXPERTS // N_EXPERT_GROUPS
    biased = scores + bias
    iota_per = lax.broadcasted_iota(jnp.int32, (per, tm), 0)
    iota_grp = lax.broadcasted_iota(jnp.int32, (N_EXPERT_GROUPS, tm), 0)
    grp = jnp.zeros((N_EXPERT_GROUPS, tm), F32)
    slabs = [biased[g * per:(g + 1) * per] for g in range(N_EXPERT_GROUPS)]
    for g, v in enumerate(slabs):
        m1, i1 = _first_argmax(v, iota_per, per)
        m2 = jnp.max(jnp.where(iota_per == i1, -jnp.inf, v), axis=0, keepdims=True)
        grp = jnp.where(iota_grp == g, m1 + m2, grp)
    keep = jnp.zeros((N_EXPERT_GROUPS, tm), jnp.int32)
    for _ in range(TOPK_GROUPS):
        _, gi = _first_argmax(grp, iota_grp, N_EXPERT_GROUPS)
        keep = jnp.where(iota_grp == gi, 1, keep)
        grp = jnp.where(iota_grp == gi, -jnp.inf, grp)
    v = jnp.concatenate([jnp.where(keep[g:g + 1] > 0, slabs[g], -jnp.inf) for g in range(N_EXPERT_GROUPS)], axis=0)
    iota_e = lax.broadcasted_iota(jnp.int32, (N_EXPERTS, tm), 0)
    iota_k = lax.broadcasted_iota(jnp.int32, (TOP_K, tm), 0)
    eidx = jnp.zeros((TOP_K, tm), jnp.int32)
    ew = jnp.zeros((TOP_K, tm), F32)
    for k in range(TOP_K):
        _, ei = _first_argmax(v, iota_e, N_EXPERTS)
        hit = iota_e == ei
        eidx = jnp.where(iota_k == k, ei, eidx)
        ew = jnp.where(iota_k == k, jnp.sum(jnp.where(hit, scores, 0.0), axis=0, keepdims=True), ew)
        v = jnp.where(hit, -jnp.inf, v)
    return eidx, ew / jnp.sum(ew, axis=0, keepdims=True) * ROUTED_SCALE


def _moe_in_kernel(x_ref, g_ref, sh_ref, sc_ref, rwt_ref, rb_ref, sg_ref, su_ref, sd_ref,
                   h_ref, eidx_ref, ew_ref, shared_ref):
    h = _modulated_norm(x_ref[0], g_ref[...], sc_ref[0], sh_ref[0])
    logits_t = lax.dot_general(rwt_ref[...], h, (((1,), (1,)), ((), ())), preferred_element_type=F32,
                               precision=lax.Precision.HIGHEST)
    eidx, ew = _route_tokens(jax.nn.sigmoid(logits_t), rb_ref[...])
    eidx_ref[...] = eidx
    ew_ref[...] = ew
    h_ref[0] = h
    hb = h.astype(BF16)
    gate = jnp.dot(hb, sg_ref[...], preferred_element_type=F32)
    up = jnp.dot(hb, su_ref[...], preferred_element_type=F32)
    mid = (jax.nn.silu(gate) * up).astype(BF16)
    shared_ref[0] = jnp.dot(mid, sd_ref[...], preferred_element_type=F32)


def moe_in(x, g, shift, scale, router_w, router_b, sg_bf, su_bf, sd_bf, tm):
    b, t, k = x.shape
    per_row = shift.shape[1] != 1
    mod_spec = (pl.BlockSpec((1, tm, k), lambda bi, i: (bi, i, 0)) if per_row
                else pl.BlockSpec((1, 1, k), lambda bi, i: (bi, 0, 0)))
    row_spec = lambda wd: pl.BlockSpec((1, tm, wd), lambda bi, i: (bi, i, 0))
    full = lambda a: pl.BlockSpec(a.shape, lambda bi, i: (0,) * a.ndim)
    tiles = t // tm
    tok_spec = pl.BlockSpec((TOP_K, tm), lambda bi, i: (0, bi * tiles + i))
    rwt = router_w.T
    rb = router_b.astype(F32).reshape(N_EXPERTS, 1)
    return pl.pallas_call(
        _moe_in_kernel,
        grid=(b, tiles),
        in_specs=[row_spec(k), pl.BlockSpec((1, k), lambda bi, i: (0, 0)), mod_spec, mod_spec,
                  full(rwt), full(rb), full(sg_bf), full(su_bf), full(sd_bf)],
        out_specs=[row_spec(k), tok_spec, tok_spec, row_spec(k)],
        out_shape=[jax.ShapeDtypeStruct((b, t, k), F32), jax.ShapeDtypeStruct((TOP_K, b * t), jnp.int32),
                   jax.ShapeDtypeStruct((TOP_K, b * t), F32), jax.ShapeDtypeStruct((b, t, k), F32)],
        compiler_params=_cparams("parallel", "parallel"),
        name="moe_in",
    )(x, g.reshape(1, k), shift, scale, rwt, rb, sg_bf, su_bf, sd_bf)


def _gmm_kernel(blk_e_ref, n_used_ref, x_ref, wg_ref, wu_ref, wd_ref, o_ref, wg_s, wu_s, wd_s):
    blk = pl.program_id(0)
    prev_e = blk_e_ref[jnp.maximum(blk - 1, 0)]
    new_expert = (blk == 0) | (blk_e_ref[blk] != prev_e)

    @pl.when(new_expert)
    def _():
        wg_s[...] = wg_ref[0].astype(BF16)
        wu_s[...] = wu_ref[0].astype(BF16)
        wd_s[...] = wd_ref[0].astype(BF16)

    @pl.when(blk < n_used_ref[0])
    def _():
        x = x_ref[...].astype(BF16)
        gate = jnp.dot(x, wg_s[...], preferred_element_type=F32)
        up = jnp.dot(x, wu_s[...], preferred_element_type=F32)
        mid = (jax.nn.silu(gate) * up).astype(BF16)
        o_ref[...] = jnp.dot(mid, wd_s[...], preferred_element_type=F32)

    @pl.when(blk >= n_used_ref[0])
    def _():
        o_ref[...] = jnp.zeros_like(o_ref)


def grouped_experts(xs, blk_e, n_used, w_gate, w_up, w_down, layer):
    cap, k = xs.shape
    tm = MOE_TILE
    n_blk = cap // tm
    de = w_gate.shape[-1]
    grid_spec = pltpu.PrefetchScalarGridSpec(
        num_scalar_prefetch=2,
        grid=(n_blk,),
        in_specs=[pl.BlockSpec((tm, k), lambda i, be, nu: (i, 0)),
                  pl.BlockSpec((None, 1, k, de), lambda i, be, nu: (layer, be[i], 0, 0)),
                  pl.BlockSpec((None, 1, k, de), lambda i, be, nu: (layer, be[i], 0, 0)),
                  pl.BlockSpec((None, 1, de, k), lambda i, be, nu: (layer, be[i], 0, 0))],
        out_specs=pl.BlockSpec((tm, k), lambda i, be, nu: (i, 0)),
        scratch_shapes=[pltpu.VMEM((k, de), BF16), pltpu.VMEM((k, de), BF16), pltpu.VMEM((de, k), BF16)],
    )
    return pl.pallas_call(
        _gmm_kernel,
        grid_spec=grid_spec,
        out_shape=jax.ShapeDtypeStruct((cap, k), F32),
        compiler_params=_cparams("arbitrary"),
        name="grouped_experts",
    )(blk_e, n_used, xs, w_gate, w_up, w_down)


def _moe_out_kernel(x_ref, gate_ref, picked_ref, ew_ref, shared_ref, o_ref):
    routed = picked_ref[0] * ew_ref[:, 0:1]
    for k in range(1, TOP_K):
        routed = routed + picked_ref[k] * ew_ref[:, k:k + 1]
    o_ref[0] = x_ref[0] + gate_ref[0] * (routed + shared_ref[0])


def moe_out(x, gate, picked, ew, shared, tm, row0):
    b, t, d = x.shape
    per_row = gate.shape[1] != 1
    gate_spec = (pl.BlockSpec((1, tm, d), lambda bi, i: (bi, i, 0)) if per_row
                 else pl.BlockSpec((1, 1, d), lambda bi, i: (bi, 0, 0)))
    row_spec = pl.BlockSpec((1, tm, d), lambda bi, i: (bi, i, 0))
    tiles, off = t // tm, row0 // tm
    return pl.pallas_call(
        _moe_out_kernel,
        grid=(b, tiles),
        in_specs=[row_spec, gate_spec,
                  pl.BlockSpec((TOP_K, tm, d), lambda bi, i: (0, off + bi * tiles + i, 0)),
                  pl.BlockSpec((tm, TOP_K), lambda bi, i: (off + bi * tiles + i, 0)),
                  row_spec],
        out_specs=row_spec,
        out_shape=jax.ShapeDtypeStruct((b, t, d), F32),
        compiler_params=_cparams("parallel", "parallel"),
        name="moe_out",
    )(x, gate, picked, ew, shared)


def _final_norm_kernel(x_ref, g_ref, o_ref):
    x = x_ref[0]
    var = jnp.mean(x * x, axis=-1, keepdims=True)
    o_ref[0] = x * lax.rsqrt(var + RMS_EPS) * g_ref[...]


def final_norm(x, g, tm):
    b, t, d = x.shape
    row_spec = pl.BlockSpec((1, tm, d), lambda bi, i: (bi, i, 0))
    return pl.pallas_call(
        _final_norm_kernel,
        grid=(b, t // tm),
        in_specs=[row_spec, pl.BlockSpec((1, d), lambda bi, i: (0, 0))],
        out_specs=row_spec,
        out_shape=jax.ShapeDtypeStruct((b, t, d), F32),
        compiler_params=_cparams("parallel", "parallel"),
        name="final_norm",
    )(x, g.reshape(1, d))


def _ada_kernel(c_ref, w_ref, b_ref, o_ref):
    a = jax.nn.silu(c_ref[...]).astype(BF16)
    o_ref[0] = jnp.dot(a, w_ref[0].astype(BF16), preferred_element_type=F32) + b_ref[0]


def ada_mods(c_all, ada_w, ada_b):
    n, d = c_all.shape
    depth, _, width = ada_w.shape
    tn = 1024
    return pl.pallas_call(
        _ada_kernel,
        grid=(depth, width // tn),
        in_specs=[pl.BlockSpec((n, d), lambda l, j: (0, 0)),
                  pl.BlockSpec((1, d, tn), lambda l, j: (l, 0, j)),
                  pl.BlockSpec((1, 1, tn), lambda l, j: (l, 0, j))],
        out_specs=pl.BlockSpec((1, n, tn), lambda l, j: (l, 0, j)),
        out_shape=jax.ShapeDtypeStruct((depth, n, width), F32),
        compiler_params=_cparams("parallel", "parallel"),
        name="ada_mods",
    )(c_all, ada_w, ada_b.reshape(depth, 1, width))


def _masked_softmax(s, mask, sink=None):
    s = jnp.where(mask, s, -jnp.inf)
    m = jnp.max(s, axis=-1, keepdims=True)
    if sink is not None:
        m = jnp.maximum(m, sink)
    m = jnp.where(jnp.isfinite(m), m, 0.0)
    e = jnp.exp(s - m)
    den = jnp.sum(e, axis=-1, keepdims=True)
    if sink is not None:
        den = den + jnp.exp(sink - m)
    p = e / jnp.maximum(den, 1e-30)
    return p, (m + jnp.log(den))[..., 0]


def _cmp_to_sel(p, n_slc):
    ratio = SEL_BLOCK // CMP_STRIDE
    left = CMP_LEN // CMP_STRIDE - 1
    pad = [(0, 0)] * (p.ndim - 1) + [(left, ratio * n_slc - p.shape[-1])]
    pp = jnp.pad(p, pad)
    out = pp[..., 0:ratio * n_slc:ratio]
    for o in range(1, ratio + left):
        out = out + pp[..., o:o + ratio * n_slc:ratio]
    return out


def _select_blocks(imp, pos, n_slc):
    k_top = N_SEL - N_FORCED
    cur = pos // SEL_BLOCK
    j = jnp.arange(n_slc)
    cand = (j[None, :] >= 1) & (j[None, :] <= cur[:, None] - 2)
    sc = jnp.where(cand, imp, -jnp.inf)
    if n_slc < k_top:
        sc = jnp.pad(sc, ((0, 0), (0, 0), (0, 0), (0, k_top - n_slc)), constant_values=-jnp.inf)
    lane = jnp.arange(sc.shape[-1])
    vals, idx = [], []
    for _ in range(k_top):
        best = jnp.argmax(sc, axis=-1)
        vals.append(jnp.max(sc, axis=-1))
        idx.append(best)
        sc = jnp.where(lane == best[..., None], -jnp.inf, sc)
    vals, idx = jnp.stack(vals, axis=-1), jnp.stack(idx, axis=-1)
    forced = jnp.stack([cur, cur - 1, jnp.zeros_like(cur)], axis=-1)
    forced_ok = jnp.stack([cur >= 0, cur >= 1, cur >= 2], axis=-1)
    lead = imp.shape[:2]
    blocks = jnp.concatenate([jnp.broadcast_to(forced, lead + forced.shape), idx.astype(cur.dtype)], axis=-1)
    ok = jnp.concatenate([jnp.broadcast_to(forced_ok, lead + forced_ok.shape), vals > -jnp.inf], axis=-1)
    return jnp.clip(blocks, 0, n_slc - 1), ok


def _compress(hp, n_cmp, w1h, b1, w2, pe):
    b = hp.shape[0]
    r = CMP_LEN // CMP_STRIDE
    hp = hp.reshape(b, hp.shape[1], 2, N_KV_C, r, CMP_HID)
    pre = (jnp.einsum('cjsd,cjsdh->ch', pe.reshape(2, r, CMP_STRIDE, HEAD_DIM), w1h) + b1)[:, None, :]
    for j in range(r):
        pre = pre + hp[:, j:j + n_cmp, :, :, j]
    return jnp.einsum('bnckh,chd->bnckd', jax.nn.gelu(pre), w2)


def _compressed_attention(q_cmp, comp, pos):
    blk_end = jnp.arange(comp.shape[1]) * CMP_STRIDE + CMP_LEN - 1
    s = jnp.einsum('bqhgd,bnhd->bhgqn', q_cmp, comp[:, :, 0], preferred_element_type=F32) * ATTN_SCALE
    p_cmp, _ = _masked_softmax(s, blk_end[None, :] <= pos[:, None])
    o_cmp = jnp.einsum('bhgqn,bnhd->bqhgd', p_cmp.astype(comp.dtype), comp[:, :, 1])
    return o_cmp, p_cmp


A_SPECS = ((0, (N_Q_A + N_KV_A) * HEAD_DIM, 'rope', 0, 0),
           ((N_Q_A + N_KV_A) * HEAD_DIM, (N_Q_A + 2 * N_KV_A) * HEAD_DIM, 'raw', 0, (N_Q_A + N_KV_A) * HEAD_DIM))
B_SPECS = ((0, 2 * B_WIDTH, 'rope', 0, 0), (2 * B_WIDTH, 3 * B_WIDTH, 'raw', 0, 2 * B_WIDTH))
_C0 = C_Q_WIDTH + C_KV_WIDTH
_C1 = _C0 + C_KV_WIDTH
C_SPECS = ((0, _C0, 'raw', 0, 0),
           (_C0, _C0 + C_KV_WIDTH // 2, 'rope', 0, _C0), (_C0 + C_KV_WIDTH // 2, _C1, 'raw', 0, _C0 + C_KV_WIDTH // 2),
           (_C1, _C1 + C_KV_WIDTH // 2, 'rope', 0, _C1), (_C1 + C_KV_WIDTH // 2, C_MAIN_WIDTH, 'raw', 0, _C1 + C_KV_WIDTH // 2),
           (0, C_Q_WIDTH, 'rope', 1, 0),
           (C_MAIN_WIDTH, C_MAIN_WIDTH + N_GATES_C, 'sigmoid', 2, 0))
_AK = N_Q_A * HEAD_DIM
_AW = N_KV_A * HEAD_DIM
A_SPECS_P = A_SPECS + ((_AK, _AK + _AW, 'rope', 1, 0), (_AK + _AW, _AK + 2 * _AW, 'raw', 1, _AW))
_BW = B_HEADS_PER_GROUP * HEAD_DIM
B_SPECS_P = B_SPECS + tuple(
    seg for gi in range(N_GROUPS_B) for seg in
    ((B_WIDTH + gi * _BW, B_WIDTH + (gi + 1) * _BW, 'rope', 1 + gi, 0),
     (2 * B_WIDTH + gi * _BW, 2 * B_WIDTH + (gi + 1) * _BW, 'raw', 1 + gi, _BW)))
_CH = C_KV_WIDTH // 2
C_SPECS_P = C_SPECS + ((C_Q_WIDTH, _C0, 'raw', 3, 0),
                       (_C0, _C0 + _CH, 'rope', 4, 0), (_C0 + _CH, _C1, 'raw', 4, _CH),
                       (_C1, _C1 + _CH, 'rope', 5, 0), (_C1 + _CH, C_MAIN_WIDTH, 'raw', 5, _CH))


def _kv_from_cols(cols, n_kv, rows):
    b = cols.shape[0]
    return cols[:, :, cols.shape[2] - rows:].reshape(b, 2, n_kv, HEAD_DIM, rows).transpose(0, 4, 1, 2, 3)


def _mixer_a(x, mods, g, w_qkv_bf, w_o_bf, sink, tabs, tm, past):
    shift, scale, gate = mods
    nq = N_Q_A * HEAD_DIM
    kvw = N_KV_A * HEAD_DIM
    if past is None:
        y, kv_cols = norm_linear(x, g, shift, scale, w_qkv_bf, tabs, A_SPECS_P,
                                 ((nq + 2 * kvw, 'rows'), (2 * kvw, 'cols_last')), tm)
        o = band_attention(y, y, y, dil=1, n_kv=N_KV_A, group=N_Q_A // N_KV_A, tq=WINDOW_A,
                           q_idx=0, k_idx=nq // kvw, v_idx=nq // kvw + 1, sink=sink)
        new = _kv_from_cols(kv_cols, N_KV_A, min(WINDOW_A, y.shape[1]))
    else:
        (y,) = norm_linear(x, g, shift, scale, w_qkv_bf, tabs, A_SPECS, ((nq + 2 * kvw, 'rows'),), tm)
        n = y.shape[1]
        o, new = decode_window_attention(y[0, :, :nq].reshape(n, N_Q_A, HEAD_DIM),
                                         y[0, :, nq:nq + kvw].reshape(n, N_KV_A, HEAD_DIM),
                                         y[0, :, nq + kvw:].reshape(n, N_KV_A, HEAD_DIM), past, sink=sink)
        o = o.reshape(1, n, nq)
    return linear_out([o], w_o_bf, x, gate, 'plain', tm), new


def _mixer_b(x, mods, g, w_qkv_bf, w_o_bf, tabs, tm, past):
    shift, scale, gate = mods
    hpg = B_HEADS_PER_GROUP
    gw = hpg * HEAD_DIM
    outs, lses, news = [], [], []
    if past is None:
        t = x.shape[1]
        descs = [(3 * B_WIDTH, 'rows')] + [(2 * gw, 'cols_last' if win <= tm else 'cols') for win, _ in B_PATTERNS]
        y, *kv_cols = norm_linear(x, g, shift, scale, w_qkv_bf, tabs, B_SPECS_P, descs, tm)
        for gi, (win, dil) in enumerate(B_PATTERNS):
            o, lse = band_attention(y, y, y, dil=dil, n_kv=hpg, group=1, tq=win // dil,
                                    q_idx=gi, k_idx=N_GROUPS_B + gi, v_idx=2 * N_GROUPS_B + gi, want_lse=True)
            outs.append(o)
            lses.append(lse)
            news.append(_kv_from_cols(kv_cols[gi], hpg, min(win, t)))
    else:
        (y,) = norm_linear(x, g, shift, scale, w_qkv_bf, tabs, B_SPECS, ((3 * B_WIDTH, 'rows'),), tm)
        n = y.shape[1]
        for gi, (win, dil) in enumerate(B_PATTERNS):
            part = lambda c: y[0, :, c * B_WIDTH + gi * gw:c * B_WIDTH + (gi + 1) * gw].reshape(n, hpg, HEAD_DIM)
            o, lse, new = decode_window_attention(part(0), part(1), part(2), past[gi], dil=dil, want_lse=True)
            outs.append(o.reshape(1, n, gw))
            lses.append(lse.reshape(1, n, gw))
            news.append(new)
    return linear_out(outs + lses, w_o_bf, x, gate, 'mix3', tm), tuple(news)


def _mixer_c(x, mods, g, w_in_bf, w_o_bf, cmp_w1, cmp_b1, cmp_w2, cmp_pe, tabs, pos, tm, past):
    shift, scale, gate = mods
    grp = N_Q_C // N_KV_C
    kvw = N_KV_C * HEAD_DIM
    descs = [(C_MAIN_WIDTH, 'rows'), (C_Q_WIDTH, 'rows'), (N_GATES_C, 'rows')]
    w1h = cmp_w1.reshape(2, CMP_LEN // CMP_STRIDE, CMP_STRIDE, HEAD_DIM, CMP_HID)
    w_bd = _half_proj_weight(w1h)
    cmp_idx, slc_idx, win_idx = C_Q_WIDTH // C_KV_WIDTH, _C0 // C_KV_WIDTH, _C1 // C_KV_WIDTH
    if past is None:
        descs += [(C_KV_WIDTH, 'cols'), (C_KV_WIDTH, 'cols'), (C_KV_WIDTH, 'cols_last')]
        y, q_rope, gates, cmp_cols, slc_cols, win_cols = norm_linear(x, g, shift, scale, w_in_bf, tabs, C_SPECS_P,
                                                                     descs, tm)
        b, t = y.shape[:2]
        hp = half_block_proj(y, cmp_idx, w_bd, t)
        n_cmp = (t - CMP_LEN) // CMP_STRIDE + 1
        comp = _compress(hp, n_cmp, w1h, cmp_b1, cmp_w2, cmp_pe)
        o_cmp, selmask = cmp_select_prompt(y, comp, gates)
        o_slc = selected_attention_prompt(q_rope, y, selmask, gates, k_idx=2 * slc_idx, v_idx=2 * slc_idx + 1,
                                          gate_col=N_Q_C)
        o_win = band_attention(q_rope, y, y, dil=1, n_kv=N_KV_C, group=grp, tq=WINDOW_C, q_idx=0,
                               k_idx=2 * win_idx, v_idx=2 * win_idx + 1, gate=gates, gate_col=2 * N_Q_C)
        new_win = _kv_from_cols(win_cols, N_KV_C, min(WINDOW_C, t))
        cmp_kv = _kv_from_cols(cmp_cols, N_KV_C, t)
        slc_kv = _kv_from_cols(slc_cols, N_KV_C, t)
    else:
        y, q_rope, gates = norm_linear(x, g, shift, scale, w_in_bf, tabs, C_SPECS, descs, tm)
        win_buf, cmp_pool, slc_pool, page_table = past
        n = y.shape[1]
        t = 1
        cmp_kv = y[0, :, C_Q_WIDTH:_C0].reshape(n, t, 2, N_KV_C, HEAD_DIM)
        slc_kv = y[0, :, _C0:_C1].reshape(n, t, 2, N_KV_C, HEAD_DIM)
        win_kv = y[0, :, _C1:].reshape(n, t, 2, N_KV_C, HEAD_DIM)
        q_cmp = y[0, :, :C_Q_WIDTH].reshape(n, t, N_KV_C, grp, HEAD_DIM)
        gview = gates[0].reshape(n, t, 3, N_KV_C, grp, 1)
        hp = paged_half_block_proj(cmp_pool, page_table, w_bd, CMP_PAGES_PER_STEP)
        length = PAST_LEN + t
        n_cmp = (length - CMP_LEN) // CMP_STRIDE + 1
        comp = _compress(hp, n_cmp, w1h, cmp_b1, cmp_w2, cmp_pe)
        o_cmp, p_cmp = _compressed_attention(q_cmp, comp, pos)
        n_slc = -(-length // SEL_BLOCK)
        blocks, ok = _select_blocks(_cmp_to_sel(p_cmp.sum(axis=2), n_slc), pos, n_slc)
        sub = PAGE_SIZE // SEL_BLOCK
        pb = jnp.minimum(blocks[:, :, 0], PAST_LEN // SEL_BLOCK - 1)
        page = jnp.take_along_axis(jnp.broadcast_to(page_table[:, None], (n, N_KV_C, page_table.shape[1])),
                                   pb // sub, axis=2)
        want = jnp.where(ok[:, :, 0], pb % sub, -1).astype(F32)
        o_slc = selected_attention_sample(
            q_rope[0].reshape(n, N_Q_C, HEAD_DIM), jnp.repeat(slc_kv[:, 0, 0], grp, axis=1),
            jnp.repeat(slc_kv[:, 0, 1], grp, axis=1), gates[0, :, N_Q_C:2 * N_Q_C].reshape(n, N_Q_C, 1),
            slc_pool, page.reshape(-1).astype(jnp.int32), jnp.repeat(want, grp, axis=1))
        o_slc = o_slc.reshape(1, n, C_Q_WIDTH)
        o_win, new_win = decode_window_attention(
            q_rope[0].reshape(n, N_Q_C, HEAD_DIM), win_kv[:, 0, 0], win_kv[:, 0, 1], win_buf,
            gate=gates[0, :, 2 * N_Q_C:].reshape(n, N_Q_C, 1))
        o_win = o_win.reshape(1, n, C_Q_WIDTH)
        o_cmp = (gview[:, :, 0] * o_cmp.astype(F32)).reshape(1, n, C_Q_WIDTH)
    return linear_out([o_cmp, o_slc, o_win], w_o_bf, x, gate, 'sum3', tm), (new_win, cmp_kv, slc_kv)


def _dispatch_plan(eidx, n_blk):
    n = eidx.shape[0]
    tm = MOE_TILE
    chunk = LANES_V7X
    onehot = (eidx[:, :, None] == jnp.arange(N_EXPERTS)).astype(jnp.int32)
    sel = onehot.sum(axis=1).astype(F32).reshape(n // chunk, chunk, N_EXPERTS)
    tril = jnp.tril(jnp.ones((chunk, chunk), F32))
    within = jnp.einsum('ij,cjk->cik', tril, sel)
    chunk_tot = within[:, -1, :]
    before = jnp.cumsum(chunk_tot, axis=0) - chunk_tot
    rank = (within - sel + before[:, None, :]).reshape(n, N_EXPERTS).astype(jnp.int32)
    counts = jnp.sum(chunk_tot, axis=0).astype(jnp.int32)
    padded = (counts + tm - 1) // tm * tm
    ends = jnp.cumsum(padded)
    dest = jnp.sum(onehot * (ends - padded + rank)[:, None, :], axis=-1)
    blk_start = jnp.arange(n_blk, dtype=jnp.int32) * tm
    blk_e = jnp.minimum(jnp.sum(ends[None, :] <= blk_start[:, None], axis=1), N_EXPERTS - 1).astype(jnp.int32)
    n_used = (ends[-1] // tm).astype(jnp.int32).reshape(1)
    keys = (eidx.astype(jnp.int32) * n + jnp.arange(n, dtype=jnp.int32)[:, None]).reshape(-1)
    sorted_tok = jnp.sort(keys) % n
    shift = (ends - padded) - (jnp.cumsum(counts) - counts)
    dense = jnp.arange(n_blk * tm, dtype=jnp.int32) - jnp.repeat(shift[blk_e], tm)
    row_tok = sorted_tok[jnp.clip(dense, 0, n * TOP_K - 1)]
    return dest, row_tok, blk_e, n_used


def _moe(groups, g, router_w, router_b, w_gate, w_up, w_down, layer, sg_bf, su_bf, sd_bf):
    d = groups[0][0].shape[-1]
    fronts = [moe_in(x, g, mods[0], mods[1], router_w, router_b, sg_bf, su_bf, sd_bf, tm) for x, mods, tm in groups]
    join = lambda parts, axis: parts[0] if len(parts) == 1 else jnp.concatenate(parts, axis=axis)
    h_all = join([f[0].reshape(-1, d) for f in fronts], 0)
    eidx = join([f[1] for f in fronts], 1).T
    ew = join([f[2] for f in fronts], 1).T
    n = h_all.shape[0]
    n_blk = (n * TOP_K + N_EXPERTS * (MOE_TILE - 1)) // MOE_TILE + 1
    dest, row_tok, blk_e, n_used = _dispatch_plan(eidx, n_blk)
    ys = grouped_experts(h_all[row_tok], blk_e, n_used, w_gate, w_up, w_down, layer)
    picked = ys[dest.T.reshape(-1)].reshape(TOP_K, n, d)
    outs, row0 = [], 0
    for (x, mods, tm), front in zip(groups, fronts):
        outs.append(moe_out(x, mods[2], picked, ew, front[3], min(tm, MOE_OUT_TILE), row0))
        row0 += x.shape[0] * x.shape[1]
    return outs


def kernel(x_prompt, x_sample, c_prompt, c_sample, cache_a_kv, cache_b_kv_w128, cache_b_kv_w512, cache_b_kv_w2048, cache_c_win_kv, cache_c_cmp_kv, cache_c_slc_kv, page_table, norm_g, final_g, ada_w, ada_b, a_w_qkv, a_w_o, a_sink, b_w_qkv, b_w_o, c_w_in, c_w_o, c_cmp_w1, c_cmp_b1, c_cmp_w2, c_cmp_pe, moe_router, moe_bias, moe_w_gate, moe_w_up, moe_w_down, shared_w_gate, shared_w_up, shared_w_down):
    bp, seq, d = x_prompt.shape
    ns = x_sample.shape[0]
    b_caches = (cache_b_kv_w128, cache_b_kv_w512, cache_b_kv_w2048)
    pos_p = jnp.arange(seq, dtype=jnp.int32)
    pos_s = PAST_LEN + jnp.arange(x_sample.shape[1], dtype=jnp.int32)
    tabs_p = _rope_tables(pos_p, seq)
    tabs_s = _rope_tables(pos_s, ns)

    mods = ada_mods(jnp.concatenate([c_prompt, c_sample], axis=0), ada_w, ada_b)
    mods_p = mods[:, :bp].reshape(DEPTH, bp, 6, 1, d)
    mods_s = mods[:, bp:].reshape(DEPTH, 1, ns, 6, d)

    per = bp // PROMPT_CHAINS
    chains = [slice(c * per, (c + 1) * per) for c in range(PROMPT_CHAINS)]
    xps = [x_prompt[c] for c in chains]
    xs = x_sample.reshape(1, ns, d)
    st_p = {0: [], 1: [], 2: []}
    st_s = {0: [], 1: [], 2: []}
    for l in range(DEPTH):
        kind, slot = LAYER_KIND[l], LAYER_SLOT[l]
        mps = [[mods_p[l, c, i] for i in range(6)] for c in chains]
        ms = [mods_s[l, :, :, i] for i in range(6)]
        g_mix, g_moe = norm_g[l, 0], norm_g[l, 1]
        if kind == 0:
            w_in, w_o = a_w_qkv[slot].astype(BF16), a_w_o[slot].astype(BF16)
            res = [_mixer_a(xp, mp[:3], g_mix, w_in, w_o, a_sink[slot], tabs_p, ROW_TILE, None) for xp, mp in zip(xps, mps)]
            xs, ss = _mixer_a(xs, ms[:3], g_mix, w_in, w_o, a_sink[slot], tabs_s, ns, cache_a_kv[slot])
        elif kind == 1:
            w_in, w_o = b_w_qkv[slot].astype(BF16), b_w_o[slot].astype(BF16)
            res = [_mixer_b(xp, mp[:3], g_mix, w_in, w_o, tabs_p, ROW_TILE, None) for xp, mp in zip(xps, mps)]
            xs, ss = _mixer_b(xs, ms[:3], g_mix, w_in, w_o, tabs_s, ns, tuple(buf[slot] for buf in b_caches))
        else:
            w_in, w_o = c_w_in[slot].astype(BF16), c_w_o[slot].astype(BF16)
            cargs = (c_cmp_w1[slot], c_cmp_b1[slot], c_cmp_w2[slot], c_cmp_pe[slot])
            res = [_mixer_c(xp, mp[:3], g_mix, w_in, w_o, *cargs, tabs_p, pos_p, ROW_TILE, None) for xp, mp in zip(xps, mps)]
            xs, ss = _mixer_c(xs, ms[:3], g_mix, w_in, w_o, *cargs, tabs_s, pos_s, ns,
                              (cache_c_win_kv[slot], cache_c_cmp_kv[slot], cache_c_slc_kv[slot], page_table))
        xps = [r[0] for r in res]
        states = [r[1] for r in res]
        cat = lambda parts: jnp.concatenate(parts, axis=0)
        st_p[kind].append(tuple(cat([s[i] for s in states]) for i in range(len(states[0])))
                          if isinstance(states[0], tuple) else cat(states))
        st_s[kind].append(ss)
        moe_args = (g_moe, moe_router[l], moe_bias[l], moe_w_gate, moe_w_up, moe_w_down, l,
                    shared_w_gate[l].astype(BF16), shared_w_up[l].astype(BF16), shared_w_down[l].astype(BF16))
        for c in range(PROMPT_CHAINS - 1):
            (xps[c],) = _moe([(xps[c], mps[c][3:], ROW_TILE)], *moe_args)
        xps[-1], xs = _moe([(xps[-1], mps[-1][3:], ROW_TILE), (xs, ms[3:], ns)], *moe_args)
    y_prompt = jnp.concatenate([final_norm(xp, final_g, ROW_TILE) for xp in xps], axis=0)
    y_sample = final_norm(xs, final_g, ns).reshape(x_sample.shape)

    outs = [y_prompt, y_sample, jnp.stack(st_p[0]), jnp.stack(st_s[0])]
    for i in range(N_GROUPS_B):
        outs += [jnp.stack([s[i] for s in st_p[1]]), jnp.stack([s[i] for s in st_s[1]])]
    outs += [jnp.stack([s[0] for s in st_p[2]]), jnp.stack([s[0] for s in st_s[2]])]
    outs += [jnp.stack([s[1] for s in st_p[2]]), jnp.stack([s[1] for s in st_s[2]])]
    outs += [jnp.stack([s[2] for s in st_p[2]]), jnp.stack([s[2] for s in st_s[2]])]
    return tuple(outs)
```

```python
import functools

import jax
import jax.numpy as jnp
from jax import lax
from jax.experimental import pallas as pl
from jax.experimental.pallas import tpu as pltpu

D_MODEL = 1024
DEPTH = 4
PAST_LEN = 8192
PAGE_SIZE = 128
HEAD_DIM = 64
ROT_DIM = HEAD_DIM // 4
ROPE_THETA = 500000.0
ATTN_SCALE = HEAD_DIM ** -0.5
RMS_EPS = 1e-6

N_MIXERS = 3
LAYER_KIND = tuple(i % N_MIXERS for i in range(DEPTH))
LAYER_SLOT = tuple(LAYER_KIND[:i].count(LAYER_KIND[i]) for i in range(DEPTH))

N_Q_A = 16
N_KV_A = 4
WINDOW_A = 128
B_PATTERNS = ((128, 1), (512, 4), (2048, 16))
N_GROUPS_B = len(B_PATTERNS)
B_HEADS_PER_GROUP = 4
B_WIDTH = N_GROUPS_B * B_HEADS_PER_GROUP * HEAD_DIM
N_Q_C = 16
N_KV_C = 4
CMP_LEN = 32
CMP_STRIDE = 16
CMP_HID = 64
SEL_BLOCK = 64
N_SEL = 16
N_FORCED = 3
WINDOW_C = 512
C_KV_WIDTH = 2 * N_KV_C * HEAD_DIM
C_Q_WIDTH = N_Q_C * HEAD_DIM
C_MAIN_WIDTH = C_Q_WIDTH + 3 * C_KV_WIDTH
N_GATES_C = 3 * N_Q_C

N_EXPERTS = 64
TOP_K = 8
N_EXPERT_GROUPS = 8
TOPK_GROUPS = 4
D_EXPERT = 256
ROUTED_SCALE = 2.5

LANES_V7X = 128
VMEM_LIMIT_V7X = 56 * 1024 * 1024

ROW_TILE = 512
COL_GROUP = 512
MOE_TILE = 512
MOE_OUT_TILE = 256
BAND_STACK_ROWS = 512
CMP_PAGES_PER_STEP = 32
PROMPT_CHAINS = 1
DECODE_REQS_PER_STEP = 4
DECODE_ROWS_PER_STEP = 2048
SEL_Q_TILE = 128
SEL_K_TILE = 256
NEG_BIG = -1e30

BF16 = jnp.bfloat16
F32 = jnp.float32


def _cparams(*sem):
    return pltpu.CompilerParams(dimension_semantics=sem, vmem_limit_bytes=VMEM_LIMIT_V7X)


def _rope_tables(pos, rows):
    half = ROT_DIM // 2
    inv_freq = ROPE_THETA ** (-jnp.arange(half, dtype=F32) / half)
    ang = pos.astype(F32)[:, None] * inv_freq
    cos, sin = jnp.cos(ang), jnp.sin(ang)
    t = pos.shape[0]
    z8 = jnp.zeros((t, half), F32)
    rest1 = jnp.ones((t, HEAD_DIM - ROT_DIM), F32)
    rest0 = jnp.zeros((t, HEAD_DIM - ROT_DIM), F32)
    reps = LANES_V7X // HEAD_DIM
    cos_t = jnp.tile(jnp.concatenate([cos, cos, rest1], axis=1), (1, reps))
    sin_a = jnp.tile(jnp.concatenate([-sin, z8, rest0], axis=1), (1, reps))
    sin_b = jnp.tile(jnp.concatenate([z8, sin, rest0], axis=1), (1, reps))
    return tuple(jnp.broadcast_to(a, (rows, LANES_V7X)) for a in (cos_t, sin_a, sin_b))


def _rope_chunk(blk, cos_t, sin_a, sin_b):
    return (blk * cos_t + pltpu.roll(blk, LANES_V7X - ROT_DIM // 2, 1) * sin_a
            + pltpu.roll(blk, ROT_DIM // 2, 1) * sin_b)


def _build_plan(n_cols, specs):
    plan = []
    for c0 in range(0, n_cols, COL_GROUP):
        width = min(COL_GROUP, n_cols - c0)
        segs = []
        for (s0, s1, mode, oi, d0) in specs:
            lo, hi = max(s0, c0), min(s1, c0 + width)
            if lo >= hi:
                continue
            step = LANES_V7X if mode == 'rope' else hi - lo
            for a in range(lo, hi, step):
                segs.append((a - c0, min(step, hi - a), mode, oi, d0 + a - s0))
        plan.append((c0, width, tuple(segs)))
    return tuple(plan)


def _modulated_norm(x, g, scale, shift):
    var = jnp.mean(x * x, axis=-1, keepdims=True)
    y = x * lax.rsqrt(var + RMS_EPS) * g
    return y * (1 + scale) + shift


def _norm_linear_kernel(x_ref, g_ref, sh_ref, sc_ref, w_ref, cos_ref, sa_ref, sb_ref, *refs, plan, layouts):
    outs, h_ref = refs[:len(layouts)], refs[len(layouts)]
    h_ref[...] = _modulated_norm(x_ref[0], g_ref[...], sc_ref[0], sh_ref[0]).astype(BF16)
    for (c0, width, segs) in plan:
        acc = jnp.dot(h_ref[...], w_ref[:, c0:c0 + width], preferred_element_type=F32)
        for (off, wd, mode, oi, dst) in segs:
            blk = acc[:, off:off + wd]
            if mode == 'rope':
                blk = _rope_chunk(blk, cos_ref[...], sa_ref[...], sb_ref[...])
            elif mode == 'sigmoid':
                blk = jax.nn.sigmoid(blk)
            if layouts[oi] == 'rows':
                outs[oi][0, :, dst:dst + wd] = blk
            else:
                outs[oi][0, dst:dst + wd, :] = blk.T


def norm_linear(x, g, shift, scale, w_bf, tabs, specs, out_descs, tm):
    b, t, k = x.shape
    n = w_bf.shape[1]
    per_row = shift.shape[1] != 1
    mod_spec = (pl.BlockSpec((1, tm, k), lambda bi, i: (bi, i, 0)) if per_row
                else pl.BlockSpec((1, 1, k), lambda bi, i: (bi, 0, 0)))
    tab_spec = pl.BlockSpec((tm, LANES_V7X), lambda bi, i: (i, 0))
    plan = _build_plan(n, specs)
    out_specs, out_shape = [], []
    for wd, layout in out_descs:
        if layout == 'rows':
            out_specs.append(pl.BlockSpec((1, tm, wd), lambda bi, i: (bi, i, 0)))
            out_shape.append(jax.ShapeDtypeStruct((b, t, wd), F32))
        elif layout == 'cols':
            out_specs.append(pl.BlockSpec((1, wd, tm), lambda bi, i: (bi, 0, i)))
            out_shape.append(jax.ShapeDtypeStruct((b, wd, t), F32))
        else:
            out_specs.append(pl.BlockSpec((1, wd, tm), lambda bi, i: (bi, 0, 0)))
            out_shape.append(jax.ShapeDtypeStruct((b, wd, tm), F32))
    outs = pl.pallas_call(
        functools.partial(_norm_linear_kernel, plan=plan, layouts=tuple(l for _, l in out_descs)),
        grid=(b, t // tm),
        in_specs=[pl.BlockSpec((1, tm, k), lambda bi, i: (bi, i, 0)),
                  pl.BlockSpec((1, k), lambda bi, i: (0, 0)),
                  mod_spec, mod_spec,
                  pl.BlockSpec((k, n), lambda bi, i: (0, 0)),
                  tab_spec, tab_spec, tab_spec],
        out_specs=out_specs,
        out_shape=out_shape,
        scratch_shapes=[pltpu.VMEM((tm, k), BF16)],
        compiler_params=_cparams("parallel", "arbitrary"),
        name="norm_linear",
    )(x, g.reshape(1, k), shift, scale, w_bf, *tabs)
    return outs


def _linear_out_kernel(*refs, mode, n_o):
    o_refs = refs[:n_o]
    w_ref, x_ref, gate_ref, out_ref = refs[n_o:]
    if mode == 'plain':
        o = o_refs[0][0]
    elif mode == 'sum3':
        o = (o_refs[0][0] + o_refs[1][0]) + o_refs[2][0]
    else:
        ng = n_o // 2
        lses = [r[0] for r in o_refs[ng:]]
        m = functools.reduce(jnp.maximum, lses)
        es = [jnp.exp(l - m) for l in lses]
        den = functools.reduce(lambda a, c: a + c, es)
        o = jnp.concatenate([(e / den) * r[0] for e, r in zip(es, o_refs[:ng])], axis=1)
    acc = jnp.dot(o.astype(BF16), w_ref[...], preferred_element_type=F32)
    out_ref[0] = x_ref[0] + gate_ref[0] * acc


def linear_out(o_list, w_bf, x, gate, mode, tm):
    b, t, d = x.shape
    per_row = gate.shape[1] != 1
    gate_spec = (pl.BlockSpec((1, tm, d), lambda bi, i: (bi, i, 0)) if per_row
                 else pl.BlockSpec((1, 1, d), lambda bi, i: (bi, 0, 0)))
    row_spec = lambda wd: pl.BlockSpec((1, tm, wd), lambda bi, i: (bi, i, 0))
    return pl.pallas_call(
        functools.partial(_linear_out_kernel, mode=mode, n_o=len(o_list)),
        grid=(b, t // tm),
        in_specs=[row_spec(o.shape[-1]) for o in o_list]
        + [pl.BlockSpec(w_bf.shape, lambda bi, i: (0, 0)), row_spec(d), gate_spec],
        out_specs=row_spec(d),
        out_shape=jax.ShapeDtypeStruct((b, t, d), F32),
        compiler_params=_cparams("parallel", "parallel"),
        name="linear_out",
    )(*o_list, w_bf, x, gate)


def _band_kernel(*refs, n_kv, group, tq, has_sink, has_gate, gate_col, want_lse):
    q_ref, kp_ref, kc_ref, vp_ref, vc_ref = refs[:5]
    pos = 5
    sink_ref = gate_ref = lse_ref = None
    if has_sink:
        sink_ref = refs[pos]
        pos += 1
    if has_gate:
        gate_ref = refs[pos]
        pos += 1
    o_ref = refs[pos]
    if want_lse:
        lse_ref = refs[pos + 1]
    no_prev = jnp.where(pl.program_id(2) == 0, 2 * tq, 0)
    rows = lax.broadcasted_iota(jnp.int32, (tq, 2 * tq), 0)
    cols = lax.broadcasted_iota(jnp.int32, (tq, 2 * tq), 1)
    mask = ((cols < tq) & (cols >= rows + no_prev)) | ((cols >= tq) & ((cols - tq) <= rows))
    for j in range(n_kv):
        ks = slice(j * HEAD_DIM, (j + 1) * HEAD_DIM)
        k = jnp.concatenate([kp_ref[0, :, ks], kc_ref[0, :, ks]], axis=0).astype(BF16)
        v = jnp.concatenate([vp_ref[0, :, ks], vc_ref[0, :, ks]], axis=0).astype(BF16)
        per = max(1, min(group, BAND_STACK_ROWS // tq))
        for g0 in range(0, group, per):
            h0 = j * group + g0
            q = jnp.concatenate(
                [(q_ref[0, :, (h0 + g) * HEAD_DIM:(h0 + g + 1) * HEAD_DIM] * ATTN_SCALE).astype(BF16)
                 for g in range(per)], axis=0)
            s = lax.dot_general(q, k, (((1,), (1,)), ((), ())), preferred_element_type=F32)
            s = jnp.where(mask[None], s.reshape(per, tq, 2 * tq), -jnp.inf).reshape(per * tq, 2 * tq)
            m = jnp.max(s, axis=-1, keepdims=True)
            if has_sink:
                row_head = lax.broadcasted_iota(jnp.int32, (per, tq, 1), 0).reshape(per * tq, 1)
                sink = jnp.zeros((per * tq, 1), F32)
                for g in range(per):
                    sink = jnp.where(row_head == g, sink_ref[h0 + g], sink)
                m = jnp.maximum(m, sink)
            e = jnp.exp(s - m)
            den = jnp.sum(e, axis=-1, keepdims=True)
            if has_sink:
                den = den + jnp.exp(sink - m)
            o_all = jnp.dot(e.astype(BF16), v, preferred_element_type=F32) / den
            lse_all = m + jnp.log(den)
            for g in range(per):
                h = h0 + g
                hs = slice(h * HEAD_DIM, (h + 1) * HEAD_DIM)
                o = o_all[g * tq:(g + 1) * tq]
                if has_gate:
                    o = gate_ref[0, :, gate_col + h:gate_col + h + 1] * o
                o_ref[0, :, hs] = o
                if want_lse:
                    lse_ref[0, :, hs] = jnp.broadcast_to(lse_all[g * tq:(g + 1) * tq], (tq, HEAD_DIM))


def band_attention(qa, ka, va, *, dil, n_kv, group, tq, q_idx, k_idx, v_idx, sink=None, gate=None,
                   gate_col=0, want_lse=False):
    b, s = qa.shape[:2]
    l = s // dil
    qw, kw = n_kv * group * HEAD_DIM, n_kv * HEAD_DIM
    q_rs, k_rs, v_rs = qa.shape[2] // qw, ka.shape[2] // kw, va.shape[2] // kw
    q2, k2, v2 = (a.reshape(b, l, dil * a.shape[2]) for a in (qa, ka, va))
    prev = lambda i: jnp.maximum(i - 1, 0)
    in_specs = [pl.BlockSpec((1, tq, qw), lambda bi, r, i: (bi, i, r * q_rs + q_idx)),
                pl.BlockSpec((1, tq, kw), lambda bi, r, i: (bi, prev(i), r * k_rs + k_idx)),
                pl.BlockSpec((1, tq, kw), lambda bi, r, i: (bi, i, r * k_rs + k_idx)),
                pl.BlockSpec((1, tq, kw), lambda bi, r, i: (bi, prev(i), r * v_rs + v_idx)),
                pl.BlockSpec((1, tq, kw), lambda bi, r, i: (bi, i, r * v_rs + v_idx))]
    args = [q2, k2, k2, v2, v2]
    if sink is not None:
        in_specs.append(pl.BlockSpec(memory_space=pltpu.SMEM))
        args.append(sink.astype(F32))
    if gate is not None:
        in_specs.append(pl.BlockSpec((1, tq, gate.shape[2]), lambda bi, r, i: (bi, i, 0)))
        args.append(gate)
    o_spec = pl.BlockSpec((1, tq, qw), lambda bi, r, i: (bi, i, r))
    o_shape = jax.ShapeDtypeStruct((b, l, dil * qw), F32)
    res = pl.pallas_call(
        functools.partial(_band_kernel, n_kv=n_kv, group=group, tq=tq, has_sink=sink is not None,
                          has_gate=gate is not None, gate_col=gate_col, want_lse=want_lse),
        grid=(b, dil, l // tq),
        in_specs=in_specs,
        out_specs=[o_spec, o_spec] if want_lse else o_spec,
        out_shape=[o_shape, o_shape] if want_lse else o_shape,
        compiler_params=_cparams("parallel", "parallel", "arbitrary"),
        name="band_attention",
    )(*args)
    if want_lse:
        return res[0].reshape(b, s, qw), res[1].reshape(b, s, qw)
    return res.reshape(b, s, qw)


def _decode_kernel(*refs, reqs, n_kv, group, width, dil, has_sink, has_gate, want_lse):
    q_ref, knew_ref, vnew_ref, kcol_ref, vcol_ref, cache_ref = refs[:6]
    pos = 6
    sink_ref = gate_ref = lse_ref = None
    if has_sink:
        sink_ref = refs[pos]
        pos += 1
    if has_gate:
        gate_ref = refs[pos]
        pos += 1
    o_ref = refs[pos]
    pos += 1
    if want_lse:
        lse_ref = refs[pos]
        pos += 1
    newc_ref = refs[pos]
    lane_g = lax.broadcasted_iota(jnp.int32, (group, width), 1)
    lane_d = lax.broadcasted_iota(jnp.int32, (HEAD_DIM, width), 1)
    tap = ((width - lane_g) & (dil - 1)) == 0
    for r, j in [(r, j) for r in range(reqs) for j in range(n_kv)]:
        rows = slice(j * group, (j + 1) * group)
        k_t, v_t = cache_ref[r, 0, j], cache_ref[r, 1, j]
        q = q_ref[r, rows] * ATTN_SCALE
        sc = jnp.dot(q.astype(BF16), k_t.astype(BF16), preferred_element_type=F32)
        s_new = jnp.sum(q * knew_ref[r, j:j + 1], axis=-1, keepdims=True)
        sc = jnp.where(tap, sc, NEG_BIG)
        m = jnp.maximum(jnp.max(sc, axis=-1, keepdims=True), s_new)
        if has_sink:
            row_g = lax.broadcasted_iota(jnp.int32, (group, 1), 0)
            sink = jnp.zeros((group, 1), F32)
            for g in range(group):
                sink = jnp.where(row_g == g, sink_ref[j * group + g], sink)
            m = jnp.maximum(m, sink)
        p = jnp.where(tap, jnp.exp(sc - m), 0.0)
        p_new = jnp.exp(s_new - m)
        den = jnp.sum(p, axis=-1, keepdims=True) + p_new
        if has_sink:
            den = den + jnp.exp(sink - m)
        pv = lax.dot_general(p.astype(BF16), v_t.astype(BF16), (((1,), (1,)), ((), ())), preferred_element_type=F32)
        o = (pv + p_new * vnew_ref[r, j:j + 1]) / den
        if has_gate:
            o = gate_ref[r, rows] * o
        o_ref[r, rows] = o
        if want_lse:
            lse_ref[r, rows] = jnp.broadcast_to(m + jnp.log(den), (group, HEAD_DIM))
        newc_ref[r, 0, j] = jnp.where(lane_d == width - 1, kcol_ref[r, j], pltpu.roll(k_t, width - 1, 1))
        newc_ref[r, 1, j] = jnp.where(lane_d == width - 1, vcol_ref[r, j], pltpu.roll(v_t, width - 1, 1))


def decode_window_attention(q, k_new, v_new, cache, *, dil=1, sink=None, gate=None, want_lse=False):
    n, n_q, d = q.shape
    width, n_kv = cache.shape[1], cache.shape[3]
    group = n_q // n_kv
    cache_t = cache.transpose(0, 2, 3, 4, 1)
    reqs = max(1, min(DECODE_REQS_PER_STEP, DECODE_ROWS_PER_STEP // width))
    row_spec = lambda a: pl.BlockSpec((reqs,) + a.shape[1:], lambda bi: (bi,) + (0,) * (a.ndim - 1))
    k_col, v_col = k_new[..., None], v_new[..., None]
    args = [q, k_new, v_new, k_col, v_col, cache_t]
    in_specs = [row_spec(a) for a in args]
    if sink is not None:
        in_specs.append(pl.BlockSpec(memory_space=pltpu.SMEM))
        args.append(sink.astype(F32))
    if gate is not None:
        in_specs.append(row_spec(gate))
        args.append(gate)
    o_shape = jax.ShapeDtypeStruct((n, n_q, d), F32)
    out_shape = [o_shape] + ([o_shape] if want_lse else []) + [jax.ShapeDtypeStruct(cache_t.shape, F32)]
    res = pl.pallas_call(
        functools.partial(_decode_kernel, reqs=reqs, n_kv=n_kv, group=group, width=width, dil=dil,
                          has_sink=sink is not None, has_gate=gate is not None, want_lse=want_lse),
        grid=(n // reqs,),
        in_specs=in_specs,
        out_specs=[row_spec(s) for s in out_shape],
        out_shape=out_shape,
        compiler_params=_cparams("parallel"),
        name="decode_window_attention",
    )(*args)
    return tuple(res[:-1]) + (res[-1].transpose(0, 4, 1, 2, 3),)


CMP_PAD = LANES_V7X


def _cmp_select_kernel(q_ref, ck_ref, cv_ref, fold_ref, gate_ref, o_ref, sel_ref, *, tq, n_cmp, n_kv, group):
    i = pl.program_id(1)
    wide = (CMP_PAD, group * tq)
    n_idx = lax.broadcasted_iota(jnp.int32, wide, 0)
    qpos_w = i * tq + (lax.broadcasted_iota(jnp.int32, wide, 1) & (tq - 1))
    valid = (n_idx * CMP_STRIDE + (CMP_LEN - 1) <= qpos_w) & (n_idx < n_cmp)
    n_slc_pad = fold_ref.shape[0]
    blk = lax.broadcasted_iota(jnp.int32, (n_slc_pad, tq), 0)
    cur = lax.shift_right_logical(i * tq + lax.broadcasted_iota(jnp.int32, (n_slc_pad, tq), 1),
                                  SEL_BLOCK.bit_length() - 1)
    cand = (blk >= 1) & (blk <= cur - 2)
    forced = (blk == cur) | ((blk == cur - 1) & (cur >= 1)) | ((blk == 0) & (cur >= 2))
    for j in range(n_kv):
        q = jnp.concatenate(
            [(q_ref[0, :, (j * group + g) * HEAD_DIM:(j * group + g + 1) * HEAD_DIM] * ATTN_SCALE).astype(BF16)
             for g in range(group)], axis=0)
        s_t = lax.dot_general(ck_ref[0, j].astype(BF16), q, (((1,), (1,)), ((), ())), preferred_element_type=F32)
        s_t = jnp.where(valid, s_t, -jnp.inf)
        m = jnp.max(s_t, axis=0, keepdims=True)
        m = jnp.where(m == -jnp.inf, 0.0, m)
        e = jnp.exp(s_t - m)
        p_t = e / jnp.maximum(jnp.sum(e, axis=0, keepdims=True), 1e-30)
        cv = cv_ref[0, j].astype(BF16)
        p_sum = jnp.zeros((CMP_PAD, tq), F32)
        for g in range(group):
            h = j * group + g
            p_g = p_t[:, g * tq:(g + 1) * tq]
            p_sum = p_sum + p_g
            o = jnp.dot(p_g.T.astype(BF16), cv, preferred_element_type=F32)
            o_ref[0, :, h * HEAD_DIM:(h + 1) * HEAD_DIM] = gate_ref[0, :, h:h + 1] * o
        imp = jnp.dot(fold_ref[...], p_sum, preferred_element_type=F32, precision=lax.Precision.HIGHEST)
        sc = jnp.where(cand, imp, -jnp.inf)
        chosen = forced
        for _ in range(N_SEL - N_FORCED):
            best, bi = _first_argmax(sc, blk, n_slc_pad)
            hit = blk == bi
            chosen = chosen | (hit & (best > -jnp.inf))
            sc = jnp.where(hit, -jnp.inf, sc)
        sel_t = jnp.concatenate([jnp.where(chosen, 1.0, 0.0), jnp.zeros((CMP_PAD - n_slc_pad, tq), F32)], axis=0)
        sel_ref[0, j] = sel_t.T.astype(BF16)


def cmp_select_prompt(y_main, comp, gates, tq=LANES_V7X):
    b, t = y_main.shape[:2]
    n_cmp = comp.shape[1]
    n_slc = -(-t // SEL_BLOCK)
    n_slc_pad = -(-n_slc // 8) * 8
    grp = N_Q_C // N_KV_C
    ckv = jnp.pad(comp, ((0, 0), (0, CMP_PAD - n_cmp), (0, 0), (0, 0), (0, 0))).transpose(2, 0, 3, 1, 4)
    ratio, left = SEL_BLOCK // CMP_STRIDE, CMP_LEN // CMP_STRIDE - 1
    mi, bi = jnp.arange(CMP_PAD)[None, :], jnp.arange(n_slc_pad)[:, None]
    fold = ((mi >= ratio * bi - left) & (mi <= ratio * bi + ratio - 1) & (mi < n_cmp) & (bi < n_slc)).astype(F32)
    return pl.pallas_call(
        functools.partial(_cmp_select_kernel, tq=tq, n_cmp=n_cmp, n_kv=N_KV_C, group=grp),
        grid=(b, t // tq),
        in_specs=[pl.BlockSpec((1, tq, C_Q_WIDTH), lambda bi_, i: (bi_, i, 0)),
                  pl.BlockSpec((1, N_KV_C, CMP_PAD, HEAD_DIM), lambda bi_, i: (bi_, 0, 0, 0)),
                  pl.BlockSpec((1, N_KV_C, CMP_PAD, HEAD_DIM), lambda bi_, i: (bi_, 0, 0, 0)),
                  pl.BlockSpec(fold.shape, lambda bi_, i: (0, 0)),
                  pl.BlockSpec((1, tq, gates.shape[2]), lambda bi_, i: (bi_, i, 0))],
        out_specs=[pl.BlockSpec((1, tq, C_Q_WIDTH), lambda bi_, i: (bi_, i, 0)),
                   pl.BlockSpec((1, N_KV_C, tq, CMP_PAD), lambda bi_, i: (bi_, 0, i, 0))],
        out_shape=[jax.ShapeDtypeStruct((b, t, C_Q_WIDTH), F32),
                   jax.ShapeDtypeStruct((b, N_KV_C, t, CMP_PAD), BF16)],
        compiler_params=_cparams("parallel", "parallel"),
        name="cmp_select",
    )(y_main, ckv[0], ckv[1], fold, gates)


def _sel_kernel(q_ref, k_ref, v_ref, sel_ref, gate_ref, o_ref, *, tq, tk, n_kv, group, n_blocks, gate_col):
    i = pl.program_id(1)
    n_chunks = (i * tq + tq + tk - 1) // tk
    qpos = i * tq + lax.broadcasted_iota(jnp.int32, (tq, tk), 0)
    qs = [jnp.concatenate(
        [(q_ref[0, :, (j * group + g) * HEAD_DIM:(j * group + g + 1) * HEAD_DIM] * ATTN_SCALE).astype(BF16)
         for g in range(group)], axis=0) for j in range(n_kv)]

    def body(c, carry):
        k0 = pl.multiple_of(c * tk, tk)
        causal = (k0 + lax.broadcasted_iota(jnp.int32, (tq, tk), 1)) <= qpos
        blk_of_key = lax.shift_right_logical(k0 + lax.broadcasted_iota(jnp.int32, (n_blocks, tk), 1),
                                             SEL_BLOCK.bit_length() - 1)
        expand = (blk_of_key == lax.broadcasted_iota(jnp.int32, (n_blocks, tk), 0)).astype(BF16)
        out = []
        for j in range(n_kv):
            m, l, acc = carry[j]
            ks = slice(j * HEAD_DIM, (j + 1) * HEAD_DIM)
            k = k_ref[0, pl.ds(k0, tk), ks].astype(BF16)
            v = v_ref[0, pl.ds(k0, tk), ks].astype(BF16)
            mask = (jnp.dot(sel_ref[0, j], expand, preferred_element_type=F32) > 0.5) & causal
            s = lax.dot_general(qs[j], k, (((1,), (1,)), ((), ())), preferred_element_type=F32)
            s = jnp.where(mask[None], s.reshape(group, tq, tk), -jnp.inf).reshape(group * tq, tk)
            m_new = jnp.maximum(m, jnp.max(s, axis=-1, keepdims=True))
            p = jnp.exp(s - m_new)
            alpha = jnp.exp(m - m_new)
            l = alpha * l + jnp.sum(p, axis=-1, keepdims=True)
            acc = alpha * acc + jnp.dot(p.astype(BF16), v, preferred_element_type=F32)
            out.append((m_new, l, acc))
        return tuple(out)

    init = tuple((jnp.full((group * tq, 1), NEG_BIG, F32), jnp.zeros((group * tq, 1), F32),
                  jnp.zeros((group * tq, HEAD_DIM), F32)) for _ in range(n_kv))
    final = lax.fori_loop(0, n_chunks, body, init)
    for j in range(n_kv):
        _, l, acc = final[j]
        o = acc / l
        for g in range(group):
            h = j * group + g
            o_ref[0, :, h * HEAD_DIM:(h + 1) * HEAD_DIM] = (
                gate_ref[0, :, gate_col + h:gate_col + h + 1] * o[g * tq:(g + 1) * tq])


def selected_attention_prompt(q_rope, y_main, selmask, gates, *, k_idx, v_idx, gate_col):
    b, t, qw = q_rope.shape
    n_blocks = selmask.shape[-1]
    kw = N_KV_C * HEAD_DIM
    tq, tk = SEL_Q_TILE, SEL_K_TILE
    return pl.pallas_call(
        functools.partial(_sel_kernel, tq=tq, tk=tk, n_kv=N_KV_C, group=N_Q_C // N_KV_C,
                          n_blocks=n_blocks, gate_col=gate_col),
        grid=(b, t // tq),
        in_specs=[pl.BlockSpec((1, tq, qw), lambda bi, i: (bi, i, 0)),
                  pl.BlockSpec((1, t, kw), lambda bi, i: (bi, 0, k_idx)),
                  pl.BlockSpec((1, t, kw), lambda bi, i: (bi, 0, v_idx)),
                  pl.BlockSpec((1, N_KV_C, tq, n_blocks), lambda bi, i: (bi, 0, i, 0)),
                  pl.BlockSpec((1, tq, gates.shape[2]), lambda bi, i: (bi, i, 0))],
        out_specs=pl.BlockSpec((1, tq, qw), lambda bi, i: (bi, i, 0)),
        out_shape=jax.ShapeDtypeStruct((b, t, qw), F32),
        compiler_params=_cparams("parallel", "arbitrary"),
        name="selected_attention",
    )(q_rope, y_main, y_main, selmask, gates)


def _sel_sample_kernel(page_ref, q_ref, qbd_ref, knew_ref, vnew_ref, gate_ref, want_ref, *refs, n_sel):
    del page_ref
    n_q, grp, n_past = N_Q_C, N_Q_C // N_KV_C, n_sel - 1
    pages, o_ref = refs[:n_past * N_KV_C], refs[n_past * N_KV_C]

    def stacked(c):
        return jnp.concatenate(
            [jnp.concatenate([pages[s * N_KV_C + j][0, c, 0] for j in range(N_KV_C)], axis=0) for s in range(n_past)],
            axis=1).astype(BF16)

    width = n_past * PAGE_SIZE
    page_shift = PAGE_SIZE.bit_length() - 1
    sc = jnp.dot((qbd_ref[0] * ATTN_SCALE).astype(BF16), stacked(0), preferred_element_type=F32)
    lane = lax.broadcasted_iota(jnp.int32, (n_q, width), 1)
    key_half = lax.shift_right_logical(lane, SEL_BLOCK.bit_length() - 1) & (PAGE_SIZE // SEL_BLOCK - 1)
    slot_of_lane = lax.shift_right_logical(lax.broadcasted_iota(jnp.int32, (n_sel, width), 1), page_shift) + 1
    expand = (slot_of_lane == lax.broadcasted_iota(jnp.int32, (n_sel, width), 0)).astype(BF16)
    want = jnp.dot(want_ref[0].astype(BF16), expand, preferred_element_type=F32)
    mask = key_half.astype(F32) == want
    s_new = jnp.sum(q_ref[0] * ATTN_SCALE * knew_ref[0], axis=-1, keepdims=True)
    sc = jnp.where(mask, sc, NEG_BIG)
    m = jnp.maximum(jnp.max(sc, axis=-1, keepdims=True), s_new)
    p = jnp.where(mask, jnp.exp(sc - m), 0.0)
    p_new = jnp.exp(s_new - m)
    den = jnp.sum(p, axis=-1, keepdims=True) + p_new
    pv = lax.dot_general(p.astype(BF16), stacked(1), (((1,), (1,)), ((), ())), preferred_element_type=F32)
    head_of_row = lax.shift_right_logical(lax.broadcasted_iota(jnp.int32, (n_q, HEAD_DIM), 0), grp.bit_length() - 1)
    o = jnp.zeros((n_q, HEAD_DIM), F32)
    for j in range(N_KV_C):
        o = jnp.where(head_of_row == j, pv[:, j * HEAD_DIM:(j + 1) * HEAD_DIM], o)
    o_ref[0] = gate_ref[0] * ((o + p_new * vnew_ref[0]) / den)


def selected_attention_sample(q, k_new, v_new, gate, pool, page, want):
    n, n_q, d = q.shape
    grp = n_q // N_KV_C
    n_sel = want.shape[2]
    pool_t = pool.transpose(0, 2, 3, 4, 1)
    eye = jnp.repeat(jnp.eye(N_KV_C, dtype=q.dtype), grp, axis=0)
    q_bd = (q[:, :, None, :] * eye[None, :, :, None]).reshape(n, n_q, N_KV_C * d)
    row_spec = lambda w: pl.BlockSpec((1, n_q, w), lambda bi, pg: (bi, 0, 0))

    def page_spec(s, j):
        return pl.BlockSpec((1, 2, 1, d, PAGE_SIZE), lambda bi, pg: (pg[(bi * N_KV_C + j) * n_sel + s], 0, j, 0, 0))

    page_specs = [page_spec(s, j) for s in range(1, n_sel) for j in range(N_KV_C)]
    grid_spec = pltpu.PrefetchScalarGridSpec(
        num_scalar_prefetch=1,
        grid=(n,),
        in_specs=[row_spec(d), row_spec(N_KV_C * d), row_spec(d), row_spec(d), row_spec(1), row_spec(n_sel)] + page_specs,
        out_specs=row_spec(d),
    )
    return pl.pallas_call(
        functools.partial(_sel_sample_kernel, n_sel=n_sel),
        grid_spec=grid_spec,
        out_shape=jax.ShapeDtypeStruct((n, n_q, d), F32),
        compiler_params=_cparams("parallel"),
        name="selected_attention_sample",
    )(page, q, q_bd, k_new, v_new, gate, want, *([pool_t] * len(page_specs)))


def _half_proj_kernel(*refs, n_rows):
    n_pairs = N_KV_C // 2
    x_refs, w_ref, o_ref = refs[:2 * n_pairs], refs[2 * n_pairs], refs[2 * n_pairs + 1]
    out_w = 2 * 2 * CMP_HID
    for c in range(2):
        for p in range(n_pairs):
            x_ref = x_refs[c * n_pairs + p]
            acc = jnp.zeros((n_rows, out_w), F32)
            for sp in range(CMP_STRIDE // 2):
                lo = x_ref[0, pl.ds(2 * sp, n_rows, stride=CMP_STRIDE), :]
                hi = x_ref[0, pl.ds(2 * sp + 1, n_rows, stride=CMP_STRIDE), :]
                lhs = jnp.concatenate([lo, hi], axis=1).astype(BF16)
                acc = acc + jnp.dot(lhs, w_ref[c, sp], preferred_element_type=F32)
            o_ref[0, :, (c * n_pairs + p) * out_w:(c * n_pairs + p + 1) * out_w] = acc


def half_block_proj(rows_arr, col_idx, w_bd, tile_rows):
    b, l = rows_arr.shape[:2]
    n_half_tile = tile_rows // CMP_STRIDE
    out_w = 2 * 2 * N_KV_C * CMP_HID
    n_chunks = C_KV_WIDTH // LANES_V7X

    def chunk_spec(cp):
        return pl.BlockSpec((1, tile_rows, LANES_V7X), lambda bi, i: (bi, i, col_idx * n_chunks + cp))

    return pl.pallas_call(
        functools.partial(_half_proj_kernel, n_rows=n_half_tile),
        grid=(b, l // tile_rows),
        in_specs=[chunk_spec(cp) for cp in range(n_chunks)] + [pl.BlockSpec(w_bd.shape, lambda bi, i: (0, 0, 0, 0))],
        out_specs=pl.BlockSpec((1, n_half_tile, out_w), lambda bi, i: (bi, i, 0)),
        out_shape=jax.ShapeDtypeStruct((b, l // CMP_STRIDE, out_w), F32),
        compiler_params=_cparams("parallel", "parallel"),
        name="half_block_proj",
    )(*([rows_arr] * n_chunks), w_bd)


def _paged_half_proj_kernel(pt_ref, *refs, n_pages):
    del pt_ref
    n_pairs = N_KV_C // 2
    pages, w_ref, o_ref = refs[:n_pages], refs[n_pages], refs[n_pages + 1]
    rows = refs[n_pages + 2:]
    for j, page in enumerate(pages):
        for c in range(2):
            for p in range(n_pairs):
                x_t = page[0, c, 2 * p:2 * p + 2].reshape(2 * HEAD_DIM, PAGE_SIZE)
                rows[c * n_pairs + p][j * PAGE_SIZE:(j + 1) * PAGE_SIZE, :] = x_t.T
    n_rows = n_pages * PAGE_SIZE // CMP_STRIDE
    out_w = 2 * 2 * CMP_HID
    for c in range(2):
        for p in range(n_pairs):
            x_ref = rows[c * n_pairs + p]
            acc = jnp.zeros((n_rows, out_w), F32)
            for sp in range(CMP_STRIDE // 2):
                lo = x_ref[pl.ds(2 * sp, n_rows, stride=CMP_STRIDE), :]
                hi = x_ref[pl.ds(2 * sp + 1, n_rows, stride=CMP_STRIDE), :]
                lhs = jnp.concatenate([lo, hi], axis=1).astype(BF16)
                acc = acc + jnp.dot(lhs, w_ref[c, sp], preferred_element_type=F32)
            o_ref[0, :, (c * n_pairs + p) * out_w:(c * n_pairs + p + 1) * out_w] = acc


def paged_half_block_proj(pool, page_table, w_bd, n_pages):
    n, pages_per_req = page_table.shape
    pool_t = pool.transpose(0, 2, 3, 4, 1)
    steps = pages_per_req // n_pages
    n_half = n_pages * PAGE_SIZE // CMP_STRIDE
    out_w = 2 * 2 * N_KV_C * CMP_HID

    def page_spec(j):
        return pl.BlockSpec((1, 2, N_KV_C, HEAD_DIM, PAGE_SIZE),
                            lambda bi, i, pt: (pt[bi * pages_per_req + i * n_pages + j], 0, 0, 0, 0))

    grid_spec = pltpu.PrefetchScalarGridSpec(
        num_scalar_prefetch=1,
        grid=(n, steps),
        in_specs=[page_spec(j) for j in range(n_pages)] + [pl.BlockSpec(w_bd.shape, lambda bi, i, pt: (0, 0, 0, 0))],
        out_specs=pl.BlockSpec((1, n_half, out_w), lambda bi, i, pt: (bi, i, 0)),
        scratch_shapes=[pltpu.VMEM((n_pages * PAGE_SIZE, LANES_V7X), F32) for _ in range(C_KV_WIDTH // LANES_V7X)],
    )
    return pl.pallas_call(
        functools.partial(_paged_half_proj_kernel, n_pages=n_pages),
        grid_spec=grid_spec,
        out_shape=jax.ShapeDtypeStruct((n, pages_per_req * PAGE_SIZE // CMP_STRIDE, out_w), F32),
        compiler_params=_cparams("parallel", "arbitrary"),
        name="paged_half_block_proj",
    )(page_table.reshape(-1), *([pool_t] * n_pages), w_bd)


def _half_proj_weight(w1h):
    r = CMP_LEN // CMP_STRIDE
    eye = jnp.eye(2, dtype=F32)
    w = w1h.reshape(2, r, CMP_STRIDE // 2, 2, HEAD_DIM, CMP_HID)
    w = jnp.einsum('cjpldh,kq->cplkdqjh', w, eye)
    return w.reshape(2, CMP_STRIDE // 2, 4 * HEAD_DIM, 2 * r * CMP_HID).astype(BF16)


def _first_argmax(v, iota, size):
    m = jnp.max(v, axis=0, keepdims=True)
    return m, jnp.min(jnp.where(v == m, iota, size), axis=0, keepdims=True)


def _route_tokens(scores, bias):
    tm = scores.shape[1]
    per = N_EXPERTS // N_EXPERT_GROUPS
    biased = scores + bias
    iota_per = lax.broadcasted_iota(jnp.int32, (per, tm), 0)
    iota_grp = lax.broadcasted_iota(jnp.int32, (N_EXPERT_GROUPS, tm), 0)
    grp = jnp.zeros((N_EXPERT_GROUPS, tm), F32)
    slabs = [biased[g * per:(g + 1) * per] for g in range(N_EXPERT_GROUPS)]
    for g, v in enumerate(slabs):
        m1, i1 = _first_argmax(v, iota_per, per)
        m2 = jnp.max(jnp.where(iota_per == i1, -jnp.inf, v), axis=0, keepdims=True)
        grp = jnp.where(iota_grp == g, m1 + m2, grp)
    keep = jnp.zeros((N_EXPERT_GROUPS, tm), jnp.int32)
    for _ in range(TOPK_GROUPS):
        _, gi = _first_argmax(grp, iota_grp, N_EXPERT_GROUPS)
        keep = jnp.where(iota_grp == gi, 1, keep)
        grp = jnp.where(iota_grp == gi, -jnp.inf, grp)
    v = jnp.concatenate([jnp.where(keep[g:g + 1] > 0, slabs[g], -jnp.inf) for g in range(N_EXPERT_GROUPS)], axis=0)
    iota_e = lax.broadcasted_iota(jnp.int32, (N_EXPERTS, tm), 0)
    iota_k = lax.broadcasted_iota(jnp.int32, (TOP_K, tm), 0)
    eidx = jnp.zeros((TOP_K, tm), jnp.int32)
    ew = jnp.zeros((TOP_K, tm), F32)
    for k in range(TOP_K):
        _, ei = _first_argmax(v, iota_e, N_EXPERTS)
        hit = iota_e == ei
        eidx = jnp.where(iota_k == k, ei, eidx)
        ew = jnp.where(iota_k == k, jnp.sum(jnp.where(hit, scores, 0.0), axis=0, keepdims=True), ew)
        v = jnp.where(hit, -jnp.inf, v)
    return eidx, ew / jnp.sum(ew, axis=0, keepdims=True) * ROUTED_SCALE


def _moe_in_kernel(x_ref, g_ref, sh_ref, sc_ref, rwt_ref, rb_ref, sg_ref, su_ref, sd_ref,
                   h_ref, eidx_ref, ew_ref, shared_ref):
    h = _modulated_norm(x_ref[0], g_ref[...], sc_ref[0], sh_ref[0])
    logits_t = lax.dot_general(rwt_ref[...], h, (((1,), (1,)), ((), ())), preferred_element_type=F32,
                               precision=lax.Precision.HIGHEST)
    eidx, ew = _route_tokens(jax.nn.sigmoid(logits_t), rb_ref[...])
    eidx_ref[...] = eidx
    ew_ref[...] = ew
    h_ref[0] = h
    hb = h.astype(BF16)
    gate = jnp.dot(hb, sg_ref[...], preferred_element_type=F32)
    up = jnp.dot(hb, su_ref[...], preferred_element_type=F32)
    mid = (jax.nn.silu(gate) * up).astype(BF16)
    shared_ref[0] = jnp.dot(mid, sd_ref[...], preferred_element_type=F32)


def moe_in(x, g, shift, scale, router_w, router_b, sg_bf, su_bf, sd_bf, tm):
    b, t, k = x.shape
    per_row = shift.shape[1] != 1
    mod_spec = (pl.BlockSpec((1, tm, k), lambda bi, i: (bi, i, 0)) if per_row
                else pl.BlockSpec((1, 1, k), lambda bi, i: (bi, 0, 0)))
    row_spec = lambda wd: pl.BlockSpec((1, tm, wd), lambda bi, i: (bi, i, 0))
    full = lambda a: pl.BlockSpec(a.shape, lambda bi, i: (0,) * a.ndim)
    tiles = t // tm
    tok_spec = pl.BlockSpec((TOP_K, tm), lambda bi, i: (0, bi * tiles + i))
    rwt = router_w.T
    rb = router_b.astype(F32).reshape(N_EXPERTS, 1)
    return pl.pallas_call(
        _moe_in_kernel,
        grid=(b, tiles),
        in_specs=[row_spec(k), pl.BlockSpec((1, k), lambda bi, i: (0, 0)), mod_spec, mod_spec,
                  full(rwt), full(rb), full(sg_bf), full(su_bf), full(sd_bf)],
        out_specs=[row_spec(k), tok_spec, tok_spec, row_spec(k)],
        out_shape=[jax.ShapeDtypeStruct((b, t, k), F32), jax.ShapeDtypeStruct((TOP_K, b * t), jnp.int32),
                   jax.ShapeDtypeStruct((TOP_K, b * t), F32), jax.ShapeDtypeStruct((b, t, k), F32)],
        compiler_params=_cparams("parallel", "parallel"),
        name="moe_in",
    )(x, g.reshape(1, k), shift, scale, rwt, rb, sg_bf, su_bf, sd_bf)


def _gmm_kernel(blk_e_ref, n_used_ref, x_ref, wg_ref, wu_ref, wd_ref, o_ref, wg_s, wu_s, wd_s):
    blk = pl.program_id(0)
    prev_e = blk_e_ref[jnp.maximum(blk - 1, 0)]
    new_expert = (blk == 0) | (blk_e_ref[blk] != prev_e)

    @pl.when(new_expert)
    def _():
        wg_s[...] = wg_ref[0].astype(BF16)
        wu_s[...] = wu_ref[0].astype(BF16)
        wd_s[...] = wd_ref[0].astype(BF16)

    @pl.when(blk < n_used_ref[0])
    def _():
        x = x_ref[...].astype(BF16)
        gate = jnp.dot(x, wg_s[...], preferred_element_type=F32)
        up = jnp.dot(x, wu_s[...], preferred_element_type=F32)
        mid = (jax.nn.silu(gate) * up).astype(BF16)
        o_ref[...] = jnp.dot(mid, wd_s[...], preferred_element_type=F32)

    @pl.when(blk >= n_used_ref[0])
    def _():
        o_ref[...] = jnp.zeros_like(o_ref)


def grouped_experts(xs, blk_e, n_used, w_gate, w_up, w_down, layer):
    cap, k = xs.shape
    tm = MOE_TILE
    n_blk = cap // tm
    de = w_gate.shape[-1]
    grid_spec = pltpu.PrefetchScalarGridSpec(
        num_scalar_prefetch=2,
        grid=(n_blk,),
        in_specs=[pl.BlockSpec((tm, k), lambda i, be, nu: (i, 0)),
                  pl.BlockSpec((None, 1, k, de), lambda i, be, nu: (layer, be[i], 0, 0)),
                  pl.BlockSpec((None, 1, k, de), lambda i, be, nu: (layer, be[i], 0, 0)),
                  pl.BlockSpec((None, 1, de, k), lambda i, be, nu: (layer, be[i], 0, 0))],
        out_specs=pl.BlockSpec((tm, k), lambda i, be, nu: (i, 0)),
        scratch_shapes=[pltpu.VMEM((k, de), BF16), pltpu.VMEM((k, de), BF16), pltpu.VMEM((de, k), BF16)],
    )
    return pl.pallas_call(
        _gmm_kernel,
        grid_spec=grid_spec,
        out_shape=jax.ShapeDtypeStruct((cap, k), F32),
        compiler_params=_cparams("arbitrary"),
        name="grouped_experts",
    )(blk_e, n_used, xs, w_gate, w_up, w_down)


def _moe_out_kernel(x_ref, gate_ref, picked_ref, ew_ref, shared_ref, o_ref):
    routed = picked_ref[0] * ew_ref[:, 0:1]
    for k in range(1, TOP_K):
        routed = routed + picked_ref[k] * ew_ref[:, k:k + 1]
    o_ref[0] = x_ref[0] + gate_ref[0] * (routed + shared_ref[0])


def moe_out(x, gate, picked, ew, shared, tm, row0):
    b, t, d = x.shape
    per_row = gate.shape[1] != 1
    gate_spec = (pl.BlockSpec((1, tm, d), lambda bi, i: (bi, i, 0)) if per_row
                 else pl.BlockSpec((1, 1, d), lambda bi, i: (bi, 0, 0)))
    row_spec = pl.BlockSpec((1, tm, d), lambda bi, i: (bi, i, 0))
    tiles, off = t // tm, row0 // tm
    return pl.pallas_call(
        _moe_out_kernel,
        grid=(b, tiles),
        in_specs=[row_spec, gate_spec,
                  pl.BlockSpec((TOP_K, tm, d), lambda bi, i: (0, off + bi * tiles + i, 0)),
                  pl.BlockSpec((tm, TOP_K), lambda bi, i: (off + bi * tiles + i, 0)),
                  row_spec],
        out_specs=row_spec,
        out_shape=jax.ShapeDtypeStruct((b, t, d), F32),
        compiler_params=_cparams("parallel", "parallel"),
        name="moe_out",
    )(x, gate, picked, ew, shared)


def _final_norm_kernel(x_ref, g_ref, o_ref):
    x = x_ref[0]
    var = jnp.mean(x * x, axis=-1, keepdims=True)
    o_ref[0] = x * lax.rsqrt(var + RMS_EPS) * g_ref[...]


def final_norm(x, g, tm):
    b, t, d = x.shape
    row_spec = pl.BlockSpec((1, tm, d), lambda bi, i: (bi, i, 0))
    return pl.pallas_call(
        _final_norm_kernel,
        grid=(b, t // tm),
        in_specs=[row_spec, pl.BlockSpec((1, d), lambda bi, i: (0, 0))],
        out_specs=row_spec,
        out_shape=jax.ShapeDtypeStruct((b, t, d), F32),
        compiler_params=_cparams("parallel", "parallel"),
        name="final_norm",
    )(x, g.reshape(1, d))


def _ada_kernel(c_ref, w_ref, b_ref, o_ref):
    a = jax.nn.silu(c_ref[...]).astype(BF16)
    o_ref[0] = jnp.dot(a, w_ref[0].astype(BF16), preferred_element_type=F32) + b_ref[0]


def ada_mods(c_all, ada_w, ada_b):
    n, d = c_all.shape
    depth, _, width = ada_w.shape
    tn = 1024
    return pl.pallas_call(
        _ada_kernel,
        grid=(depth, width // tn),
        in_specs=[pl.BlockSpec((n, d), lambda l, j: (0, 0)),
                  pl.BlockSpec((1, d, tn), lambda l, j: (l, 0, j)),
                  pl.BlockSpec((1, 1, tn), lambda l, j: (l, 0, j))],
        out_specs=pl.BlockSpec((1, n, tn), lambda l, j: (l, 0, j)),
        out_shape=jax.ShapeDtypeStruct((depth, n, width), F32),
        compiler_params=_cparams("parallel", "parallel"),
        name="ada_mods",
    )(c_all, ada_w, ada_b.reshape(depth, 1, width))


def _masked_softmax(s, mask, sink=None):
    s = jnp.where(mask, s, -jnp.inf)
    m = jnp.max(s, axis=-1, keepdims=True)
    if sink is not None:
        m = jnp.maximum(m, sink)
    m = jnp.where(jnp.isfinite(m), m, 0.0)
    e = jnp.exp(s - m)
    den = jnp.sum(e, axis=-1, keepdims=True)
    if sink is not None:
        den = den + jnp.exp(sink - m)
    p = e / jnp.maximum(den, 1e-30)
    return p, (m + jnp.log(den))[..., 0]


def _cmp_to_sel(p, n_slc):
    ratio = SEL_BLOCK // CMP_STRIDE
    left = CMP_LEN // CMP_STRIDE - 1
    pad = [(0, 0)] * (p.ndim - 1) + [(left, ratio * n_slc - p.shape[-1])]
    pp = jnp.pad(p, pad)
    out = pp[..., 0:ratio * n_slc:ratio]
    for o in range(1, ratio + left):
        out = out + pp[..., o:o + ratio * n_slc:ratio]
    return out


def _select_blocks(imp, pos, n_slc):
    k_top = N_SEL - N_FORCED
    cur = pos // SEL_BLOCK
    j = jnp.arange(n_slc)
    cand = (j[None, :] >= 1) & (j[None, :] <= cur[:, None] - 2)
    sc = jnp.where(cand, imp, -jnp.inf)
    if n_slc < k_top:
        sc = jnp.pad(sc, ((0, 0), (0, 0), (0, 0), (0, k_top - n_slc)), constant_values=-jnp.inf)
    lane = jnp.arange(sc.shape[-1])
    vals, idx = [], []
    for _ in range(k_top):
        best = jnp.argmax(sc, axis=-1)
        vals.append(jnp.max(sc, axis=-1))
        idx.append(best)
        sc = jnp.where(lane == best[..., None], -jnp.inf, sc)
    vals, idx = jnp.stack(vals, axis=-1), jnp.stack(idx, axis=-1)
    forced = jnp.stack([cur, cur - 1, jnp.zeros_like(cur)], axis=-1)
    forced_ok = jnp.stack([cur >= 0, cur >= 1, cur >= 2], axis=-1)
    lead = imp.shape[:2]
    blocks = jnp.concatenate([jnp.broadcast_to(forced, lead + forced.shape), idx.astype(cur.dtype)], axis=-1)
    ok = jnp.concatenate([jnp.broadcast_to(forced_ok, lead + forced_ok.shape), vals > -jnp.inf], axis=-1)
    return jnp.clip(blocks, 0, n_slc - 1), ok


def _compress(hp, n_cmp, w1h, b1, w2, pe):
    b = hp.shape[0]
    r = CMP_LEN // CMP_STRIDE
    hp = hp.reshape(b, hp.shape[1], 2, N_KV_C, r, CMP_HID)
    pre = (jnp.einsum('cjsd,cjsdh->ch', pe.reshape(2, r, CMP_STRIDE, HEAD_DIM), w1h) + b1)[:, None, :]
    for j in range(r):
        pre = pre + hp[:, j:j + n_cmp, :, :, j]
    return jnp.einsum('bnckh,chd->bnckd', jax.nn.gelu(pre), w2)


def _compressed_attention(q_cmp, comp, pos):
    blk_end = jnp.arange(comp.shape[1]) * CMP_STRIDE + CMP_LEN - 1
    s = jnp.einsum('bqhgd,bnhd->bhgqn', q_cmp, comp[:, :, 0], preferred_element_type=F32) * ATTN_SCALE
    p_cmp, _ = _masked_softmax(s, blk_end[None, :] <= pos[:, None])
    o_cmp = jnp.einsum('bhgqn,bnhd->bqhgd', p_cmp.astype(comp.dtype), comp[:, :, 1])
    return o_cmp, p_cmp


A_SPECS = ((0, (N_Q_A + N_KV_A) * HEAD_DIM, 'rope', 0, 0),
           ((N_Q_A + N_KV_A) * HEAD_DIM, (N_Q_A + 2 * N_KV_A) * HEAD_DIM, 'raw', 0, (N_Q_A + N_KV_A) * HEAD_DIM))
B_SPECS = ((0, 2 * B_WIDTH, 'rope', 0, 0), (2 * B_WIDTH, 3 * B_WIDTH, 'raw', 0, 2 * B_WIDTH))
_C0 = C_Q_WIDTH + C_KV_WIDTH
_C1 = _C0 + C_KV_WIDTH
C_SPECS = ((0, _C0, 'raw', 0, 0),
           (_C0, _C0 + C_KV_WIDTH // 2, 'rope', 0, _C0), (_C0 + C_KV_WIDTH // 2, _C1, 'raw', 0, _C0 + C_KV_WIDTH // 2),
           (_C1, _C1 + C_KV_WIDTH // 2, 'rope', 0, _C1), (_C1 + C_KV_WIDTH // 2, C_MAIN_WIDTH, 'raw', 0, _C1 + C_KV_WIDTH // 2),
           (0, C_Q_WIDTH, 'rope', 1, 0),
           (C_MAIN_WIDTH, C_MAIN_WIDTH + N_GATES_C, 'sigmoid', 2, 0))
_AK = N_Q_A * HEAD_DIM
_AW = N_KV_A * HEAD_DIM
A_SPECS_P = A_SPECS + ((_AK, _AK + _AW, 'rope', 1, 0), (_AK + _AW, _AK + 2 * _AW, 'raw', 1, _AW))
_BW = B_HEADS_PER_GROUP * HEAD_DIM
B_SPECS_P = B_SPECS + tuple(
    seg for gi in range(N_GROUPS_B) for seg in
    ((B_WIDTH + gi * _BW, B_WIDTH + (gi + 1) * _BW, 'rope', 1 + gi, 0),
     (2 * B_WIDTH + gi * _BW, 2 * B_WIDTH + (gi + 1) * _BW, 'raw', 1 + gi, _BW)))
_CH = C_KV_WIDTH // 2
C_SPECS_P = C_SPECS + ((C_Q_WIDTH, _C0, 'raw', 3, 0),
                       (_C0, _C0 + _CH, 'rope', 4, 0), (_C0 + _CH, _C1, 'raw', 4, _CH),
                       (_C1, _C1 + _CH, 'rope', 5, 0), (_C1 + _CH, C_MAIN_WIDTH, 'raw', 5, _CH))


def _kv_from_cols(cols, n_kv, rows):
    b = cols.shape[0]
    return cols[:, :, cols.shape[2] - rows:].reshape(b, 2, n_kv, HEAD_DIM, rows).transpose(0, 4, 1, 2, 3)


def _mixer_a(x, mods, g, w_qkv_bf, w_o_bf, sink, tabs, tm, past):
    shift, scale, gate = mods
    nq = N_Q_A * HEAD_DIM
    kvw = N_KV_A * HEAD_DIM
    if past is None:
        y, kv_cols = norm_linear(x, g, shift, scale, w_qkv_bf, tabs, A_SPECS_P,
                                 ((nq + 2 * kvw, 'rows'), (2 * kvw, 'cols_last')), tm)
        o = band_attention(y, y, y, dil=1, n_kv=N_KV_A, group=N_Q_A // N_KV_A, tq=WINDOW_A,
                           q_idx=0, k_idx=nq // kvw, v_idx=nq // kvw + 1, sink=sink)
        new = _kv_from_cols(kv_cols, N_KV_A, min(WINDOW_A, y.shape[1]))
    else:
        (y,) = norm_linear(x, g, shift, scale, w_qkv_bf, tabs, A_SPECS, ((nq + 2 * kvw, 'rows'),), tm)
        n = y.shape[1]
        o, new = decode_window_attention(y[0, :, :nq].reshape(n, N_Q_A, HEAD_DIM),
                                         y[0, :, nq:nq + kvw].reshape(n, N_KV_A, HEAD_DIM),
                                         y[0, :, nq + kvw:].reshape(n, N_KV_A, HEAD_DIM), past, sink=sink)
        o = o.reshape(1, n, nq)
    return linear_out([o], w_o_bf, x, gate, 'plain', tm), new


def _mixer_b(x, mods, g, w_qkv_bf, w_o_bf, tabs, tm, past):
    shift, scale, gate = mods
    hpg = B_HEADS_PER_GROUP
    gw = hpg * HEAD_DIM
    outs, lses, news = [], [], []
    if past is None:
        t = x.shape[1]
        descs = [(3 * B_WIDTH, 'rows')] + [(2 * gw, 'cols_last' if win <= tm else 'cols') for win, _ in B_PATTERNS]
        y, *kv_cols = norm_linear(x, g, shift, scale, w_qkv_bf, tabs, B_SPECS_P, descs, tm)
        for gi, (win, dil) in enumerate(B_PATTERNS):
            o, lse = band_attention(y, y, y, dil=dil, n_kv=hpg, group=1, tq=win // dil,
                                    q_idx=gi, k_idx=N_GROUPS_B + gi, v_idx=2 * N_GROUPS_B + gi, want_lse=True)
            outs.append(o)
            lses.append(lse)
            news.append(_kv_from_cols(kv_cols[gi], hpg, min(win, t)))
    else:
        (y,) = norm_linear(x, g, shift, scale, w_qkv_bf, tabs, B_SPECS, ((3 * B_WIDTH, 'rows'),), tm)
        n = y.shape[1]
        for gi, (win, dil) in enumerate(B_PATTERNS):
            part = lambda c: y[0, :, c * B_WIDTH + gi * gw:c * B_WIDTH + (gi + 1) * gw].reshape(n, hpg, HEAD_DIM)
            o, lse, new = decode_window_attention(part(0), part(1), part(2), past[gi], dil=dil, want_lse=True)
            outs.append(o.reshape(1, n, gw))
            lses.append(lse.reshape(1, n, gw))
            news.append(new)
    return linear_out(outs + lses, w_o_bf, x, gate, 'mix3', tm), tuple(news)


def _mixer_c(x, mods, g, w_in_bf, w_o_bf, cmp_w1, cmp_b1, cmp_w2, cmp_pe, tabs, pos, tm, past):
    shift, scale, gate = mods
    grp = N_Q_C // N_KV_C
    kvw = N_KV_C * HEAD_DIM
    descs = [(C_MAIN_WIDTH, 'rows'), (C_Q_WIDTH, 'rows'), (N_GATES_C, 'rows')]
    w1h = cmp_w1.reshape(2, CMP_LEN // CMP_STRIDE, CMP_STRIDE, HEAD_DIM, CMP_HID)
    w_bd = _half_proj_weight(w1h)
    cmp_idx, slc_idx, win_idx = C_Q_WIDTH // C_KV_WIDTH, _C0 // C_KV_WIDTH, _C1 // C_KV_WIDTH
    if past is None:
        descs += [(C_KV_WIDTH, 'cols'), (C_KV_WIDTH, 'cols'), (C_KV_WIDTH, 'cols_last')]
        y, q_rope, gates, cmp_cols, slc_cols, win_cols = norm_linear(x, g, shift, scale, w_in_bf, tabs, C_SPECS_P,
                                                                     descs, tm)
        b, t = y.shape[:2]
        hp = half_block_proj(y, cmp_idx, w_bd, t)
        n_cmp = (t - CMP_LEN) // CMP_STRIDE + 1
        comp = _compress(hp, n_cmp, w1h, cmp_b1, cmp_w2, cmp_pe)
        o_cmp, selmask = cmp_select_prompt(y, comp, gates)
        o_slc = selected_attention_prompt(q_rope, y, selmask, gates, k_idx=2 * slc_idx, v_idx=2 * slc_idx + 1,
                                          gate_col=N_Q_C)
        o_win = band_attention(q_rope, y, y, dil=1, n_kv=N_KV_C, group=grp, tq=WINDOW_C, q_idx=0,
                               k_idx=2 * win_idx, v_idx=2 * win_idx + 1, gate=gates, gate_col=2 * N_Q_C)
        new_win = _kv_from_cols(win_cols, N_KV_C, min(WINDOW_C, t))
        cmp_kv = _kv_from_cols(cmp_cols, N_KV_C, t)
        slc_kv = _kv_from_cols(slc_cols, N_KV_C, t)
    else:
        y, q_rope, gates = norm_linear(x, g, shift, scale, w_in_bf, tabs, C_SPECS, descs, tm)
        win_buf, cmp_pool, slc_pool, page_table = past
        n = y.shape[1]
        t = 1
        cmp_kv = y[0, :, C_Q_WIDTH:_C0].reshape(n, t, 2, N_KV_C, HEAD_DIM)
        slc_kv = y[0, :, _C0:_C1].reshape(n, t, 2, N_KV_C, HEAD_DIM)
        win_kv = y[0, :, _C1:].reshape(n, t, 2, N_KV_C, HEAD_DIM)
        q_cmp = y[0, :, :C_Q_WIDTH].reshape(n, t, N_KV_C, grp, HEAD_DIM)
        gview = gates[0].reshape(n, t, 3, N_KV_C, grp, 1)
        hp = paged_half_block_proj(cmp_pool, page_table, w_bd, CMP_PAGES_PER_STEP)
        length = PAST_LEN + t
        n_cmp = (length - CMP_LEN) // CMP_STRIDE + 1
        comp = _compress(hp, n_cmp, w1h, cmp_b1, cmp_w2, cmp_pe)
        o_cmp, p_cmp = _compressed_attention(q_cmp, comp, pos)
        n_slc = -(-length // SEL_BLOCK)
        blocks, ok = _select_blocks(_cmp_to_sel(p_cmp.sum(axis=2), n_slc), pos, n_slc)
        sub = PAGE_SIZE // SEL_BLOCK
        pb = jnp.minimum(blocks[:, :, 0], PAST_LEN // SEL_BLOCK - 1)
        page = jnp.take_along_axis(jnp.broadcast_to(page_table[:, None], (n, N_KV_C, page_table.shape[1])),
                                   pb // sub, axis=2)
        want = jnp.where(ok[:, :, 0], pb % sub, -1).astype(F32)
        o_slc = selected_attention_sample(
            q_rope[0].reshape(n, N_Q_C, HEAD_DIM), jnp.repeat(slc_kv[:, 0, 0], grp, axis=1),
            jnp.repeat(slc_kv[:, 0, 1], grp, axis=1), gates[0, :, N_Q_C:2 * N_Q_C].reshape(n, N_Q_C, 1),
            slc_pool, page.reshape(-1).astype(jnp.int32), jnp.repeat(want, grp, axis=1))
        o_slc = o_slc.reshape(1, n, C_Q_WIDTH)
        o_win, new_win = decode_window_attention(
            q_rope[0].reshape(n, N_Q_C, HEAD_DIM), win_kv[:, 0, 0], win_kv[:, 0, 1], win_buf,
            gate=gates[0, :, 2 * N_Q_C:].reshape(n, N_Q_C, 1))
        o_win = o_win.reshape(1, n, C_Q_WIDTH)
        o_cmp = (gview[:, :, 0] * o_cmp.astype(F32)).reshape(1, n, C_Q_WIDTH)
    return linear_out([o_cmp, o_slc, o_win], w_o_bf, x, gate, 'sum3', tm), (new_win, cmp_kv, slc_kv)


def _dispatch_plan(eidx, n_blk):
    n = eidx.shape[0]
    tm = MOE_TILE
    chunk = LANES_V7X
    onehot = (eidx[:, :, None] == jnp.arange(N_EXPERTS)).astype(jnp.int32)
    sel = onehot.sum(axis=1).astype(F32).reshape(n // chunk, chunk, N_EXPERTS)
    tril = jnp.tril(jnp.ones((chunk, chunk), F32))
    within = jnp.einsum('ij,cjk->cik', tril, sel)
    chunk_tot = within[:, -1, :]
    before = jnp.cumsum(chunk_tot, axis=0) - chunk_tot
    rank = (within - sel + before[:, None, :]).reshape(n, N_EXPERTS).astype(jnp.int32)
    counts = jnp.sum(chunk_tot, axis=0).astype(jnp.int32)
    padded = (counts + tm - 1) // tm * tm
    ends = jnp.cumsum(padded)
    dest = jnp.sum(onehot * (ends - padded + rank)[:, None, :], axis=-1)
    blk_start = jnp.arange(n_blk, dtype=jnp.int32) * tm
    blk_e = jnp.minimum(jnp.sum(ends[None, :] <= blk_start[:, None], axis=1), N_EXPERTS - 1).astype(jnp.int32)
    n_used = (ends[-1] // tm).astype(jnp.int32).reshape(1)
    keys = (eidx.astype(jnp.int32) * n + jnp.arange(n, dtype=jnp.int32)[:, None]).reshape(-1)
    sorted_tok = jnp.sort(keys) % n
    shift = (ends - padded) - (jnp.cumsum(counts) - counts)
    dense = jnp.arange(n_blk * tm, dtype=jnp.int32) - jnp.repeat(shift[blk_e], tm)
    row_tok = sorted_tok[jnp.clip(dense, 0, n * TOP_K - 1)]
    return dest, row_tok, blk_e, n_used


def _moe(groups, g, router_w, router_b, w_gate, w_up, w_down, layer, sg_bf, su_bf, sd_bf):
    d = groups[0][0].shape[-1]
    fronts = [moe_in(x, g, mods[0], mods[1], router_w, router_b, sg_bf, su_bf, sd_bf, tm) for x, mods, tm in groups]
    join = lambda parts, axis: parts[0] if len(parts) == 1 else jnp.concatenate(parts, axis=axis)
    h_all = join([f[0].reshape(-1, d) for f in fronts], 0)
    eidx = join([f[1] for f in fronts], 1).T
    ew = join([f[2] for f in fronts], 1).T
    n = h_all.shape[0]
    n_blk = (n * TOP_K + N_EXPERTS * (MOE_TILE - 1)) // MOE_TILE + 1
    dest, row_tok, blk_e, n_used = _dispatch_plan(eidx, n_blk)
    ys = grouped_experts(h_all[row_tok], blk_e, n_used, w_gate, w_up, w_down, layer)
    picked = ys[dest.T.reshape(-1)].reshape(TOP_K, n, d)
    outs, row0 = [], 0
    for (x, mods, tm), front in zip(groups, fronts):
        outs.append(moe_out(x, mods[2], picked, ew, front[3], min(tm, MOE_OUT_TILE), row0))
        row0 += x.shape[0] * x.shape[1]
    return outs


def kernel(x_prompt, x_sample, c_prompt, c_sample, cache_a_kv, cache_b_kv_w128, cache_b_kv_w512, cache_b_kv_w2048, cache_c_win_kv, cache_c_cmp_kv, cache_c_slc_kv, page_table, norm_g, final_g, ada_w, ada_b, a_w_qkv, a_w_o, a_sink, b_w_qkv, b_w_o, c_w_in, c_w_o, c_cmp_w1, c_cmp_b1, c_cmp_w2, c_cmp_pe, moe_router, moe_bias, moe_w_gate, moe_w_up, moe_w_down, shared_w_gate, shared_w_up, shared_w_down):
    bp, seq, d = x_prompt.shape
    ns = x_sample.shape[0]
    b_caches = (cache_b_kv_w128, cache_b_kv_w512, cache_b_kv_w2048)
    pos_p = jnp.arange(seq, dtype=jnp.int32)
    pos_s = PAST_LEN + jnp.arange(x_sample.shape[1], dtype=jnp.int32)
    tabs_p = _rope_tables(pos_p, seq)
    tabs_s = _rope_tables(pos_s, ns)

    mods = ada_mods(jnp.concatenate([c_prompt, c_sample], axis=0), ada_w, ada_b)
    mods_p = mods[:, :bp].reshape(DEPTH, bp, 6, 1, d)
    mods_s = mods[:, bp:].reshape(DEPTH, 1, ns, 6, d)

    per = bp // PROMPT_CHAINS
    chains = [slice(c * per, (c + 1) * per) for c in range(PROMPT_CHAINS)]
    xps = [x_prompt[c] for c in chains]
    xs = x_sample.reshape(1, ns, d)
    st_p = {0: [], 1: [], 2: []}
    st_s = {0: [], 1: [], 2: []}
    for l in range(DEPTH):
        kind, slot = LAYER_KIND[l], LAYER_SLOT[l]
        mps = [[mods_p[l, c, i] for i in range(6)] for c in chains]
        ms = [mods_s[l, :, :, i] for i in range(6)]
        g_mix, g_moe = norm_g[l, 0], norm_g[l, 1]
        if kind == 0:
            w_in, w_o = a_w_qkv[slot].astype(BF16), a_w_o[slot].astype(BF16)
            res = [_mixer_a(xp, mp[:3], g_mix, w_in, w_o, a_sink[slot], tabs_p, ROW_TILE, None) for xp, mp in zip(xps, mps)]
            xs, ss = _mixer_a(xs, ms[:3], g_mix, w_in, w_o, a_sink[slot], tabs_s, ns, cache_a_kv[slot])
        elif kind == 1:
            w_in, w_o = b_w_qkv[slot].astype(BF16), b_w_o[slot].astype(BF16)
            res = [_mixer_b(xp, mp[:3], g_mix, w_in, w_o, tabs_p, ROW_TILE, None) for xp, mp in zip(xps, mps)]
            xs, ss = _mixer_b(xs, ms[:3], g_mix, w_in, w_o, tabs_s, ns, tuple(buf[slot] for buf in b_caches))
        else:
            w_in, w_o = c_w_in[slot].astype(BF16), c_w_o[slot].astype(BF16)
            cargs = (c_cmp_w1[slot], c_cmp_b1[slot], c_cmp_w2[slot], c_cmp_pe[slot])
            res = [_mixer_c(xp, mp[:3], g_mix, w_in, w_o, *cargs, tabs_p, pos_p, ROW_TILE, None) for xp, mp in zip(xps, mps)]
            xs, ss = _mixer_c(xs, ms[:3], g_mix, w_in, w_o, *cargs, tabs_s, pos_s, ns,
                              (cache_c_win_kv[slot], cache_c_cmp_kv[slot], cache_c_slc_kv[slot], page_table))
        xps = [r[0] for r in res]
        states = [r[1] for r in res]
        cat = lambda parts: jnp.concatenate(parts, axis=0)
        st_p[kind].append(tuple(cat([s[i] for s in states]) for i in range(len(states[0])))
                          if isinstance(states[0], tuple) else cat(states))
        st_s[kind].append(ss)
        moe_args = (g_moe, moe_router[l], moe_bias[l], moe_w_gate, moe_w_up, moe_w_down, l,
                    shared_w_gate[l].astype(BF16), shared_w_up[l].astype(BF16), shared_w_down[l].astype(BF16))
        for c in range(PROMPT_CHAINS - 1):
            (xps[c],) = _moe([(xps[c], mps[c][3:], ROW_TILE)], *moe_args)
        xps[-1], xs = _moe([(xps[-1], mps[-1][3:], ROW_TILE), (xs, ms[3:], ns)], *moe_args)
    y_prompt = jnp.concatenate([final_norm(xp, final_g, ROW_TILE) for xp in xps], axis=0)
    y_sample = final_norm(xs, final_g, ns).reshape(x_sample.shape)

    outs = [y_prompt, y_sample, jnp.stack(st_p[0]), jnp.stack(st_s[0])]
    for i in range(N_GROUPS_B):
        outs += [jnp.stack([s[i] for s in st_p[1]]), jnp.stack([s[i] for s in st_s[1]])]
    outs += [jnp.stack([s[0] for s in st_p[2]]), jnp.stack([s[0] for s in st_s[2]])]
    outs += [jnp.stack([s[1] for s in st_p[2]]), jnp.stack([s[1] for s in st_s[2]])]
    outs += [jnp.stack([s[2] for s in st_p[2]]), jnp.stack([s[2] for s in st_s[2]])]
    return tuple(outs)
```
